```python
import math
import jax
import jax.numpy as jnp
from jax import lax
import numpy as np

D_MODEL = 1024
BATCH = 8
SEQ = 4096
DEPTH = 2

GRID_W = 64
CTX_LEN = 256
RMS_EPS = 1e-6
LN_EPS = 1e-5

HY_WIDTH = D_MODEL
HY_ORDER = 2
HY_BANDS = 16
HY_EMB = 1 + 2 * HY_BANDS
HY_FILTER_HIDDEN = 64
HY_MIN_DECAY = -math.log(1e-2) / 1.5
HY_MAX_DECAY = -math.log(1e-2) / 0.3
SHORT_CONV = 3

HEAD_DIM = 128
ATT_HEADS = D_MODEL // HEAD_DIM
ATT_KV_HEADS = 2
ATT_WIDTH = ATT_HEADS * HEAD_DIM
KV_WIDTH = ATT_KV_HEADS * HEAD_DIM
AXIS_DIM = HEAD_DIM // 2
ROPE_THETA = 10000.0
Q_BLOCK = 128

GM_WIDTH = D_MODEL
GM_GROUPS = 8
GM_GROUP_DIM = GM_WIDTH // GM_GROUPS
CHUNK = 128

N_BRANCH = 3
OFF_HY = 0
OFF_HY_GATE = OFF_HY + (HY_ORDER + 1) * HY_WIDTH
OFF_Q = OFF_HY_GATE + HY_WIDTH
OFF_K = OFF_Q + ATT_WIDTH
OFF_V = OFF_K + KV_WIDTH
OFF_ATT_GATE = OFF_V + KV_WIDTH
OFF_GM = OFF_ATT_GATE + ATT_WIDTH
OFF_GM_GATE = OFF_GM + 2 * GM_WIDTH
OFF_MERGE = OFF_GM_GATE + GM_WIDTH
IN_WIDTH = OFF_MERGE + N_BRANCH * D_MODEL

kernel_name = 'hybrid_hyena_gqa_gmlp_dit'


def rms_norm(x, g):
    xf = x.astype(jnp.float32)
    y = xf * lax.rsqrt(jnp.mean(xf * xf, axis=-1, keepdims=True) + RMS_EPS)
    return (y * g.astype(jnp.float32)).astype(x.dtype)


def layer_norm(x, g, b):
    xf = x.astype(jnp.float32)
    xc = xf - jnp.mean(xf, axis=-1, keepdims=True)
    var = jnp.mean(xc * xc, axis=-1, keepdims=True)
    y = xc * lax.rsqrt(var + LN_EPS) * g.astype(jnp.float32) + b.astype(jnp.float32)
    return y.astype(x.dtype)


def short_conv3(x, w, b):
    xp = jnp.pad(x, ((0, 0), (1, 1), (0, 0)))
    return xp[:, :-2] * w[0] + xp[:, 1:-1] * w[1] + xp[:, 2:] * w[2] + b


def hyena_filter_spectra(L, lp):
    f32 = jnp.float32
    t = jnp.linspace(0.0, 1.0, L, dtype=f32)
    bands = jnp.linspace(1e-4, HY_BANDS - 1, HY_BANDS, dtype=f32)
    ang = (2.0 * math.pi / L) * jnp.arange(L, dtype=f32)[:, None] * bands[None, :]
    z = jnp.concatenate([t[:, None], jnp.cos(ang), -jnp.sin(ang)], axis=-1)
    freq = lp['hy_freq'].astype(f32)
    hdn = jnp.sin(freq[0] * (z @ lp['hy_w1'].astype(f32) + lp['hy_b1'].astype(f32)))
    hdn = jnp.sin(freq[1] * (hdn @ lp['hy_w2'].astype(f32) + lp['hy_b2'].astype(f32)))
    h = (hdn @ lp['hy_w3'].astype(f32)).reshape(L, HY_ORDER, 2, HY_WIDTH)
    h = h * jnp.exp(-t[:, None, None, None] * jnp.abs(lp['hy_decay'].astype(f32)))
    fwd, bwd = h[:, :, 0], h[:, :, 1]
    k = jnp.concatenate([fwd[:1] + bwd[:1], fwd[1:], jnp.zeros_like(fwd[:1]),
                         jnp.flip(bwd[1:], axis=0)], axis=0)
    return jnp.fft.rfft(k, axis=0)


def long_conv(z, kf):
    L = z.shape[1]
    zf = jnp.fft.rfft(z.astype(jnp.float32), n=2 * L, axis=1)
    y = jnp.fft.irfft(zf * kf[None], n=2 * L, axis=1)[:, :L]
    return y.astype(z.dtype)


def hyena_branch(p, lp):
    L = p.shape[1]
    p = short_conv3(p, lp['hy_conv_w'], lp['hy_conv_b'])
    x1, x2, v = jnp.split(p, HY_ORDER + 1, axis=-1)
    kf = hyena_filter_spectra(L, lp)
    bias = lp['hy_bias']
    z = x1 * (long_conv(v, kf[:, 0]) + bias[0] * v)
    z = x2 * (long_conv(z, kf[:, 1]) + bias[1] * z)
    return z


def axial_rope_tables(rows):
    f32 = jnp.float32
    row = jnp.repeat(jnp.arange(rows, dtype=f32), GRID_W)
    col = jnp.tile(jnp.arange(GRID_W, dtype=f32), rows)
    inv = jnp.power(ROPE_THETA, -jnp.arange(0, AXIS_DIM, 2, dtype=f32) / AXIS_DIM)
    ar = row[:, None] * inv
    ac = col[:, None] * inv
    return (jnp.cos(ar), jnp.sin(ar), jnp.cos(ac), jnp.sin(ac))


def rotate_half(x, cos, sin):
    m = x.shape[-1] // 2
    x1, x2 = x[..., :m], x[..., m:]
    cos = cos[None, :, None, :]
    sin = sin[None, :, None, :]
    return jnp.concatenate([x1 * cos - x2 * sin, x2 * cos + x1 * sin], axis=-1)


def apply_axial_rope(x, rope):
    cos_r, sin_r, cos_c, sin_c = rope
    xf = x.astype(jnp.float32)
    y = jnp.concatenate([rotate_half(xf[..., :AXIS_DIM], cos_r, sin_r),
                         rotate_half(xf[..., AXIS_DIM:], cos_c, sin_c)], axis=-1)
    return y.astype(x.dtype)


def block_attention(q, k, v):
    B, Lq, H, Dh = q.shape
    Hkv = k.shape[2]
    G = H // Hkv
    nb = Lq // Q_BLOCK
    qb = q.reshape(B, nb, Q_BLOCK, Hkv, G, Dh).transpose(1, 0, 2, 3, 4, 5)
    scale = Dh ** -0.5

    def one_block(qi):
        s = jnp.einsum('bqhgd,bkhd->bhgqk', qi, k, preferred_element_type=jnp.float32) * scale
        p = jax.nn.softmax(s, axis=-1).astype(v.dtype)
        return jnp.einsum('bhgqk,bkhd->bqhgd', p, v)

    o = lax.map(one_block, qb)
    return o.transpose(1, 0, 2, 3, 4, 5).reshape(B, Lq, H * Dh)


def gmlp_branch(p, lp):
    uv = jax.nn.gelu(p, approximate=False)
    u, v = jnp.split(uv, 2, axis=-1)
    v = layer_norm(v, lp['gm_ln_g'], lp['gm_ln_b'])
    B, L, _ = v.shape
    v = v.reshape(B, L // CHUNK, CHUNK, GM_GROUPS, GM_GROUP_DIM)
    y = jnp.einsum('gpq,bnqgc->bnpgc', lp['gm_ws'], v) + lp['gm_bs'].T[None, None, :, :, None]
    return u * y.reshape(B, L, GM_WIDTH)


def gated_merge(proj, hy, att, gm, lp):
    silu = jax.nn.silu
    y_hy = (hy * silu(proj[..., OFF_HY_GATE:OFF_Q])) @ lp['w_hy_o']
    y_at = (att * silu(proj[..., OFF_ATT_GATE:OFF_GM])) @ lp['w_att_o']
    y_gm = (gm * silu(proj[..., OFF_GM_GATE:OFF_MERGE])) @ lp['w_gm_o']
    g_hy, g_at, g_gm = jnp.split(jax.nn.sigmoid(proj[..., OFF_MERGE:]), N_BRANCH, axis=-1)
    return (g_hy * y_hy + g_at * y_at + g_gm * y_gm) @ lp['w_out']


def latent_mixer(proj, kc, vc, rope, lp):
    B, L, _ = proj.shape
    q = rms_norm(proj[..., OFF_Q:OFF_K].reshape(B, L, ATT_HEADS, HEAD_DIM), lp['q_gain'])
    k = rms_norm(proj[..., OFF_K:OFF_V].reshape(B, L, ATT_KV_HEADS, HEAD_DIM), lp['k_gain'])
    v = proj[..., OFF_V:OFF_ATT_GATE].reshape(B, L, ATT_KV_HEADS, HEAD_DIM)
    q = apply_axial_rope(q, rope)
    k = apply_axial_rope(k, rope)
    att = block_attention(q, jnp.concatenate([kc, k], axis=1), jnp.concatenate([vc, v], axis=1))
    hy = hyena_branch(proj[..., OFF_HY:OFF_HY_GATE], lp)
    gm = gmlp_branch(proj[..., OFF_GM:OFF_GM_GATE], lp)
    return gated_merge(proj, hy, att, gm, lp)


def context_mixer(projc, kc, vc, lp):
    B, C, _ = projc.shape
    q = rms_norm(projc[..., OFF_Q:OFF_K].reshape(B, C, ATT_HEADS, HEAD_DIM), lp['q_gain'])
    att = block_attention(q, kc, vc)
    hy = hyena_branch(projc[..., OFF_HY:OFF_HY_GATE], lp)
    gm = gmlp_branch(projc[..., OFF_GM:OFF_GM_GATE], lp)
    return gated_merge(projc, hy, att, gm, lp)


def hybrid_layer(x, xc, c, c_ctx, rope, lp, ctx_out):
    D = D_MODEL
    w_mod, b_mod = lp['w_mod'], lp['b_mod']
    mod = jax.nn.silu(c) @ w_mod + b_mod
    shift, scale, gate = jnp.split(mod[:, None, :], 3, axis=-1)
    h = rms_norm(x, lp['g_pre']) * (1 + scale) + shift
    n_mod = 3 * D if ctx_out else 2 * D
    mc = jax.nn.silu(c_ctx) @ w_mod[:, :n_mod] + b_mod[:n_mod]
    hc = rms_norm(xc, lp['g_pre']) * (1 + mc[D:2 * D]) + mc[:D]

    w_in = lp['w_in']
    proj = h @ w_in
    B, C, _ = hc.shape
    if ctx_out:
        projc = hc @ w_in
        kvc = projc[..., OFF_K:OFF_ATT_GATE]
    else:
        kvc = hc @ w_in[:, OFF_K:OFF_ATT_GATE]
    kc = rms_norm(kvc[..., :KV_WIDTH].reshape(B, C, ATT_KV_HEADS, HEAD_DIM), lp['k_gain'])
    vc = kvc[..., KV_WIDTH:].reshape(B, C, ATT_KV_HEADS, HEAD_DIM)

    y = latent_mixer(proj, kc, vc, rope, lp)
    x_new = x + gate * rms_norm(y, lp['g_post'])
    if ctx_out:
        yc = context_mixer(projc, kc, vc, lp)
        xc_new = xc + mc[2 * D:] * rms_norm(yc, lp['g_post'])
    else:
        xc_new = xc
    return x_new, xc_new


def setup_inputs(seed: int = 0) -> dict:
    key = jax.random.key(seed)
    ks = jax.random.split(key, 32)
    f32 = jnp.float32
    D = D_MODEL
    H = HY_FILTER_HIDDEN

    def nrm(k, shape, s):
        return s * jax.random.normal(k, shape, f32)

    return {
        'x': nrm(ks[0], (BATCH, SEQ, D), 1.0),
        'c': nrm(ks[1], (BATCH, D), 1.0),
        'ctx': nrm(ks[2], (BATCH, CTX_LEN, D), 1.0),
        'c_ctx': nrm(ks[3], (D,), 1.0),
        'w_mod': nrm(ks[4], (DEPTH, D, 3 * D), 0.5 * D ** -0.5),
        'b_mod': nrm(ks[5], (DEPTH, 3 * D), 0.01),
        'g_pre': 1.0 + nrm(ks[6], (DEPTH, D), 0.02),
        'g_post': 1.0 + nrm(ks[7], (DEPTH, D), 0.02),
        'w_in': nrm(ks[8], (DEPTH, D, IN_WIDTH), D ** -0.5),
        'hy_conv_w': nrm(ks[9], (DEPTH, SHORT_CONV, (HY_ORDER + 1) * HY_WIDTH), SHORT_CONV ** -0.5),
        'hy_conv_b': nrm(ks[10], (DEPTH, (HY_ORDER + 1) * HY_WIDTH), 0.01),
        'hy_w1': nrm(ks[11], (DEPTH, HY_EMB, H), HY_EMB ** -0.5),
        'hy_b1': nrm(ks[12], (DEPTH, H), 0.1),
        'hy_w2': nrm(ks[13], (DEPTH, H, H), H ** -0.5),
        'hy_b2': nrm(ks[14], (DEPTH, H), 0.1),
        'hy_w3': nrm(ks[15], (DEPTH, H, HY_ORDER * 2 * HY_WIDTH), 0.02),
        'hy_freq': 1.0 + nrm(ks[16], (DEPTH, 2, H), 0.1),
        'hy_decay': jax.random.uniform(ks[17], (DEPTH, HY_ORDER, 2, HY_WIDTH), f32, HY_MIN_DECAY, HY_MAX_DECAY),
        'hy_bias': nrm(ks[18], (DEPTH, HY_ORDER, HY_WIDTH), 1.0),
        'q_gain': 1.0 + nrm(ks[19], (DEPTH, HEAD_DIM), 0.02),
        'k_gain': 1.0 + nrm(ks[20], (DEPTH, HEAD_DIM), 0.02),
        'gm_ln_g': 1.0 + nrm(ks[21], (DEPTH, GM_WIDTH), 0.02),
        'gm_ln_b': nrm(ks[22], (DEPTH, GM_WIDTH), 0.01),
        'gm_ws': nrm(ks[23], (DEPTH, GM_GROUPS, CHUNK, CHUNK), CHUNK ** -0.5),
        'gm_bs': 1.0 + nrm(ks[24], (DEPTH, GM_GROUPS, CHUNK), 0.01),
        'w_hy_o': nrm(ks[25], (DEPTH, HY_WIDTH, D), HY_WIDTH ** -0.5),
        'w_att_o': nrm(ks[26], (DEPTH, ATT_WIDTH, D), ATT_WIDTH ** -0.5),
        'w_gm_o': nrm(ks[27], (DEPTH, GM_WIDTH, D), GM_WIDTH ** -0.5),
        'w_out': nrm(ks[28], (DEPTH, D, D), D ** -0.5),
    }


def reference(x, c, ctx, c_ctx, w_mod, b_mod, g_pre, g_post, w_in, hy_conv_w, hy_conv_b,
              hy_w1, hy_b1, hy_w2, hy_b2, hy_w3, hy_freq, hy_decay, hy_bias, q_gain, k_gain,
              gm_ln_g, gm_ln_b, gm_ws, gm_bs, w_hy_o, w_att_o, w_gm_o, w_out):
    ROWS = x.shape[1] // GRID_W
    rope = axial_rope_tables(ROWS)
    xc = ctx
    for i in range(DEPTH):
        lp = {
            'w_mod': w_mod[i], 'b_mod': b_mod[i], 'g_pre': g_pre[i], 'g_post': g_post[i],
            'w_in': w_in[i], 'hy_conv_w': hy_conv_w[i], 'hy_conv_b': hy_conv_b[i],
            'hy_w1': hy_w1[i], 'hy_b1': hy_b1[i], 'hy_w2': hy_w2[i], 'hy_b2': hy_b2[i],
            'hy_w3': hy_w3[i], 'hy_freq': hy_freq[i], 'hy_decay': hy_decay[i], 'hy_bias': hy_bias[i],
            'q_gain': q_gain[i], 'k_gain': k_gain[i], 'gm_ln_g': gm_ln_g[i], 'gm_ln_b': gm_ln_b[i],
            'gm_ws': gm_ws[i], 'gm_bs': gm_bs[i], 'w_hy_o': w_hy_o[i], 'w_att_o': w_att_o[i],
            'w_gm_o': w_gm_o[i], 'w_out': w_out[i],
        }
        x, xc = hybrid_layer(x, xc, c, c_ctx, rope, lp, i < DEPTH - 1)
    return x
```

```python
import functools
import math

import numpy as np
import jax
import jax.numpy as jnp
from jax import lax
from jax.experimental import pallas as pl
from jax.experimental.pallas import tpu as pltpu

f32 = jnp.float32
bf16 = jnp.bfloat16

D_MODEL = 1024
DEPTH = 2
GRID_W = 64
RMS_EPS = 1e-6
LN_EPS = 1e-5

HY_WIDTH = D_MODEL
HY_ORDER = 2
HY_BANDS = 16
HY_EMB = 1 + 2 * HY_BANDS
HY_FILTER_HIDDEN = 64

HEAD_DIM = 128
ATT_HEADS = D_MODEL // HEAD_DIM
ATT_KV_HEADS = 2
ATT_GROUP = ATT_HEADS // ATT_KV_HEADS
ATT_WIDTH = ATT_HEADS * HEAD_DIM
KV_WIDTH = ATT_KV_HEADS * HEAD_DIM
AXIS_DIM = HEAD_DIM // 2
ROPE_THETA = 10000.0

GM_WIDTH = D_MODEL
GM_GROUPS = 8
GM_GROUP_DIM = GM_WIDTH // GM_GROUPS
CHUNK = 128

OFF_HY = 0
OFF_HY_GATE = OFF_HY + (HY_ORDER + 1) * HY_WIDTH
OFF_Q = OFF_HY_GATE + HY_WIDTH
OFF_K = OFF_Q + ATT_WIDTH
OFF_V = OFF_K + KV_WIDTH
OFF_ATT_GATE = OFF_V + KV_WIDTH
OFF_GM = OFF_ATT_GATE + ATT_WIDTH
OFF_GM_GATE = OFF_GM + 2 * GM_WIDTH
OFF_MERGE = OFF_GM_GATE + GM_WIDTH
IN_WIDTH = OFF_MERGE + 3 * D_MODEL

P_HY = 0
P_HY_GATE = 3 * D_MODEL
P_Q = 4 * D_MODEL
P_ATT_GATE = 5 * D_MODEL
P_GM = 6 * D_MODEL
P_GM_GATE = 8 * D_MODEL
P_MERGE = 9 * D_MODEL
P_K = 12 * D_MODEL
P_V = P_K + KV_WIDTH
_COL_PERM = np.concatenate([
    np.arange(OFF_HY, OFF_HY_GATE), np.arange(OFF_HY_GATE, OFF_Q), np.arange(OFF_Q, OFF_K),
    np.arange(OFF_ATT_GATE, OFF_GM), np.arange(OFF_GM, OFF_GM_GATE), np.arange(OFF_GM_GATE, OFF_MERGE),
    np.arange(OFF_MERGE, IN_WIDTH), np.arange(OFF_K, OFF_V), np.arange(OFF_V, OFF_ATT_GATE)])

LANES = 128
SUBLANES = 8
FFT_COLS = 256
VMEM_LIMIT = 52 * 1024 * 1024

_SM_SCALE_LOG2E = (HEAD_DIM ** -0.5) * math.log2(math.e)


def _params(sem, vmem=VMEM_LIMIT):
    return pltpu.CompilerParams(dimension_semantics=sem, vmem_limit_bytes=vmem)


def _silu(x):
    return x * jax.nn.sigmoid(x)


def _mod_kernel(cc_ref, w_ref, b_ref, o_ref):
    s = _silu(cc_ref[...]).astype(bf16)
    o_ref[0] = jnp.dot(s, w_ref[0].astype(bf16), preferred_element_type=f32) + b_ref[0]


def _modulation(cc, w_mod, b_mod):
    R, D = cc.shape
    tn = D
    return pl.pallas_call(
        _mod_kernel,
        grid=(DEPTH, 3 * D // tn),
        in_specs=[pl.BlockSpec((R, D), lambda i, n: (0, 0)),
                  pl.BlockSpec((1, D, tn), lambda i, n: (i, 0, n)),
                  pl.BlockSpec((1, 1, tn), lambda i, n: (i, 0, n))],
        out_specs=pl.BlockSpec((1, R, tn), lambda i, n: (i, 0, n)),
        out_shape=jax.ShapeDtypeStruct((DEPTH, R, 3 * D), f32),
        compiler_params=_params(("parallel", "parallel")),
    )(cc, w_mod, b_mod.reshape(DEPTH, 1, 3 * D))


def _inproj_kernel(x_ref, mod_ref, g_ref, w_ref, o_ref, h_scr):
    D = x_ref.shape[2]

    @pl.when(pl.program_id(2) == 0)
    def _():
        x = x_ref[0]
        r = lax.rsqrt(jnp.mean(x * x, axis=-1, keepdims=True) + RMS_EPS)
        shift = mod_ref[0, :, 0:D]
        scale = mod_ref[0, :, D:2 * D]
        h_scr[...] = ((x * r) * g_ref[...] * (1.0 + scale) + shift).astype(bf16)

    o_ref[0] = jnp.dot(h_scr[...], w_ref[...], preferred_element_type=f32).astype(o_ref.dtype)


def _inproj(x, mod, g_pre, w, tm, tn):
    B, L, D = x.shape
    N = w.shape[1]
    return pl.pallas_call(
        _inproj_kernel,
        grid=(B, L // tm, N // tn),
        in_specs=[pl.BlockSpec((1, tm, D), lambda b, m, n: (b, m, 0)),
                  pl.BlockSpec((1, 1, 3 * D), lambda b, m, n: (b, 0, 0)),
                  pl.BlockSpec((1, D), lambda b, m, n: (0, 0)),
                  pl.BlockSpec((D, tn), lambda b, m, n: (0, n))],
        out_specs=pl.BlockSpec((1, tm, tn), lambda b, m, n: (b, m, n)),
        out_shape=jax.ShapeDtypeStruct((B, L, N), bf16),
        scratch_shapes=[pltpu.VMEM((tm, D), bf16)],
        compiler_params=_params(("parallel", "parallel", "arbitrary")),
    )(x, mod, g_pre.reshape(1, D), w)


def _rope(x, cos, sin):
    lane = lax.broadcasted_iota(jnp.int32, x.shape, 1)
    first = (lane & (AXIS_DIM // 2)) == 0
    rot = jnp.where(first, pltpu.roll(x, HEAD_DIM - AXIS_DIM // 2, 1), pltpu.roll(x, AXIS_DIM // 2, 1))
    return x * cos + rot * sin


def _head_norm(x, gain):
    r = lax.rsqrt(jnp.mean(x * x, axis=-1, keepdims=True) + RMS_EPS)
    return x * r * gain


def _kprep_kernel(k_ref, cos_ref, sin_ref, gain_ref, o_ref):
    for h in range(ATT_KV_HEADS):
        sl = slice(h * HEAD_DIM, (h + 1) * HEAD_DIM)
        kn = _head_norm(k_ref[0, :, sl].astype(f32), gain_ref[...])
        o_ref[0, :, sl] = _rope(kn, cos_ref[...], sin_ref[...]).astype(bf16)


def _kprep(proj, col_off, cos, sin, gain, tm):
    B, L, _ = proj.shape
    cb = col_off // KV_WIDTH
    return pl.pallas_call(
        _kprep_kernel,
        grid=(B, L // tm),
        in_specs=[pl.BlockSpec((1, tm, KV_WIDTH), lambda b, m: (b, m, cb)),
                  pl.BlockSpec((tm, HEAD_DIM), lambda b, m: (m, 0)),
                  pl.BlockSpec((tm, HEAD_DIM), lambda b, m: (m, 0)),
                  pl.BlockSpec((1, HEAD_DIM), lambda b, m: (0, 0))],
        out_specs=pl.BlockSpec((1, tm, KV_WIDTH), lambda b, m: (b, m, 0)),
        out_shape=jax.ShapeDtypeStruct((B, L, KV_WIDTH), bf16),
        compiler_params=_params(("parallel", "parallel")),
    )(proj, cos, sin, gain.reshape(1, HEAD_DIM))


def _attn_kernel(q_ref, k_ref, v_ref, cos_ref, sin_ref, gain_ref, gate_ref, o_ref):
    k = k_ref[0]
    v = v_ref[0]
    for h in range(ATT_GROUP):
        sl = slice(h * HEAD_DIM, (h + 1) * HEAD_DIM)
        qn = _head_norm(q_ref[0, :, sl].astype(f32), gain_ref[...])
        q = _rope(qn, cos_ref[...], sin_ref[...]).astype(bf16)
        s = lax.dot_general(q, k, (((1,), (1,)), ((), ())), preferred_element_type=f32)
        m = jnp.max(s, axis=-1, keepdims=True)
        p = jnp.exp2((s - m) * _SM_SCALE_LOG2E)
        l = jnp.sum(p, axis=-1, keepdims=True)
        o = jnp.dot(p.astype(bf16), v, preferred_element_type=f32) / l
        o_ref[0, :, sl] = (o * _silu(gate_ref[0, :, sl].astype(f32))).astype(bf16)


def _attention(proj, k_all, v_all, cos, sin, gain, tq):
    B, L, _ = proj.shape
    Lk = k_all.shape[1]
    gw = ATT_GROUP * HEAD_DIM
    qb, gb = P_Q // gw, P_ATT_GATE // gw
    return pl.pallas_call(
        _attn_kernel,
        grid=(B, ATT_KV_HEADS, L // tq),
        in_specs=[pl.BlockSpec((1, tq, gw), lambda b, h, i: (b, i, qb + h)),
                  pl.BlockSpec((1, Lk, HEAD_DIM), lambda b, h, i: (b, 0, h)),
                  pl.BlockSpec((1, Lk, HEAD_DIM), lambda b, h, i: (b, 0, h)),
                  pl.BlockSpec((tq, HEAD_DIM), lambda b, h, i: (i, 0)),
                  pl.BlockSpec((tq, HEAD_DIM), lambda b, h, i: (i, 0)),
                  pl.BlockSpec((1, HEAD_DIM), lambda b, h, i: (0, 0)),
                  pl.BlockSpec((1, tq, gw), lambda b, h, i: (b, i, gb + h))],
        out_specs=pl.BlockSpec((1, tq, gw), lambda b, h, i: (b, i, h)),
        out_shape=jax.ShapeDtypeStruct((B, L, ATT_WIDTH), bf16),
        compiler_params=_params(("parallel", "parallel", "parallel")),
    )(proj, k_all, v_all, cos, sin, gain.reshape(1, HEAD_DIM), proj)


def _gmlp_kernel(p_ref, gate_ref, lng_ref, lnb_ref, ws_ref, bs_ref, o_ref):
    tm = p_ref.shape[1]
    p = p_ref[0].astype(f32)
    uv = 0.5 * p * (1.0 + lax.erf(p * (2.0 ** -0.5)))
    u = uv[:, :GM_WIDTH]
    v = uv[:, GM_WIDTH:]
    vc = v - jnp.mean(v, axis=-1, keepdims=True)
    var = jnp.mean(vc * vc, axis=-1, keepdims=True)
    vn = (vc * lax.rsqrt(var + LN_EPS) * lng_ref[...] + lnb_ref[...]).astype(bf16)
    ug = u * _silu(gate_ref[0].astype(f32))
    for n in range(tm // CHUNK):
        rows = slice(n * CHUNK, (n + 1) * CHUNK)
        for g in range(GM_GROUPS):
            cols = slice(g * GM_GROUP_DIM, (g + 1) * GM_GROUP_DIM)
            y = jnp.dot(ws_ref[g], vn[rows, cols], preferred_element_type=f32) + bs_ref[:, cols]
            o_ref[0, rows, cols] = (ug[rows, cols] * y).astype(bf16)


def _gmlp(proj, ln_g, ln_b, ws, bs_full, tm):
    B, L, _ = proj.shape
    pb, gb = P_GM // (2 * GM_WIDTH), P_GM_GATE // GM_WIDTH
    return pl.pallas_call(
        _gmlp_kernel,
        grid=(B, L // tm),
        in_specs=[pl.BlockSpec((1, tm, 2 * GM_WIDTH), lambda b, m: (b, m, pb)),
                  pl.BlockSpec((1, tm, GM_WIDTH), lambda b, m: (b, m, gb)),
                  pl.BlockSpec((1, GM_WIDTH), lambda b, m: (0, 0)),
                  pl.BlockSpec((1, GM_WIDTH), lambda b, m: (0, 0)),
                  pl.BlockSpec((GM_GROUPS, CHUNK, CHUNK), lambda b, m: (0, 0, 0)),
                  pl.BlockSpec((CHUNK, GM_WIDTH), lambda b, m: (0, 0))],
        out_specs=pl.BlockSpec((1, tm, GM_WIDTH), lambda b, m: (b, m, 0)),
        out_shape=jax.ShapeDtypeStruct((B, L, GM_WIDTH), bf16),
        compiler_params=_params(("parallel", "parallel")),
    )(proj, proj, ln_g.reshape(1, GM_WIDTH), ln_b.reshape(1, GM_WIDTH), ws, bs_full)


def _shortconv_kernel(p_ref, w_ref, b_ref, o_ref):
    L = p_ref.shape[1]
    x = p_ref[0].astype(f32)
    row = lax.broadcasted_iota(jnp.int32, x.shape, 0)
    xm = jnp.where(row == 0, 0.0, pltpu.roll(x, 1, 0))
    xp = jnp.where(row == L - 1, 0.0, pltpu.roll(x, L - 1, 0))
    o_ref[0] = (xm * w_ref[0:1, :] + x * w_ref[1:2, :] + xp * w_ref[2:3, :] + b_ref[...]).astype(bf16)


def _shortconv(proj, w, b, tc):
    B, L, _ = proj.shape
    W = w.shape[1]
    return pl.pallas_call(
        _shortconv_kernel,
        grid=(B, W // tc),
        in_specs=[pl.BlockSpec((1, L, tc), lambda b, c: (b, 0, c)),
                  pl.BlockSpec((3, tc), lambda b, c: (0, c)),
                  pl.BlockSpec((1, tc), lambda b, c: (0, c))],
        out_specs=pl.BlockSpec((1, L, tc), lambda b, c: (b, 0, c)),
        out_shape=jax.ShapeDtypeStruct((B, L, W), bf16),
        compiler_params=_params(("parallel", "parallel")),
    )(proj, w, b.reshape(1, W))


def _filter_embedding(L):
    n = np.arange(2 * L)
    pos = np.where(n < L, n, 2 * L - n)
    pos = np.where(n == L, 0, pos)
    t = np.linspace(0.0, 1.0, L)[pos]
    bands = np.linspace(1e-4, HY_BANDS - 1, HY_BANDS)
    ang = (2.0 * math.pi / L) * pos[:, None] * bands[None, :]
    z = np.zeros((2 * L, LANES), np.float64)
    z[:, 0] = t
    z[:, 1:1 + HY_BANDS] = np.cos(ang)
    z[:, 1 + HY_BANDS:HY_EMB] = -np.sin(ang)
    z[:, HY_EMB] = (n < L)
    z[:, HY_EMB + 1] = (n > L) | (n == 0)
    return jnp.asarray(z, f32)


def _filter_kernel(z_ref, w1_ref, b1_ref, w2_ref, b2_ref, freq_ref, w3f_ref, w3b_ref, dec_f_ref, dec_b_ref, o_ref):
    hp = lax.Precision.HIGHEST
    z = z_ref[...]
    t = z[:, 0:1]
    mf = z[:, HY_EMB:HY_EMB + 1]
    mb = z[:, HY_EMB + 1:HY_EMB + 2]
    h = jnp.sin(freq_ref[0:1, :] * (jnp.dot(z, w1_ref[...], precision=hp, preferred_element_type=f32) + b1_ref[...]))
    h = jnp.sin(freq_ref[1:2, :] * (jnp.dot(h, w2_ref[...], precision=hp, preferred_element_type=f32) + b2_ref[...]))
    hf = jnp.dot(h, w3f_ref[...], precision=hp, preferred_element_type=f32)
    hb = jnp.dot(h, w3b_ref[...], precision=hp, preferred_element_type=f32)
    o_ref[0] = (mf * hf * jnp.exp(-t * jnp.abs(dec_f_ref[0])) + mb * hb * jnp.exp(-t * jnp.abs(dec_b_ref[0])))


def _filters(L, w1, b1, w2, b2, w3, freq, decay, tc, tr):
    H = HY_FILTER_HIDDEN
    C = HY_WIDTH
    z = _filter_embedding(L)
    w1p = jnp.zeros((LANES, H), f32).at[:HY_EMB].set(w1)
    nct = C // tc
    dec = decay.reshape(HY_ORDER * 2, 1, C)
    return pl.pallas_call(
        _filter_kernel,
        grid=(HY_ORDER, nct, 2 * L // tr),
        in_specs=[pl.BlockSpec((tr, LANES), lambda o, c, r: (r, 0)),
                  pl.BlockSpec((LANES, H), lambda o, c, r: (0, 0)),
                  pl.BlockSpec((1, H), lambda o, c, r: (0, 0)),
                  pl.BlockSpec((H, H), lambda o, c, r: (0, 0)),
                  pl.BlockSpec((1, H), lambda o, c, r: (0, 0)),
                  pl.BlockSpec((2, H), lambda o, c, r: (0, 0)),
                  pl.BlockSpec((H, tc), lambda o, c, r: (0, (2 * o) * nct + c)),
                  pl.BlockSpec((H, tc), lambda o, c, r: (0, (2 * o + 1) * nct + c)),
                  pl.BlockSpec((1, 1, tc), lambda o, c, r: (2 * o, 0, c)),
                  pl.BlockSpec((1, 1, tc), lambda o, c, r: (2 * o + 1, 0, c))],
        out_specs=pl.BlockSpec((1, tr, tc), lambda o, c, r: (o, r, c)),
        out_shape=jax.ShapeDtypeStruct((HY_ORDER, 2 * L, C), f32),
        compiler_params=_params(("parallel", "parallel", "parallel")),
    )(z, w1p, b1.reshape(1, H), w2, b2.reshape(1, H), freq, w3, w3, dec, dec)


def _fft_dims(L):
    n2 = SUBLANES
    while (2 * n2) * (2 * n2) <= L:
        n2 *= 2
    return n2, L // n2, 2 * (L // n2), n2 + SUBLANES


def _stack(fr, fi):
    return np.block([[fr, -fi], [fi, fr]])


@functools.lru_cache(maxsize=None)
def _dft_tables(L):
    N2, N1h, N1, _ = _fft_dims(L)
    N = 2 * L
    k1 = np.arange(N1)
    a1 = -2.0 * np.pi * np.outer(k1, np.arange(N1)) / N1
    f1 = _stack(np.cos(a1[:, :N1h]), np.sin(a1[:, :N1h]))
    f1k = np.concatenate([np.cos(a1), np.sin(a1)], axis=0)
    f3 = _stack(np.cos(-a1[:, :N1h].T), np.sin(-a1[:, :N1h].T))
    n2 = np.arange(N2)
    a2 = -2.0 * np.pi * (np.outer(n2, n2)[None] / N2 + (k1[:, None, None] * n2[None, None, :]) / N)
    f2 = np.stack([_stack(np.cos(a), np.sin(a)) for a in a2])
    a2t = -np.transpose(a2, (0, 2, 1))
    g2 = np.stack([_stack(np.cos(a), np.sin(a)) for a in a2t])
    return tuple(jnp.asarray(m, bf16) for m in (f1, f1k, f2, g2, f3))


def _load_cols(ref, lead, rows):
    return jnp.concatenate([ref[lead + (t, rows, slice(None))] for t in range(ref.shape[-3])], axis=1)


def _store_cols(ref, lead, rows, val):
    for t in range(ref.shape[-3]):
        ref[lead + (t, rows, slice(None))] = val[:, t * LANES:(t + 1) * LANES]


def _spectrum_kernel(k_ref, f1k_ref, f2_ref, o_ref, a_scr, *, L):
    N2, N1h, N1, P = _fft_dims(L)
    T = a_scr.shape[1]
    inv_n = 1.0 / (2 * L)

    def stage1(n2, c):
        w = jnp.concatenate([k_ref[0, pl.ds(n2, N1, stride=N2), t * LANES:(t + 1) * LANES] for t in range(T)], axis=1)
        r = jnp.dot(f1k_ref[...], w.astype(bf16), preferred_element_type=f32)
        for ri in range(2):
            _store_cols(a_scr, (ri,), pl.ds(n2, N1, stride=P), r[ri * N1:(ri + 1) * N1])
        return c

    lax.fori_loop(0, N2, stage1, 0)

    def stage2(k1, c):
        rows = pl.ds(pl.multiple_of(k1 * P, SUBLANES), N2)
        a = jnp.concatenate([_load_cols(a_scr, (ri,), rows) for ri in range(2)], axis=0).astype(bf16)
        x = jnp.dot(f2_ref[k1], a, preferred_element_type=f32) * inv_n
        out_rows = pl.ds(pl.multiple_of(k1 * N2, N2), N2)
        o_ref[0, 0, out_rows, :] = x[:N2].astype(bf16)
        o_ref[0, 1, out_rows, :] = x[N2:].astype(bf16)
        return c

    lax.fori_loop(0, N1, stage2, 0)


def _spectrum(k, tc):
    _, N, C = k.shape
    L = N // 2
    N2, N1h, N1, P = _fft_dims(L)
    _, f1k, f2, _, _ = _dft_tables(L)
    return pl.pallas_call(
        functools.partial(_spectrum_kernel, L=L),
        grid=(HY_ORDER, C // tc),
        in_specs=[pl.BlockSpec((1, N, tc), lambda o, c: (o, 0, c)),
                  pl.BlockSpec(f1k.shape, lambda o, c: (0, 0)),
                  pl.BlockSpec(f2.shape, lambda o, c: (0, 0, 0))],
        out_specs=pl.BlockSpec((1, 2, N, tc), lambda o, c: (o, 0, 0, c)),
        out_shape=jax.ShapeDtypeStruct((HY_ORDER, 2, N, C), bf16),
        scratch_shapes=[pltpu.VMEM((2, tc // LANES, N1 * P, LANES), f32)],
        compiler_params=_params(("parallel", "parallel")),
    )(k, f1k, f2)


def _fftconv_kernel(*refs, L, chunk, has_gate):
    if has_gate:
        u_in, u_ep, g_ref, gate_ref, bias_ref, kf_ref, f1_ref, f2_ref, g2_ref, f3_ref, o_ref, z_scr, a_scr = refs
    else:
        u_in, u_ep, g_ref, bias_ref, kf_ref, f1_ref, f2_ref, g2_ref, f3_ref, o_ref, z_scr, a_scr = refs
    N2, N1h, N1, P = _fft_dims(L)
    J = L // chunk
    nb = chunk // N2
    j = pl.program_id(2)

    @pl.when(j < J)
    def _load():
        for ri in range(2):
            for blk in range(nb):
                rows = pl.ds(pl.multiple_of((j * nb + blk) * P, SUBLANES), N2)
                _store_cols(z_scr, (ri,), rows, u_in[ri, blk * N2:(blk + 1) * N2, :].astype(f32))

    @pl.when(j == J - 1)
    def _transform():
        def stage1(n2, c):
            rows = pl.ds(n2, N1h, stride=P)
            w = jnp.concatenate([_load_cols(z_scr, (ri,), rows) for ri in range(2)], axis=0).astype(bf16)
            r = jnp.dot(f1_ref[...], w, preferred_element_type=f32)
            for ri in range(2):
                _store_cols(a_scr, (ri,), pl.ds(n2, N1, stride=P), r[ri * N1:(ri + 1) * N1])
            return c

        lax.fori_loop(0, N2, stage1, 0)

        def middle(k1, c):
            rows = pl.ds(pl.multiple_of(k1 * P, SUBLANES), N2)
            a = jnp.concatenate([_load_cols(a_scr, (ri,), rows) for ri in range(2)], axis=0).astype(bf16)
            x = jnp.dot(f2_ref[k1], a, preferred_element_type=f32)
            krows = pl.ds(pl.multiple_of(k1 * N2, N2), N2)
            kr = kf_ref[0, 0, krows, :].astype(f32)
            ki = kf_ref[0, 1, krows, :].astype(f32)
            xr, xi = x[:N2], x[N2:]
            y = jnp.concatenate([xr * kr - xi * ki, xr * ki + xi * kr], axis=0).astype(bf16)
            b = jnp.dot(g2_ref[k1], y, preferred_element_type=f32)
            for ri in range(2):
                _store_cols(a_scr, (ri,), rows, b[ri * N2:(ri + 1) * N2])
            return c

        lax.fori_loop(0, N1, middle, 0)

        def stage3(n2, c):
            rows = pl.ds(n2, N1, stride=P)
            b = jnp.concatenate([_load_cols(a_scr, (ri,), rows) for ri in range(2)], axis=0).astype(bf16)
            y = jnp.dot(f3_ref[...], b, preferred_element_type=f32)
            for ri in range(2):
                _store_cols(z_scr, (ri,), pl.ds(n2, N1h, stride=P), y[ri * N1h:(ri + 1) * N1h])
            return c

        lax.fori_loop(0, N2, stage3, 0)

    @pl.when(j >= J)
    def _epilogue():
        for ri in range(2):
            for blk in range(nb):
                rows = pl.ds(pl.multiple_of(((j - J) * nb + blk) * P, SUBLANES), N2)
                y = _load_cols(z_scr, (ri,), rows)
                sl = slice(blk * N2, (blk + 1) * N2)
                u = u_ep[ri, sl, :].astype(f32)
                o = g_ref[ri, sl, :].astype(f32) * (y + bias_ref[...] * u)
                if has_gate:
                    o = o * _silu(gate_ref[ri, sl, :].astype(f32))
                o_ref[ri, sl, :] = o.astype(bf16)


def _fftconv(u, u_cb, g, g_cb, gate, gate_cb, bias, kf, order, tc):
    B, L, _ = u.shape
    C = HY_WIDTH
    N2, N1h, N1, P = _fft_dims(L)
    chunk = min(L, 256)
    J = L // chunk
    f1, _, f2, g2, f3 = _dft_tables(L)
    T = tc // LANES
    in_map = lambda cb: (lambda c, p, j: (p, jnp.minimum(j, J - 1), cb + c))
    ep_map = lambda cb: (lambda c, p, j: (p, jnp.maximum(j - J, 0), cb + c))
    const = lambda nd: (lambda c, p, j: (0,) * nd)
    once = pl.Buffered(1)
    ops = [u, u, g]
    specs = [pl.BlockSpec((2, chunk, tc), in_map(u_cb)),
             pl.BlockSpec((2, chunk, tc), ep_map(u_cb)),
             pl.BlockSpec((2, chunk, tc), ep_map(g_cb))]
    if gate is not None:
        ops.append(gate)
        specs.append(pl.BlockSpec((2, chunk, tc), ep_map(gate_cb)))
    ops += [bias.reshape(HY_ORDER, 1, C), kf, f1, f2, g2, f3]
    specs += [pl.BlockSpec((None, 1, tc), lambda c, p, j: (order, 0, c)),
              pl.BlockSpec((1, 2, 2 * L, tc), lambda c, p, j: (order, 0, 0, c), pipeline_mode=once),
              pl.BlockSpec(f1.shape, const(2), pipeline_mode=once),
              pl.BlockSpec(f2.shape, const(3), pipeline_mode=once),
              pl.BlockSpec(g2.shape, const(3), pipeline_mode=once),
              pl.BlockSpec(f3.shape, const(2), pipeline_mode=once)]
    return pl.pallas_call(
        functools.partial(_fftconv_kernel, L=L, chunk=chunk, has_gate=gate is not None),
        grid=(C // tc, B // 2, 2 * J),
        in_specs=specs,
        out_specs=pl.BlockSpec((2, chunk, tc), ep_map(0)),
        out_shape=jax.ShapeDtypeStruct((B, L, C), bf16),
        scratch_shapes=[pltpu.VMEM((2, T, N1h * P, LANES), f32),
                        pltpu.VMEM((2, T, N1 * P, LANES), f32)],
        compiler_params=_params(("parallel", "parallel", "arbitrary")),
    )(*ops)


def _merge_kernel(hy_ref, at_ref, gm_ref, mg_ref, x_ref, gate_ref, gpost_ref,
                  why_ref, wat_ref, wgm_ref, wout_ref, o_ref):
    D = x_ref.shape[2]
    acc = None
    for i, (br, w) in enumerate(((hy_ref, why_ref), (at_ref, wat_ref), (gm_ref, wgm_ref))):
        y = jnp.dot(br[0], w[...], preferred_element_type=f32)
        y = jax.nn.sigmoid(mg_ref[0, :, i * D:(i + 1) * D].astype(f32)) * y
        acc = y if acc is None else acc + y
    o = jnp.dot(acc.astype(bf16), wout_ref[...], preferred_element_type=f32)
    r = o * lax.rsqrt(jnp.mean(o * o, axis=-1, keepdims=True) + RMS_EPS) * gpost_ref[...]
    o_ref[0] = x_ref[0] + gate_ref[0] * r


def _merge(hy, at, gm, proj, x, mod, g_post, w_hy, w_at, w_gm, w_out, tm):
    B, L, D = x.shape
    mb = P_MERGE // (3 * D)
    row = lambda b, m: (b, m, 0)
    wspec = pl.BlockSpec((D, D), lambda b, m: (0, 0))
    return pl.pallas_call(
        _merge_kernel,
        grid=(B, L // tm),
        in_specs=[pl.BlockSpec((1, tm, D), row), pl.BlockSpec((1, tm, D), row), pl.BlockSpec((1, tm, D), row),
                  pl.BlockSpec((1, tm, 3 * D), lambda b, m: (b, m, mb)),
                  pl.BlockSpec((1, tm, D), row),
                  pl.BlockSpec((1, 1, D), lambda b, m: (b, 0, 2)),
                  pl.BlockSpec((1, D), lambda b, m: (0, 0)),
                  wspec, wspec, wspec, wspec],
        out_specs=pl.BlockSpec((1, tm, D), row),
        out_shape=jax.ShapeDtypeStruct((B, L, D), f32),
        compiler_params=_params(("parallel", "parallel")),
    )(hy, at, gm, proj, x, mod, g_post.reshape(1, D), w_hy, w_at, w_gm, w_out)


def _rope_tables(L):
    pos = np.arange(L)
    inv = np.power(ROPE_THETA, -np.arange(0, AXIS_DIM, 2) / AXIS_DIM)
    ar = (pos // GRID_W)[:, None] * inv
    ac = (pos % GRID_W)[:, None] * inv
    cos = np.concatenate([np.cos(ar), np.cos(ar), np.cos(ac), np.cos(ac)], axis=1)
    sin = np.concatenate([-np.sin(ar), np.sin(ar), -np.sin(ac), np.sin(ac)], axis=1)
    return jnp.asarray(cos, f32), jnp.asarray(sin, f32)


def _identity_rope(L):
    return jnp.ones((L, HEAD_DIM), f32), jnp.zeros((L, HEAD_DIM), f32)


def _hyena(proj, lp):
    B, L, _ = proj.shape
    tc = FFT_COLS
    nct = HY_WIDTH // tc
    hyc = _shortconv(proj, lp['hy_conv_w'], lp['hy_conv_b'], tc)
    k = _filters(L, lp['hy_w1'], lp['hy_b1'], lp['hy_w2'], lp['hy_b2'], lp['hy_w3'], lp['hy_freq'],
                 lp['hy_decay'], tc, min(2 * L, 512))
    kf = _spectrum(k, LANES)
    z1 = _fftconv(hyc, 2 * nct, hyc, 0, None, 0, lp['hy_bias'], kf, 0, tc)
    return _fftconv(z1, 0, hyc, nct, proj, P_HY_GATE // tc, lp['hy_bias'], kf, 1, tc)


def _mixer(proj, k_all, v_all, rope_q, lp, tq, tm_gm):
    att = _attention(proj, k_all, v_all, rope_q[0], rope_q[1], lp['q_gain'], tq)
    hy = _hyena(proj, lp)
    gm = _gmlp(proj, lp['gm_ln_g'], lp['gm_ln_b'], lp['gm_ws'], lp['gm_bs_full'], tm_gm)
    return hy, att, gm


def _layer(x, xc, mod_x, mod_c, rope, lp, ctx_out):
    B, L, D = x.shape
    C = xc.shape[1]
    w_in = lp['w_in']
    proj = _inproj(x, mod_x, lp['g_pre'], w_in, min(L, 1024), 1280)
    wc = w_in if ctx_out else w_in[:, P_K:]
    projc = _inproj(xc, mod_c, lp['g_pre'], wc, C, min(wc.shape[1], 1280))
    kv_off = P_K if ctx_out else 0
    ident = _identity_rope(C)
    kc = _kprep(projc, kv_off, ident[0], ident[1], lp['k_gain'], C)
    vc = projc[:, :, kv_off + KV_WIDTH:kv_off + 2 * KV_WIDTH]
    k = _kprep(proj, P_K, rope[0], rope[1], lp['k_gain'], min(L, 1024))
    k_all = jnp.concatenate([kc, k], axis=1)
    v_all = jnp.concatenate([vc, proj[:, :, P_V:P_V + KV_WIDTH]], axis=1)

    hy, att, gm = _mixer(proj, k_all, v_all, rope, lp, 256, 256)
    x_new = _merge(hy, att, gm, proj, x, mod_x, lp['g_post'], lp['w_hy_o'], lp['w_att_o'], lp['w_gm_o'],
                   lp['w_out'], min(L, 512))
    if not ctx_out:
        return x_new, xc
    hyc, attc, gmc = _mixer(projc, kc, vc, ident, lp, C, min(C, 256))
    xc_new = _merge(hyc, attc, gmc, projc, xc, mod_c, lp['g_post'], lp['w_hy_o'], lp['w_att_o'], lp['w_gm_o'],
                    lp['w_out'], C)
    return x_new, xc_new


def kernel(x, c, ctx, c_ctx, w_mod, b_mod, g_pre, g_post, w_in, hy_conv_w, hy_conv_b, hy_w1, hy_b1, hy_w2, hy_b2, hy_w3, hy_freq, hy_decay, hy_bias, q_gain, k_gain, gm_ln_g, gm_ln_b, gm_ws, gm_bs, w_hy_o, w_att_o, w_gm_o, w_out):
    B, L, D = x.shape
    rope = _rope_tables(L)

    R = -(-(B + 1) // (2 * SUBLANES)) * (2 * SUBLANES)
    cc = jnp.zeros((R, D), f32).at[:B].set(c).at[B].set(c_ctx)
    mod = _modulation(cc, w_mod, b_mod)

    w_in_p = jnp.take(w_in, jnp.asarray(_COL_PERM), axis=2).astype(bf16)
    gm_bs_full = jnp.repeat(jnp.swapaxes(gm_bs, 1, 2), GM_GROUP_DIM, axis=2)

    xc = ctx
    for i in range(DEPTH):
        lp = {
            'g_pre': g_pre[i], 'g_post': g_post[i], 'w_in': w_in_p[i],
            'hy_conv_w': hy_conv_w[i], 'hy_conv_b': hy_conv_b[i],
            'hy_w1': hy_w1[i], 'hy_b1': hy_b1[i], 'hy_w2': hy_w2[i], 'hy_b2': hy_b2[i],
            'hy_w3': hy_w3[i], 'hy_freq': hy_freq[i], 'hy_decay': hy_decay[i], 'hy_bias': hy_bias[i],
            'q_gain': q_gain[i], 'k_gain': k_gain[i], 'gm_ln_g': gm_ln_g[i], 'gm_ln_b': gm_ln_b[i],
            'gm_ws': gm_ws[i].astype(bf16), 'gm_bs_full': gm_bs_full[i],
            'w_hy_o': w_hy_o[i].astype(bf16), 'w_att_o': w_att_o[i].astype(bf16),
            'w_gm_o': w_gm_o[i].astype(bf16), 'w_out': w_out[i].astype(bf16),
        }
        mod_x = mod[i, :B].reshape(B, 1, 3 * D)
        mod_c = jnp.broadcast_to(mod[i, B].reshape(1, 1, 3 * D), (B, 1, 3 * D))
        x, xc = _layer(x, xc, mod_x, mod_c, rope, lp, i < DEPTH - 1)
    return x
```

```python
import functools
import math

import numpy as np
import jax
import jax.numpy as jnp
from jax import lax
from jax.experimental import pallas as pl
from jax.experimental.pallas import tpu as pltpu

f32 = jnp.float32
bf16 = jnp.bfloat16

D_MODEL = 1024
DEPTH = 2
GRID_W = 64
RMS_EPS = 1e-6
LN_EPS = 1e-5

HY_WIDTH = D_MODEL
HY_ORDER = 2
HY_BANDS = 16
HY_EMB = 1 + 2 * HY_BANDS
HY_FILTER_HIDDEN = 64

HEAD_DIM = 128
ATT_HEADS = D_MODEL // HEAD_DIM
ATT_KV_HEADS = 2
ATT_GROUP = ATT_HEADS // ATT_KV_HEADS
ATT_WIDTH = ATT_HEADS * HEAD_DIM
KV_WIDTH = ATT_KV_HEADS * HEAD_DIM
AXIS_DIM = HEAD_DIM // 2
ROPE_THETA = 10000.0

GM_WIDTH = D_MODEL
GM_GROUPS = 8
GM_GROUP_DIM = GM_WIDTH // GM_GROUPS
CHUNK = 128

OFF_HY = 0
OFF_HY_GATE = OFF_HY + (HY_ORDER + 1) * HY_WIDTH
OFF_Q = OFF_HY_GATE + HY_WIDTH
OFF_K = OFF_Q + ATT_WIDTH
OFF_V = OFF_K + KV_WIDTH
OFF_ATT_GATE = OFF_V + KV_WIDTH
OFF_GM = OFF_ATT_GATE + ATT_WIDTH
OFF_GM_GATE = OFF_GM + 2 * GM_WIDTH
OFF_MERGE = OFF_GM_GATE + GM_WIDTH
IN_WIDTH = OFF_MERGE + 3 * D_MODEL

P_HY = 0
P_HY_GATE = 3 * D_MODEL
P_Q = 4 * D_MODEL
P_ATT_GATE = 5 * D_MODEL
P_GM = 6 * D_MODEL
P_GM_GATE = 8 * D_MODEL
P_MERGE = 9 * D_MODEL
P_K = 12 * D_MODEL
P_V = P_K + KV_WIDTH
_COL_RANGES = ((OFF_HY, OFF_K), (OFF_ATT_GATE, IN_WIDTH), (OFF_K, OFF_ATT_GATE))

LANES = 128
SUBLANES = 8
FFT_COLS = 256
FFT_GROUP = 8
VMEM_LIMIT = 52 * 1024 * 1024

_SM_SCALE_LOG2E = (HEAD_DIM ** -0.5) * math.log2(math.e)


def _params(sem, vmem=VMEM_LIMIT):
    return pltpu.CompilerParams(dimension_semantics=sem, vmem_limit_bytes=vmem)


def _silu(x):
    return x * jax.nn.sigmoid(x)


def _mod_kernel(cc_ref, w_ref, b_ref, o_ref):
    s = _silu(cc_ref[...]).astype(bf16)
    o_ref[0] = jnp.dot(s, w_ref[0].astype(bf16), preferred_element_type=f32) + b_ref[0]


def _modulation(cc, w_mod, b_mod):
    R, D = cc.shape
    tn = D
    return pl.pallas_call(
        _mod_kernel,
        grid=(DEPTH, 3 * D // tn),
        in_specs=[pl.BlockSpec((R, D), lambda i, n: (0, 0)),
                  pl.BlockSpec((1, D, tn), lambda i, n: (i, 0, n)),
                  pl.BlockSpec((1, 1, tn), lambda i, n: (i, 0, n))],
        out_specs=pl.BlockSpec((1, R, tn), lambda i, n: (i, 0, n)),
        out_shape=jax.ShapeDtypeStruct((DEPTH, R, 3 * D), f32),
        compiler_params=_params(("parallel", "parallel")),
    )(cc, w_mod, b_mod.reshape(DEPTH, 1, 3 * D))


def _inproj_kernel(x_ref, mod_ref, g_ref, w_ref, o_ref, h_scr):
    D = x_ref.shape[2]

    @pl.when(pl.program_id(2) == 0)
    def _():
        x = x_ref[0]
        r = lax.rsqrt(jnp.mean(x * x, axis=-1, keepdims=True) + RMS_EPS)
        shift = mod_ref[0, :, 0:D]
        scale = mod_ref[0, :, D:2 * D]
        h_scr[...] = ((x * r) * g_ref[...] * (1.0 + scale) + shift).astype(bf16)

    o_ref[0] = jnp.dot(h_scr[...], w_ref[...], preferred_element_type=f32).astype(o_ref.dtype)


def _inproj(x, mod, g_pre, w, tm, tn):
    B, L, D = x.shape
    N = w.shape[1]
    return pl.pallas_call(
        _inproj_kernel,
        grid=(B, L // tm, N // tn),
        in_specs=[pl.BlockSpec((1, tm, D), lambda b, m, n: (b, m, 0)),
                  pl.BlockSpec((1, 1, 3 * D), lambda b, m, n: (b, 0, 0)),
                  pl.BlockSpec((1, D), lambda b, m, n: (0, 0)),
                  pl.BlockSpec((D, tn), lambda b, m, n: (0, n))],
        out_specs=pl.BlockSpec((1, tm, tn), lambda b, m, n: (b, m, n)),
        out_shape=jax.ShapeDtypeStruct((B, L, N), bf16),
        scratch_shapes=[pltpu.VMEM((tm, D), bf16)],
        compiler_params=_params(("parallel", "parallel", "arbitrary")),
    )(x, mod, g_pre.reshape(1, D), w)


def _rope(x, cos, sin):
    lane = lax.broadcasted_iota(jnp.int32, x.shape, 1)
    first = (lane & (AXIS_DIM // 2)) == 0
    rot = jnp.where(first, pltpu.roll(x, HEAD_DIM - AXIS_DIM // 2, 1), pltpu.roll(x, AXIS_DIM // 2, 1))
    return x * cos + rot * sin


def _head_norm(x, gain):
    r = lax.rsqrt(jnp.mean(x * x, axis=-1, keepdims=True) + RMS_EPS)
    return x * r * gain


def _kprep_kernel(k_ref, cos_ref, sin_ref, gain_ref, o_ref):
    for h in range(ATT_KV_HEADS):
        sl = slice(h * HEAD_DIM, (h + 1) * HEAD_DIM)
        kn = _head_norm(k_ref[0, :, sl].astype(f32), gain_ref[...])
        o_ref[0, :, sl] = _rope(kn, cos_ref[...], sin_ref[...]).astype(bf16)


def _kprep(proj, col_off, cos, sin, gain, tm):
    B, L, _ = proj.shape
    cb = col_off // KV_WIDTH
    return pl.pallas_call(
        _kprep_kernel,
        grid=(B, L // tm),
        in_specs=[pl.BlockSpec((1, tm, KV_WIDTH), lambda b, m: (b, m, cb)),
                  pl.BlockSpec((tm, HEAD_DIM), lambda b, m: (m, 0)),
                  pl.BlockSpec((tm, HEAD_DIM), lambda b, m: (m, 0)),
                  pl.BlockSpec((1, HEAD_DIM), lambda b, m: (0, 0))],
        out_specs=pl.BlockSpec((1, tm, KV_WIDTH), lambda b, m: (b, m, 0)),
        out_shape=jax.ShapeDtypeStruct((B, L, KV_WIDTH), bf16),
        compiler_params=_params(("parallel", "parallel")),
    )(proj, cos, sin, gain.reshape(1, HEAD_DIM))


def _attn_kernel(q_ref, k_ref, v_ref, cos_ref, sin_ref, gain_ref, gate_ref, o_ref):
    k = k_ref[0]
    v = v_ref[0]
    for h in range(ATT_GROUP):
        sl = slice(h * HEAD_DIM, (h + 1) * HEAD_DIM)
        qn = _head_norm(q_ref[0, :, sl].astype(f32), gain_ref[...])
        q = _rope(qn, cos_ref[...], sin_ref[...]).astype(bf16)
        s = lax.dot_general(q, k, (((1,), (1,)), ((), ())), preferred_element_type=f32)
        m = jnp.max(s, axis=-1, keepdims=True)
        p = jnp.exp2((s - m) * _SM_SCALE_LOG2E)
        l = jnp.sum(p, axis=-1, keepdims=True)
        o = jnp.dot(p.astype(bf16), v, preferred_element_type=f32) / l
        o_ref[0, :, sl] = (o * _silu(gate_ref[0, :, sl].astype(f32))).astype(bf16)


def _attention(proj, k_all, v_all, cos, sin, gain, tq):
    B, L, _ = proj.shape
    Lk = k_all.shape[1]
    gw = ATT_GROUP * HEAD_DIM
    qb, gb = P_Q // gw, P_ATT_GATE // gw
    return pl.pallas_call(
        _attn_kernel,
        grid=(B, ATT_KV_HEADS, L // tq),
        in_specs=[pl.BlockSpec((1, tq, gw), lambda b, h, i: (b, i, qb + h)),
                  pl.BlockSpec((1, Lk, HEAD_DIM), lambda b, h, i: (b, 0, h)),
                  pl.BlockSpec((1, Lk, HEAD_DIM), lambda b, h, i: (b, 0, h)),
                  pl.BlockSpec((tq, HEAD_DIM), lambda b, h, i: (i, 0)),
                  pl.BlockSpec((tq, HEAD_DIM), lambda b, h, i: (i, 0)),
                  pl.BlockSpec((1, HEAD_DIM), lambda b, h, i: (0, 0)),
                  pl.BlockSpec((1, tq, gw), lambda b, h, i: (b, i, gb + h))],
        out_specs=pl.BlockSpec((1, tq, gw), lambda b, h, i: (b, i, h)),
        out_shape=jax.ShapeDtypeStruct((B, L, ATT_WIDTH), bf16),
        compiler_params=_params(("parallel", "parallel", "parallel")),
    )(proj, k_all, v_all, cos, sin, gain.reshape(1, HEAD_DIM), proj)


def _gmlp_kernel(p_ref, gate_ref, lng_ref, lnb_ref, ws_ref, bs_ref, o_ref):
    tm = p_ref.shape[1]
    p = p_ref[0].astype(f32)
    uv = 0.5 * p * (1.0 + lax.erf(p * (2.0 ** -0.5)))
    u = uv[:, :GM_WIDTH]
    v = uv[:, GM_WIDTH:]
    vc = v - jnp.mean(v, axis=-1, keepdims=True)
    var = jnp.mean(vc * vc, axis=-1, keepdims=True)
    vn = (vc * lax.rsqrt(var + LN_EPS) * lng_ref[...] + lnb_ref[...]).astype(bf16)
    ug = u * _silu(gate_ref[0].astype(f32))
    for n in range(tm // CHUNK):
        rows = slice(n * CHUNK, (n + 1) * CHUNK)
        for g in range(GM_GROUPS):
            cols = slice(g * GM_GROUP_DIM, (g + 1) * GM_GROUP_DIM)
            y = jnp.dot(ws_ref[g], vn[rows, cols], preferred_element_type=f32) + bs_ref[:, cols]
            o_ref[0, rows, cols] = (ug[rows, cols] * y).astype(bf16)


def _gmlp(proj, ln_g, ln_b, ws, bs_full, tm):
    B, L, _ = proj.shape
    pb, gb = P_GM // (2 * GM_WIDTH), P_GM_GATE // GM_WIDTH
    return pl.pallas_call(
        _gmlp_kernel,
        grid=(B, L // tm),
        in_specs=[pl.BlockSpec((1, tm, 2 * GM_WIDTH), lambda b, m: (b, m, pb)),
                  pl.BlockSpec((1, tm, GM_WIDTH), lambda b, m: (b, m, gb)),
                  pl.BlockSpec((1, GM_WIDTH), lambda b, m: (0, 0)),
                  pl.BlockSpec((1, GM_WIDTH), lambda b, m: (0, 0)),
                  pl.BlockSpec((GM_GROUPS, CHUNK, CHUNK), lambda b, m: (0, 0, 0)),
                  pl.BlockSpec((CHUNK, GM_WIDTH), lambda b, m: (0, 0))],
        out_specs=pl.BlockSpec((1, tm, GM_WIDTH), lambda b, m: (b, m, 0)),
        out_shape=jax.ShapeDtypeStruct((B, L, GM_WIDTH), bf16),
        compiler_params=_params(("parallel", "parallel")),
    )(proj, proj, ln_g.reshape(1, GM_WIDTH), ln_b.reshape(1, GM_WIDTH), ws, bs_full)


def _shortconv_kernel(p_ref, w_ref, b_ref, o_ref):
    L = p_ref.shape[1]
    x = p_ref[0].astype(f32)
    row = lax.broadcasted_iota(jnp.int32, x.shape, 0)
    xm = jnp.where(row == 0, 0.0, pltpu.roll(x, 1, 0))
    xp = jnp.where(row == L - 1, 0.0, pltpu.roll(x, L - 1, 0))
    o_ref[0] = (xm * w_ref[0:1, :] + x * w_ref[1:2, :] + xp * w_ref[2:3, :] + b_ref[...]).astype(bf16)


def _shortconv(proj, w, b, tc):
    B, L, _ = proj.shape
    W = w.shape[1]
    return pl.pallas_call(
        _shortconv_kernel,
        grid=(B, W // tc),
        in_specs=[pl.BlockSpec((1, L, tc), lambda b, c: (b, 0, c)),
                  pl.BlockSpec((3, tc), lambda b, c: (0, c)),
                  pl.BlockSpec((1, tc), lambda b, c: (0, c))],
        out_specs=pl.BlockSpec((1, L, tc), lambda b, c: (b, 0, c)),
        out_shape=jax.ShapeDtypeStruct((B, L, W), bf16),
        compiler_params=_params(("parallel", "parallel")),
    )(proj, w, b.reshape(1, W))


def _filter_embedding(L):
    n = np.arange(2 * L)
    pos = np.where(n < L, n, 2 * L - n)
    pos = np.where(n == L, 0, pos)
    t = np.linspace(0.0, 1.0, L)[pos]
    bands = np.linspace(1e-4, HY_BANDS - 1, HY_BANDS)
    ang = (2.0 * math.pi / L) * pos[:, None] * bands[None, :]
    z = np.zeros((2 * L, LANES), np.float64)
    z[:, 0] = t
    z[:, 1:1 + HY_BANDS] = np.cos(ang)
    z[:, 1 + HY_BANDS:HY_EMB] = -np.sin(ang)
    z[:, HY_EMB] = (n < L)
    z[:, HY_EMB + 1] = (n > L) | (n == 0)
    return jnp.asarray(z, f32)


def _filter_kernel(z_ref, w1_ref, b1_ref, w2_ref, b2_ref, freq_ref, w3_ref, dec_ref, o_ref):
    hp = lax.Precision.HIGHEST
    C = o_ref.shape[2]
    z = z_ref[...]
    t = z[:, 0:1]
    mf = z[:, HY_EMB:HY_EMB + 1]
    mb = z[:, HY_EMB + 1:HY_EMB + 2]
    h = jnp.sin(freq_ref[0:1, :] * (jnp.dot(z, w1_ref[...], precision=hp, preferred_element_type=f32) + b1_ref[...]))
    h = jnp.sin(freq_ref[1:2, :] * (jnp.dot(h, w2_ref[...], precision=hp, preferred_element_type=f32) + b2_ref[...]))
    for blk in range(2 * HY_ORDER):
        cols = slice(blk * C, (blk + 1) * C)
        taps = jnp.dot(h, w3_ref[:, cols], precision=hp, preferred_element_type=f32)
        taps = (mb if blk % 2 else mf) * taps * jnp.exp(-t * jnp.abs(dec_ref[:, cols]))
        if blk % 2:
            o_ref[blk // 2] = o_ref[blk // 2] + taps
        else:
            o_ref[blk // 2] = taps


def _filters(L, w1, b1, w2, b2, w3, freq, decay, tr):
    H = HY_FILTER_HIDDEN
    C = HY_WIDTH
    W = 2 * HY_ORDER * C
    z = _filter_embedding(L)
    w1p = jnp.zeros((LANES, H), f32).at[:HY_EMB].set(w1)
    full = lambda shape: pl.BlockSpec(shape, lambda r: (0,) * len(shape))
    return pl.pallas_call(
        _filter_kernel,
        grid=(2 * L // tr,),
        in_specs=[pl.BlockSpec((tr, LANES), lambda r: (r, 0)),
                  full((LANES, H)), full((1, H)), full((H, H)), full((1, H)), full((2, H)),
                  full((H, W)), full((1, W))],
        out_specs=pl.BlockSpec((HY_ORDER, tr, C), lambda r: (0, r, 0)),
        out_shape=jax.ShapeDtypeStruct((HY_ORDER, 2 * L, C), f32),
        compiler_params=_params(("parallel",)),
    )(z, w1p, b1.reshape(1, H), w2, b2.reshape(1, H), freq, w3, decay.reshape(1, W))


def _fft_dims(L):
    n2 = SUBLANES
    while (2 * n2) * (2 * n2) <= L:
        n2 *= 2
    return n2, L // n2, 2 * (L // n2), n2 + SUBLANES


def _stack(fr, fi):
    return np.block([[fr, -fi], [fi, fr]])


@functools.lru_cache(maxsize=None)
def _dft_tables(L):
    N2, N1h, N1, _ = _fft_dims(L)
    N = 2 * L
    k1 = np.arange(N1)
    a1 = -2.0 * np.pi * np.outer(k1, np.arange(N1)) / N1
    f1 = _stack(np.cos(a1[:, :N1h]), np.sin(a1[:, :N1h]))
    f1k = np.concatenate([np.cos(a1), np.sin(a1)], axis=0)
    f3 = _stack(np.cos(-a1[:, :N1h].T), np.sin(-a1[:, :N1h].T))
    n2 = np.arange(N2)
    a2 = -2.0 * np.pi * (np.outer(n2, n2)[None] / N2 + (k1[:, None, None] * n2[None, None, :]) / N)
    f2 = np.stack([_stack(np.cos(a), np.sin(a)) for a in a2])
    a2t = -np.transpose(a2, (0, 2, 1))
    g2 = np.stack([_stack(np.cos(a), np.sin(a)) for a in a2t])
    return tuple(jnp.asarray(m, bf16) for m in (f1, f1k, f2, g2, f3))


def _load_cols(ref, lead, rows):
    return jnp.concatenate([ref[lead + (t, rows, slice(None))] for t in range(ref.shape[-3])], axis=1)


def _store_cols(ref, lead, rows, val):
    for t in range(ref.shape[-3]):
        ref[lead + (t, rows, slice(None))] = val[:, t * LANES:(t + 1) * LANES]


def _grouped_loop(n, group, load, compute, store):
    def body(i, c):
        idx = [i * group + u for u in range(group)]
        vals = [load(ix) for ix in idx]
        outs = [compute(ix, v) for ix, v in zip(idx, vals)]
        for ix, o in zip(idx, outs):
            store(ix, o)
        return c

    lax.fori_loop(0, n // group, body, 0)


def _load_ri(ref, rows):
    return jnp.concatenate([_load_cols(ref, (ri,), rows) for ri in range(2)], axis=0).astype(bf16)


def _store_ri(ref, rows, val):
    h = val.shape[0] // 2
    for ri in range(2):
        _store_cols(ref, (ri,), rows, val[ri * h:(ri + 1) * h])


def _spectrum_kernel(k_ref, f1k_ref, f2_ref, o_ref, a_scr, *, L):
    N2, N1h, N1, P = _fft_dims(L)
    T = a_scr.shape[1]
    inv_n = 1.0 / (2 * L)

    def load1(n2):
        return jnp.concatenate([k_ref[0, pl.ds(n2, N1, stride=N2), t * LANES:(t + 1) * LANES]
                                for t in range(T)], axis=1).astype(bf16)

    _grouped_loop(N2, FFT_GROUP, load1,
                  lambda n2, w: jnp.dot(f1k_ref[...], w, preferred_element_type=f32),
                  lambda n2, r: _store_ri(a_scr, pl.ds(n2, N1, stride=P), r))

    def store2(k1, x):
        out_rows = pl.ds(pl.multiple_of(k1 * N2, N2), N2)
        o_ref[0, 0, out_rows, :] = x[:N2].astype(bf16)
        o_ref[0, 1, out_rows, :] = x[N2:].astype(bf16)

    _grouped_loop(N1, FFT_GROUP,
                  lambda k1: _load_ri(a_scr, pl.ds(pl.multiple_of(k1 * P, SUBLANES), N2)),
                  lambda k1, a: jnp.dot(f2_ref[k1], a, preferred_element_type=f32) * inv_n,
                  store2)


def _spectrum(k, tc):
    _, N, C = k.shape
    L = N // 2
    N2, N1h, N1, P = _fft_dims(L)
    _, f1k, f2, _, _ = _dft_tables(L)
    return pl.pallas_call(
        functools.partial(_spectrum_kernel, L=L),
        grid=(HY_ORDER, C // tc),
        in_specs=[pl.BlockSpec((1, N, tc), lambda o, c: (o, 0, c)),
                  pl.BlockSpec(f1k.shape, lambda o, c: (0, 0)),
                  pl.BlockSpec(f2.shape, lambda o, c: (0, 0, 0))],
        out_specs=pl.BlockSpec((1, 2, N, tc), lambda o, c: (o, 0, 0, c)),
        out_shape=jax.ShapeDtypeStruct((HY_ORDER, 2, N, C), bf16),
        scratch_shapes=[pltpu.VMEM((2, tc // LANES, N1 * P, LANES), f32)],
        compiler_params=_params(("parallel", "parallel")),
    )(k, f1k, f2)


def _fftconv_kernel(*refs, L, chunk, has_gate):
    if has_gate:
        u_in, u_ep, g_ref, gate_ref, bias_ref, kf_ref, f1_ref, f2_ref, g2_ref, f3_ref, o_ref, z_scr, a_scr = refs
    else:
        u_in, u_ep, g_ref, bias_ref, kf_ref, f1_ref, f2_ref, g2_ref, f3_ref, o_ref, z_scr, a_scr = refs
    N2, N1h, N1, P = _fft_dims(L)
    J = L // chunk
    nb = chunk // N2
    j = pl.program_id(2)

    @pl.when(j < J)
    def _load():
        for ri in range(2):
            for blk in range(nb):
                rows = pl.ds(pl.multiple_of((j * nb + blk) * P, SUBLANES), N2)
                _store_cols(z_scr, (ri,), rows, u_in[ri, blk * N2:(blk + 1) * N2, :].astype(f32))

    @pl.when(j == J - 1)
    def _transform():
        _grouped_loop(N2, FFT_GROUP,
                      lambda n2: _load_ri(z_scr, pl.ds(n2, N1h, stride=P)),
                      lambda n2, w: jnp.dot(f1_ref[...], w, preferred_element_type=f32),
                      lambda n2, r: _store_ri(a_scr, pl.ds(n2, N1, stride=P), r))

        def block_rows(k1):
            return pl.ds(pl.multiple_of(k1 * P, SUBLANES), N2)

        def middle(k1, a):
            x = jnp.dot(f2_ref[k1], a, preferred_element_type=f32)
            krows = pl.ds(pl.multiple_of(k1 * N2, N2), N2)
            kr = kf_ref[0, 0, krows, :].astype(f32)
            ki = kf_ref[0, 1, krows, :].astype(f32)
            xr, xi = x[:N2], x[N2:]
            y = jnp.concatenate([xr * kr - xi * ki, xr * ki + xi * kr], axis=0).astype(bf16)
            return jnp.dot(g2_ref[k1], y, preferred_element_type=f32)

        _grouped_loop(N1, FFT_GROUP,
                      lambda k1: _load_ri(a_scr, block_rows(k1)),
                      middle,
                      lambda k1, b: _store_ri(a_scr, block_rows(k1), b))

        _grouped_loop(N2, FFT_GROUP,
                      lambda n2: _load_ri(a_scr, pl.ds(n2, N1, stride=P)),
                      lambda n2, b: jnp.dot(f3_ref[...], b, preferred_element_type=f32),
                      lambda n2, y: _store_ri(z_scr, pl.ds(n2, N1h, stride=P), y))

    @pl.when(j >= J)
    def _epilogue():
        for ri in range(2):
            for blk in range(nb):
                rows = pl.ds(pl.multiple_of(((j - J) * nb + blk) * P, SUBLANES), N2)
                y = _load_cols(z_scr, (ri,), rows)
                sl = slice(blk * N2, (blk + 1) * N2)
                u = u_ep[ri, sl, :].astype(f32)
                o = g_ref[ri, sl, :].astype(f32) * (y + bias_ref[...] * u)
                if has_gate:
                    o = o * _silu(gate_ref[ri, sl, :].astype(f32))
                o_ref[ri, sl, :] = o.astype(bf16)


def _fftconv(u, u_cb, g, g_cb, gate, gate_cb, bias, kf, order, tc):
    B, L, _ = u.shape
    C = HY_WIDTH
    N2, N1h, N1, P = _fft_dims(L)
    chunk = min(L, 256)
    J = L // chunk
    f1, _, f2, g2, f3 = _dft_tables(L)
    T = tc // LANES
    in_map = lambda cb: (lambda c, p, j: (p, jnp.minimum(j, J - 1), cb + c))
    ep_map = lambda cb: (lambda c, p, j: (p, jnp.maximum(j - J, 0), cb + c))
    const = lambda nd: (lambda c, p, j: (0,) * nd)
    once = pl.Buffered(1)
    ops = [u, u, g]
    specs = [pl.BlockSpec((2, chunk, tc), in_map(u_cb)),
             pl.BlockSpec((2, chunk, tc), ep_map(u_cb)),
             pl.BlockSpec((2, chunk, tc), ep_map(g_cb))]
    if gate is not None:
        ops.append(gate)
        specs.append(pl.BlockSpec((2, chunk, tc), ep_map(gate_cb)))
    ops += [bias.reshape(HY_ORDER, 1, C), kf, f1, f2, g2, f3]
    specs += [pl.BlockSpec((None, 1, tc), lambda c, p, j: (order, 0, c)),
              pl.BlockSpec((1, 2, 2 * L, tc), lambda c, p, j: (order, 0, 0, c), pipeline_mode=once),
              pl.BlockSpec(f1.shape, const(2), pipeline_mode=once),
              pl.BlockSpec(f2.shape, const(3), pipeline_mode=once),
              pl.BlockSpec(g2.shape, const(3), pipeline_mode=once),
              pl.BlockSpec(f3.shape, const(2), pipeline_mode=once)]
    return pl.pallas_call(
        functools.partial(_fftconv_kernel, L=L, chunk=chunk, has_gate=gate is not None),
        grid=(C // tc, B // 2, 2 * J),
        in_specs=specs,
        out_specs=pl.BlockSpec((2, chunk, tc), ep_map(0)),
        out_shape=jax.ShapeDtypeStruct((B, L, C), bf16),
        scratch_shapes=[pltpu.VMEM((2, T, N1h * P, LANES), f32),
                        pltpu.VMEM((2, T, N1 * P, LANES), f32)],
        compiler_params=_params(("parallel", "parallel", "arbitrary")),
    )(*ops)


def _merge_kernel(hy_ref, at_ref, gm_ref, mg_ref, x_ref, gate_ref, gpost_ref,
                  why_ref, wat_ref, wgm_ref, wout_ref, o_ref):
    D = x_ref.shape[2]
    acc = None
    for i, (br, w) in enumerate(((hy_ref, why_ref), (at_ref, wat_ref), (gm_ref, wgm_ref))):
        y = jnp.dot(br[0], w[...], preferred_element_type=f32)
        y = jax.nn.sigmoid(mg_ref[0, :, i * D:(i + 1) * D].astype(f32)) * y
        acc = y if acc is None else acc + y
    o = jnp.dot(acc.astype(bf16), wout_ref[...], preferred_element_type=f32)
    r = o * lax.rsqrt(jnp.mean(o * o, axis=-1, keepdims=True) + RMS_EPS) * gpost_ref[...]
    o_ref[0] = x_ref[0] + gate_ref[0] * r


def _merge(hy, at, gm, proj, x, mod, g_post, w_hy, w_at, w_gm, w_out, tm):
    B, L, D = x.shape
    mb = P_MERGE // (3 * D)
    row = lambda b, m: (b, m, 0)
    wspec = pl.BlockSpec((D, D), lambda b, m: (0, 0))
    return pl.pallas_call(
        _merge_kernel,
        grid=(B, L // tm),
        in_specs=[pl.BlockSpec((1, tm, D), row), pl.BlockSpec((1, tm, D), row), pl.BlockSpec((1, tm, D), row),
                  pl.BlockSpec((1, tm, 3 * D), lambda b, m: (b, m, mb)),
                  pl.BlockSpec((1, tm, D), row),
                  pl.BlockSpec((1, 1, D), lambda b, m: (b, 0, 2)),
                  pl.BlockSpec((1, D), lambda b, m: (0, 0)),
                  wspec, wspec, wspec, wspec],
        out_specs=pl.BlockSpec((1, tm, D), row),
        out_shape=jax.ShapeDtypeStruct((B, L, D), f32),
        compiler_params=_params(("parallel", "parallel")),
    )(hy, at, gm, proj, x, mod, g_post.reshape(1, D), w_hy, w_at, w_gm, w_out)


def _rope_tables(L):
    pos = np.arange(L)
    inv = np.power(ROPE_THETA, -np.arange(0, AXIS_DIM, 2) / AXIS_DIM)
    ar = (pos // GRID_W)[:, None] * inv
    ac = (pos % GRID_W)[:, None] * inv
    cos = np.concatenate([np.cos(ar), np.cos(ar), np.cos(ac), np.cos(ac)], axis=1)
    sin = np.concatenate([-np.sin(ar), np.sin(ar), -np.sin(ac), np.sin(ac)], axis=1)
    return jnp.asarray(cos, f32), jnp.asarray(sin, f32)


def _identity_rope(L):
    return jnp.ones((L, HEAD_DIM), f32), jnp.zeros((L, HEAD_DIM), f32)


def _hyena(proj, lp):
    B, L, _ = proj.shape
    tc = FFT_COLS
    nct = HY_WIDTH // tc
    hyc = _shortconv(proj, lp['hy_conv_w'], lp['hy_conv_b'], tc)
    k = _filters(L, lp['hy_w1'], lp['hy_b1'], lp['hy_w2'], lp['hy_b2'], lp['hy_w3'], lp['hy_freq'],
                 lp['hy_decay'], min(2 * L, 256))
    kf = _spectrum(k, LANES)
    z1 = _fftconv(hyc, 2 * nct, hyc, 0, None, 0, lp['hy_bias'], kf, 0, tc)
    return _fftconv(z1, 0, hyc, nct, proj, P_HY_GATE // tc, lp['hy_bias'], kf, 1, tc)


def _mixer(proj, k_all, v_all, rope_q, lp, tq, tm_gm):
    att = _attention(proj, k_all, v_all, rope_q[0], rope_q[1], lp['q_gain'], tq)
    hy = _hyena(proj, lp)
    gm = _gmlp(proj, lp['gm_ln_g'], lp['gm_ln_b'], lp['gm_ws'], lp['gm_bs_full'], tm_gm)
    return hy, att, gm


def _layer(x, xc, mod_x, mod_c, rope, lp, ctx_out):
    B, L, D = x.shape
    C = xc.shape[1]
    w_in = lp['w_in']
    proj = _inproj(x, mod_x, lp['g_pre'], w_in, min(L, 1024), 1280)
    wc = w_in if ctx_out else w_in[:, P_K:]
    projc = _inproj(xc, mod_c, lp['g_pre'], wc, C, min(wc.shape[1], 1280))
    kv_off = P_K if ctx_out else 0
    ident = _identity_rope(C)
    kc = _kprep(projc, kv_off, ident[0], ident[1], lp['k_gain'], C)
    vc = projc[:, :, kv_off + KV_WIDTH:kv_off + 2 * KV_WIDTH]
    k = _kprep(proj, P_K, rope[0], rope[1], lp['k_gain'], min(L, 1024))
    k_all = jnp.concatenate([kc, k], axis=1)
    v_all = jnp.concatenate([vc, proj[:, :, P_V:P_V + KV_WIDTH]], axis=1)

    hy, att, gm = _mixer(proj, k_all, v_all, rope, lp, 256, 256)
    x_new = _merge(hy, att, gm, proj, x, mod_x, lp['g_post'], lp['w_hy_o'], lp['w_att_o'], lp['w_gm_o'],
                   lp['w_out'], min(L, 512))
    if not ctx_out:
        return x_new, xc
    hyc, attc, gmc = _mixer(projc, kc, vc, ident, lp, C, min(C, 256))
    xc_new = _merge(hyc, attc, gmc, projc, xc, mod_c, lp['g_post'], lp['w_hy_o'], lp['w_att_o'], lp['w_gm_o'],
                    lp['w_out'], C)
    return x_new, xc_new


def kernel(x, c, ctx, c_ctx, w_mod, b_mod, g_pre, g_post, w_in, hy_conv_w, hy_conv_b, hy_w1, hy_b1, hy_w2, hy_b2, hy_w3, hy_freq, hy_decay, hy_bias, q_gain, k_gain, gm_ln_g, gm_ln_b, gm_ws, gm_bs, w_hy_o, w_att_o, w_gm_o, w_out):
    B, L, D = x.shape
    rope = _rope_tables(L)

    R = -(-(B + 1) // (2 * SUBLANES)) * (2 * SUBLANES)
    cc = jnp.zeros((R, D), f32).at[:B].set(c).at[B].set(c_ctx)
    mod = _modulation(cc, w_mod, b_mod)

    w_in_p = jnp.concatenate([w_in[:, :, a:b].astype(bf16) for a, b in _COL_RANGES], axis=2)
    gm_bs_full = jnp.repeat(jnp.swapaxes(gm_bs, 1, 2), GM_GROUP_DIM, axis=2)

    xc = ctx
    for i in range(DEPTH):
        lp = {
            'g_pre': g_pre[i], 'g_post': g_post[i], 'w_in': w_in_p[i],
            'hy_conv_w': hy_conv_w[i], 'hy_conv_b': hy_conv_b[i],
            'hy_w1': hy_w1[i], 'hy_b1': hy_b1[i], 'hy_w2': hy_w2[i], 'hy_b2': hy_b2[i],
            'hy_w3': hy_w3[i], 'hy_freq': hy_freq[i], 'hy_decay': hy_decay[i], 'hy_bias': hy_bias[i],
            'q_gain': q_gain[i], 'k_gain': k_gain[i], 'gm_ln_g': gm_ln_g[i], 'gm_ln_b': gm_ln_b[i],
            'gm_ws': gm_ws[i].astype(bf16), 'gm_bs_full': gm_bs_full[i],
            'w_hy_o': w_hy_o[i].astype(bf16), 'w_att_o': w_att_o[i].astype(bf16),
            'w_gm_o': w_gm_o[i].astype(bf16), 'w_out': w_out[i].astype(bf16),
        }
        mod_x = mod[i, :B].reshape(B, 1, 3 * D)
        mod_c = jnp.broadcast_to(mod[i, B].reshape(1, 1, 3 * D), (B, 1, 3 * D))
        x, xc = _layer(x, xc, mod_x, mod_c, rope, lp, i < DEPTH - 1)
    return x
```

```python
import functools
import math

import numpy as np
import jax
import jax.numpy as jnp
from jax import lax
from jax.experimental import pallas as pl
from jax.experimental.pallas import tpu as pltpu

f32 = jnp.float32
bf16 = jnp.bfloat16

D_MODEL = 1024
DEPTH = 2
GRID_W = 64
RMS_EPS = 1e-6
LN_EPS = 1e-5

HY_WIDTH = D_MODEL
HY_ORDER = 2
HY_BANDS = 16
HY_EMB = 1 + 2 * HY_BANDS
HY_FILTER_HIDDEN = 64

HEAD_DIM = 128
ATT_HEADS = D_MODEL // HEAD_DIM
ATT_KV_HEADS = 2
ATT_GROUP = ATT_HEADS // ATT_KV_HEADS
ATT_WIDTH = ATT_HEADS * HEAD_DIM
KV_WIDTH = ATT_KV_HEADS * HEAD_DIM
AXIS_DIM = HEAD_DIM // 2
ROPE_THETA = 10000.0

GM_WIDTH = D_MODEL
GM_GROUPS = 8
GM_GROUP_DIM = GM_WIDTH // GM_GROUPS
CHUNK = 128

OFF_HY = 0
OFF_HY_GATE = OFF_HY + (HY_ORDER + 1) * HY_WIDTH
OFF_Q = OFF_HY_GATE + HY_WIDTH
OFF_K = OFF_Q + ATT_WIDTH
OFF_V = OFF_K + KV_WIDTH
OFF_ATT_GATE = OFF_V + KV_WIDTH
OFF_GM = OFF_ATT_GATE + ATT_WIDTH
OFF_GM_GATE = OFF_GM + 2 * GM_WIDTH
OFF_MERGE = OFF_GM_GATE + GM_WIDTH
IN_WIDTH = OFF_MERGE + 3 * D_MODEL

P_HY = 0
P_HY_GATE = 3 * D_MODEL
P_Q = 4 * D_MODEL
P_ATT_GATE = 5 * D_MODEL
P_GM = 6 * D_MODEL
P_GM_GATE = 8 * D_MODEL
P_MERGE = 9 * D_MODEL
P_K = 12 * D_MODEL
P_V = P_K + KV_WIDTH
_COL_RANGES = ((OFF_HY, OFF_K), (OFF_ATT_GATE, IN_WIDTH), (OFF_K, OFF_ATT_GATE))

LANES = 128
SUBLANES = 8
FFT_COLS = 256
FFT_GROUP = 8
FFT_CHUNK = 512
VMEM_LIMIT = 52 * 1024 * 1024

_SM_SCALE_LOG2E = (HEAD_DIM ** -0.5) * math.log2(math.e)


def _params(sem, vmem=VMEM_LIMIT):
    return pltpu.CompilerParams(dimension_semantics=sem, vmem_limit_bytes=vmem)


def _silu(x):
    return x * jax.nn.sigmoid(x)


def _mod_kernel(cc_ref, w_ref, b_ref, o_ref):
    s = _silu(cc_ref[...]).astype(bf16)
    o_ref[0] = jnp.dot(s, w_ref[0].astype(bf16), preferred_element_type=f32) + b_ref[0]


def _modulation(cc, w_mod, b_mod):
    R, D = cc.shape
    tn = D
    return pl.pallas_call(
        _mod_kernel,
        grid=(DEPTH, 3 * D // tn),
        in_specs=[pl.BlockSpec((R, D), lambda i, n: (0, 0)),
                  pl.BlockSpec((1, D, tn), lambda i, n: (i, 0, n)),
                  pl.BlockSpec((1, 1, tn), lambda i, n: (i, 0, n))],
        out_specs=pl.BlockSpec((1, R, tn), lambda i, n: (i, 0, n)),
        out_shape=jax.ShapeDtypeStruct((DEPTH, R, 3 * D), f32),
        compiler_params=_params(("parallel", "parallel")),
    )(cc, w_mod, b_mod.reshape(DEPTH, 1, 3 * D))


def _inproj_kernel(x_ref, mod_ref, g_ref, w_ref, o_ref, h_scr):
    D = x_ref.shape[2]

    @pl.when(pl.program_id(2) == 0)
    def _():
        x = x_ref[0]
        r = lax.rsqrt(jnp.mean(x * x, axis=-1, keepdims=True) + RMS_EPS)
        shift = mod_ref[0, :, 0:D]
        scale = mod_ref[0, :, D:2 * D]
        h_scr[...] = ((x * r) * g_ref[...] * (1.0 + scale) + shift).astype(bf16)

    o_ref[0] = jnp.dot(h_scr[...], w_ref[...], preferred_element_type=f32).astype(o_ref.dtype)


def _inproj(x, mod, g_pre, w, tm, tn):
    B, L, D = x.shape
    N = w.shape[1]
    return pl.pallas_call(
        _inproj_kernel,
        grid=(B, L // tm, N // tn),
        in_specs=[pl.BlockSpec((1, tm, D), lambda b, m, n: (b, m, 0)),
                  pl.BlockSpec((1, 1, 3 * D), lambda b, m, n: (b, 0, 0)),
                  pl.BlockSpec((1, D), lambda b, m, n: (0, 0)),
                  pl.BlockSpec((D, tn), lambda b, m, n: (0, n))],
        out_specs=pl.BlockSpec((1, tm, tn), lambda b, m, n: (b, m, n)),
        out_shape=jax.ShapeDtypeStruct((B, L, N), bf16),
        scratch_shapes=[pltpu.VMEM((tm, D), bf16)],
        compiler_params=_params(("parallel", "parallel", "arbitrary")),
    )(x, mod, g_pre.reshape(1, D), w)


def _rope(x, cos, sin):
    lane = lax.broadcasted_iota(jnp.int32, x.shape, 1)
    first = (lane & (AXIS_DIM // 2)) == 0
    rot = jnp.where(first, pltpu.roll(x, HEAD_DIM - AXIS_DIM // 2, 1), pltpu.roll(x, AXIS_DIM // 2, 1))
    return x * cos + rot * sin


def _head_norm(x, gain):
    r = lax.rsqrt(jnp.mean(x * x, axis=-1, keepdims=True) + RMS_EPS)
    return x * r * gain


def _kprep_kernel(k_ref, cos_ref, sin_ref, gain_ref, o_ref):
    for h in range(ATT_KV_HEADS):
        sl = slice(h * HEAD_DIM, (h + 1) * HEAD_DIM)
        kn = _head_norm(k_ref[0, :, sl].astype(f32), gain_ref[...])
        o_ref[0, :, sl] = _rope(kn, cos_ref[...], sin_ref[...]).astype(bf16)


def _kprep(proj, col_off, cos, sin, gain, tm):
    B, L, _ = proj.shape
    cb = col_off // KV_WIDTH
    return pl.pallas_call(
        _kprep_kernel,
        grid=(B, L // tm),
        in_specs=[pl.BlockSpec((1, tm, KV_WIDTH), lambda b, m: (b, m, cb)),
                  pl.BlockSpec((tm, HEAD_DIM), lambda b, m: (m, 0)),
                  pl.BlockSpec((tm, HEAD_DIM), lambda b, m: (m, 0)),
                  pl.BlockSpec((1, HEAD_DIM), lambda b, m: (0, 0))],
        out_specs=pl.BlockSpec((1, tm, KV_WIDTH), lambda b, m: (b, m, 0)),
        out_shape=jax.ShapeDtypeStruct((B, L, KV_WIDTH), bf16),
        compiler_params=_params(("parallel", "parallel")),
    )(proj, cos, sin, gain.reshape(1, HEAD_DIM))


def _attn_kernel(q_ref, k_ref, v1_ref, cos_ref, sin_ref, gain_ref, gate_ref, o_ref):
    k = k_ref[0]
    v1 = v1_ref[0]
    for h in range(ATT_GROUP):
        sl = slice(h * HEAD_DIM, (h + 1) * HEAD_DIM)
        qn = _head_norm(q_ref[0, :, sl].astype(f32), gain_ref[...])
        q = (_rope(qn, cos_ref[...], sin_ref[...]) * _SM_SCALE_LOG2E).astype(bf16)
        s = lax.dot_general(q, k, (((1,), (1,)), ((), ())), preferred_element_type=f32)
        p = jnp.exp2(s - jnp.max(s, axis=-1, keepdims=True))
        ol = jnp.dot(p.astype(bf16), v1, preferred_element_type=f32)
        o = ol[:, :HEAD_DIM] / ol[:, HEAD_DIM:HEAD_DIM + 1]
        o_ref[0, :, sl] = (o * _silu(gate_ref[0, :, sl].astype(f32))).astype(bf16)


def _attention(proj, k_all, vt_all, cos, sin, gain, tq):
    B, L, _ = proj.shape
    Lk = k_all.shape[1]
    gw = ATT_GROUP * HEAD_DIM
    qb, gb = P_Q // gw, P_ATT_GATE // gw
    return pl.pallas_call(
        _attn_kernel,
        grid=(B, ATT_KV_HEADS, L // tq),
        in_specs=[pl.BlockSpec((1, tq, gw), lambda b, h, i: (b, i, qb + h)),
                  pl.BlockSpec((1, Lk, HEAD_DIM), lambda b, h, i: (b, 0, h)),
                  pl.BlockSpec((1, Lk, 2 * HEAD_DIM), lambda b, h, i: (b, 0, h)),
                  pl.BlockSpec((tq, HEAD_DIM), lambda b, h, i: (i, 0)),
                  pl.BlockSpec((tq, HEAD_DIM), lambda b, h, i: (i, 0)),
                  pl.BlockSpec((1, HEAD_DIM), lambda b, h, i: (0, 0)),
                  pl.BlockSpec((1, tq, gw), lambda b, h, i: (b, i, gb + h))],
        out_specs=pl.BlockSpec((1, tq, gw), lambda b, h, i: (b, i, h)),
        out_shape=jax.ShapeDtypeStruct((B, L, ATT_WIDTH), bf16),
        compiler_params=_params(("parallel", "parallel", "parallel")),
    )(proj, k_all, vt_all, cos, sin, gain.reshape(1, HEAD_DIM), proj)


def _gmlp_kernel(p_ref, gate_ref, lng_ref, lnb_ref, ws_ref, bs_ref, o_ref):
    tm = p_ref.shape[1]
    p = p_ref[0].astype(f32)
    uv = 0.5 * p * (1.0 + lax.erf(p * (2.0 ** -0.5)))
    u = uv[:, :GM_WIDTH]
    v = uv[:, GM_WIDTH:]
    vc = v - jnp.mean(v, axis=-1, keepdims=True)
    var = jnp.mean(vc * vc, axis=-1, keepdims=True)
    vn = (vc * lax.rsqrt(var + LN_EPS) * lng_ref[...] + lnb_ref[...]).astype(bf16)
    ug = u * _silu(gate_ref[0].astype(f32))
    for n in range(tm // CHUNK):
        rows = slice(n * CHUNK, (n + 1) * CHUNK)
        for g in range(GM_GROUPS):
            cols = slice(g * GM_GROUP_DIM, (g + 1) * GM_GROUP_DIM)
            y = jnp.dot(ws_ref[g], vn[rows, cols], preferred_element_type=f32) + bs_ref[:, cols]
            o_ref[0, rows, cols] = (ug[rows, cols] * y).astype(bf16)


def _gmlp(proj, ln_g, ln_b, ws, bs_full, tm):
    B, L, _ = proj.shape
    pb, gb = P_GM // (2 * GM_WIDTH), P_GM_GATE // GM_WIDTH
    return pl.pallas_call(
        _gmlp_kernel,
        grid=(B, L // tm),
        in_specs=[pl.BlockSpec((1, tm, 2 * GM_WIDTH), lambda b, m: (b, m, pb)),
                  pl.BlockSpec((1, tm, GM_WIDTH), lambda b, m: (b, m, gb)),
                  pl.BlockSpec((1, GM_WIDTH), lambda b, m: (0, 0)),
                  pl.BlockSpec((1, GM_WIDTH), lambda b, m: (0, 0)),
                  pl.BlockSpec((GM_GROUPS, CHUNK, CHUNK), lambda b, m: (0, 0, 0)),
                  pl.BlockSpec((CHUNK, GM_WIDTH), lambda b, m: (0, 0))],
        out_specs=pl.BlockSpec((1, tm, GM_WIDTH), lambda b, m: (b, m, 0)),
        out_shape=jax.ShapeDtypeStruct((B, L, GM_WIDTH), bf16),
        compiler_params=_params(("parallel", "parallel")),
    )(proj, proj, ln_g.reshape(1, GM_WIDTH), ln_b.reshape(1, GM_WIDTH), ws, bs_full)


def _shortconv_kernel(p_ref, w_ref, b_ref, o_ref):
    L = p_ref.shape[1]
    x = p_ref[0].astype(f32)
    row = lax.broadcasted_iota(jnp.int32, x.shape, 0)
    xm = jnp.where(row == 0, 0.0, pltpu.roll(x, 1, 0))
    xp = jnp.where(row == L - 1, 0.0, pltpu.roll(x, L - 1, 0))
    o_ref[0] = (xm * w_ref[0:1, :] + x * w_ref[1:2, :] + xp * w_ref[2:3, :] + b_ref[...]).astype(bf16)


def _shortconv(proj, w, b, tc):
    B, L, _ = proj.shape
    W = w.shape[1]
    return pl.pallas_call(
        _shortconv_kernel,
        grid=(B, W // tc),
        in_specs=[pl.BlockSpec((1, L, tc), lambda b, c: (b, 0, c)),
                  pl.BlockSpec((3, tc), lambda b, c: (0, c)),
                  pl.BlockSpec((1, tc), lambda b, c: (0, c))],
        out_specs=pl.BlockSpec((1, L, tc), lambda b, c: (b, 0, c)),
        out_shape=jax.ShapeDtypeStruct((B, L, W), bf16),
        compiler_params=_params(("parallel", "parallel")),
    )(proj, w, b.reshape(1, W))


def _filter_embedding(L):
    n = np.arange(2 * L)
    pos = np.where(n < L, n, 2 * L - n)
    pos = np.where(n == L, 0, pos)
    t = np.linspace(0.0, 1.0, L)[pos]
    bands = np.linspace(1e-4, HY_BANDS - 1, HY_BANDS)
    ang = (2.0 * math.pi / L) * pos[:, None] * bands[None, :]
    z = np.zeros((2 * L, LANES), np.float64)
    z[:, 0] = t
    z[:, 1:1 + HY_BANDS] = np.cos(ang)
    z[:, 1 + HY_BANDS:HY_EMB] = -np.sin(ang)
    z[:, HY_EMB] = (n < L)
    z[:, HY_EMB + 1] = (n > L) | (n == 0)
    return jnp.asarray(z, f32)


def _filter_kernel(z_ref, w1_ref, b1_ref, w2_ref, b2_ref, freq_ref, w3_ref, dec_ref, o_ref):
    hp = lax.Precision.HIGHEST
    C = o_ref.shape[2]
    z = z_ref[...]
    t = z[:, 0:1]
    mf = z[:, HY_EMB:HY_EMB + 1]
    mb = z[:, HY_EMB + 1:HY_EMB + 2]
    h = jnp.sin(freq_ref[0:1, :] * (jnp.dot(z, w1_ref[...], precision=hp, preferred_element_type=f32) + b1_ref[...]))
    h = jnp.sin(freq_ref[1:2, :] * (jnp.dot(h, w2_ref[...], precision=hp, preferred_element_type=f32) + b2_ref[...]))
    for blk in range(2 * HY_ORDER):
        cols = slice(blk * C, (blk + 1) * C)
        taps = jnp.dot(h, w3_ref[:, cols], precision=hp, preferred_element_type=f32)
        taps = (mb if blk % 2 else mf) * taps * jnp.exp(-t * jnp.abs(dec_ref[:, cols]))
        if blk % 2:
            o_ref[blk // 2] = o_ref[blk // 2] + taps
        else:
            o_ref[blk // 2] = taps


def _filters(L, w1, b1, w2, b2, w3, freq, decay, tr):
    H = HY_FILTER_HIDDEN
    C = HY_WIDTH
    W = 2 * HY_ORDER * C
    z = _filter_embedding(L)
    w1p = jnp.zeros((LANES, H), f32).at[:HY_EMB].set(w1)
    full = lambda shape: pl.BlockSpec(shape, lambda r: (0,) * len(shape))
    return pl.pallas_call(
        _filter_kernel,
        grid=(2 * L // tr,),
        in_specs=[pl.BlockSpec((tr, LANES), lambda r: (r, 0)),
                  full((LANES, H)), full((1, H)), full((H, H)), full((1, H)), full((2, H)),
                  full((H, W)), full((1, W))],
        out_specs=pl.BlockSpec((HY_ORDER, tr, C), lambda r: (0, r, 0)),
        out_shape=jax.ShapeDtypeStruct((HY_ORDER, 2 * L, C), f32),
        compiler_params=_params(("parallel",)),
    )(z, w1p, b1.reshape(1, H), w2, b2.reshape(1, H), freq, w3, decay.reshape(1, W))


def _fft_dims(L):
    n2 = SUBLANES
    while (2 * n2) * (2 * n2) <= L:
        n2 *= 2
    return n2, L // n2, 2 * (L // n2), n2 + SUBLANES


def _stack(fr, fi):
    return np.block([[fr, -fi], [fi, fr]])


@functools.lru_cache(maxsize=None)
def _dft_tables(L):
    N2, N1h, N1, _ = _fft_dims(L)
    N = 2 * L
    k1 = np.arange(N1)
    a1 = -2.0 * np.pi * np.outer(k1, np.arange(N1)) / N1
    f1 = _stack(np.cos(a1[:, :N1h]), np.sin(a1[:, :N1h]))
    f1k = np.concatenate([np.cos(a1), np.sin(a1)], axis=0)
    f3 = _stack(np.cos(-a1[:, :N1h].T), np.sin(-a1[:, :N1h].T))
    n2 = np.arange(N2)
    a2 = -2.0 * np.pi * (np.outer(n2, n2)[None] / N2 + (k1[:, None, None] * n2[None, None, :]) / N)
    f2 = np.stack([_stack(np.cos(a), np.sin(a)) for a in a2])
    a2t = -np.transpose(a2, (0, 2, 1))
    g2 = np.stack([_stack(np.cos(a), np.sin(a)) for a in a2t])
    return tuple(jnp.asarray(m, bf16) for m in (f1, f1k, f2, g2, f3))


def _load_cols(ref, lead, rows):
    return jnp.concatenate([ref[lead + (t, rows, slice(None))] for t in range(ref.shape[-3])], axis=1)


def _store_cols(ref, lead, rows, val):
    for t in range(ref.shape[-3]):
        ref[lead + (t, rows, slice(None))] = val[:, t * LANES:(t + 1) * LANES]


def _grouped_loop(n, group, load, compute, store):
    def body(i, c):
        idx = [i * group + u for u in range(group)]
        vals = [load(ix) for ix in idx]
        outs = [compute(ix, v) for ix, v in zip(idx, vals)]
        for ix, o in zip(idx, outs):
            store(ix, o)
        return c

    lax.fori_loop(0, n // group, body, 0)


def _load_ri(ref, rows):
    return jnp.concatenate([_load_cols(ref, (ri,), rows) for ri in range(2)], axis=0).astype(bf16)


def _store_ri(ref, rows, val):
    h = val.shape[0] // 2
    for ri in range(2):
        _store_cols(ref, (ri,), rows, val[ri * h:(ri + 1) * h])


def _spectrum_kernel(k_ref, f1k_ref, f2_ref, o_ref, a_scr, *, L):
    N2, N1h, N1, P = _fft_dims(L)
    T = a_scr.shape[1]
    inv_n = 1.0 / (2 * L)

    def load1(n2):
        return jnp.concatenate([k_ref[0, pl.ds(n2, N1, stride=N2), t * LANES:(t + 1) * LANES]
                                for t in range(T)], axis=1).astype(bf16)

    _grouped_loop(N2, FFT_GROUP, load1,
                  lambda n2, w: jnp.dot(f1k_ref[...], w, preferred_element_type=f32),
                  lambda n2, r: _store_ri(a_scr, pl.ds(n2, N1, stride=P), r))

    def store2(k1, x):
        out_rows = pl.ds(pl.multiple_of(k1 * N2, N2), N2)
        o_ref[0, 0, out_rows, :] = x[:N2].astype(bf16)
        o_ref[0, 1, out_rows, :] = x[N2:].astype(bf16)

    _grouped_loop(N1, FFT_GROUP,
                  lambda k1: _load_ri(a_scr, pl.ds(pl.multiple_of(k1 * P, SUBLANES), N2)),
                  lambda k1, a: jnp.dot(f2_ref[k1], a, preferred_element_type=f32) * inv_n,
                  store2)


def _spectrum(k, tc):
    _, N, C = k.shape
    L = N // 2
    N2, N1h, N1, P = _fft_dims(L)
    _, f1k, f2, _, _ = _dft_tables(L)
    return pl.pallas_call(
        functools.partial(_spectrum_kernel, L=L),
        grid=(HY_ORDER, C // tc),
        in_specs=[pl.BlockSpec((1, N, tc), lambda o, c: (o, 0, c)),
                  pl.BlockSpec(f1k.shape, lambda o, c: (0, 0)),
                  pl.BlockSpec(f2.shape, lambda o, c: (0, 0, 0))],
        out_specs=pl.BlockSpec((1, 2, N, tc), lambda o, c: (o, 0, 0, c)),
        out_shape=jax.ShapeDtypeStruct((HY_ORDER, 2, N, C), bf16),
        scratch_shapes=[pltpu.VMEM((2, tc // LANES, N1 * P, LANES), f32)],
        compiler_params=_params(("parallel", "parallel")),
    )(k, f1k, f2)


def _fftconv_kernel(*refs, L, chunk, has_gate):
    if has_gate:
        u_in, u_ep, g_ref, gate_ref, bias_ref, kf_ref, f1_ref, f2_ref, g2_ref, f3_ref, o_ref, z_scr, a_scr = refs
    else:
        u_in, u_ep, g_ref, bias_ref, kf_ref, f1_ref, f2_ref, g2_ref, f3_ref, o_ref, z_scr, a_scr = refs
    N2, N1h, N1, P = _fft_dims(L)
    J = L // chunk
    nb = chunk // N2
    j = pl.program_id(2)

    @pl.when(j < J)
    def _load():
        for ri in range(2):
            for blk in range(nb):
                rows = pl.ds(pl.multiple_of((j * nb + blk) * P, SUBLANES), N2)
                _store_cols(z_scr, (ri,), rows, u_in[ri, blk * N2:(blk + 1) * N2, :].astype(f32))

    @pl.when(j == J - 1)
    def _transform():
        _grouped_loop(N2, FFT_GROUP,
                      lambda n2: _load_ri(z_scr, pl.ds(n2, N1h, stride=P)),
                      lambda n2, w: jnp.dot(f1_ref[...], w, preferred_element_type=f32),
                      lambda n2, r: _store_ri(a_scr, pl.ds(n2, N1, stride=P), r))

        def block_rows(k1):
            return pl.ds(pl.multiple_of(k1 * P, SUBLANES), N2)

        def middle(k1, a):
            x = jnp.dot(f2_ref[k1], a, preferred_element_type=f32)
            krows = pl.ds(pl.multiple_of(k1 * N2, N2), N2)
            kr = kf_ref[0, 0, krows, :].astype(f32)
            ki = kf_ref[0, 1, krows, :].astype(f32)
            xr, xi = x[:N2], x[N2:]
            y = jnp.concatenate([xr * kr - xi * ki, xr * ki + xi * kr], axis=0).astype(bf16)
            return jnp.dot(g2_ref[k1], y, preferred_element_type=f32)

        _grouped_loop(N1, FFT_GROUP,
                      lambda k1: _load_ri(a_scr, block_rows(k1)),
                      middle,
                      lambda k1, b: _store_ri(a_scr, block_rows(k1), b))

        _grouped_loop(N2, FFT_GROUP,
                      lambda n2: _load_ri(a_scr, pl.ds(n2, N1, stride=P)),
                      lambda n2, b: jnp.dot(f3_ref[...], b, preferred_element_type=f32),
                      lambda n2, y: _store_ri(z_scr, pl.ds(n2, N1h, stride=P), y))

    @pl.when(j >= J)
    def _epilogue():
        for ri in range(2):
            for blk in range(nb):
                rows = pl.ds(pl.multiple_of(((j - J) * nb + blk) * P, SUBLANES), N2)
                y = _load_cols(z_scr, (ri,), rows)
                sl = slice(blk * N2, (blk + 1) * N2)
                u = u_ep[ri, sl, :].astype(f32)
                o = g_ref[ri, sl, :].astype(f32) * (y + bias_ref[...] * u)
                if has_gate:
                    o = o * _silu(gate_ref[ri, sl, :].astype(f32))
                o_ref[ri, sl, :] = o.astype(bf16)


def _fftconv(u, u_cb, g, g_cb, gate, gate_cb, bias, kf, order, tc):
    B, L, _ = u.shape
    C = HY_WIDTH
    N2, N1h, N1, P = _fft_dims(L)
    chunk = min(L, FFT_CHUNK)
    J = L // chunk
    f1, _, f2, g2, f3 = _dft_tables(L)
    T = tc // LANES
    in_map = lambda cb: (lambda c, p, j: (p, jnp.minimum(j, J - 1), cb + c))
    ep_map = lambda cb: (lambda c, p, j: (p, jnp.maximum(j - J, 0), cb + c))
    const = lambda nd: (lambda c, p, j: (0,) * nd)
    once = pl.Buffered(1)
    ops = [u, u, g]
    specs = [pl.BlockSpec((2, chunk, tc), in_map(u_cb)),
             pl.BlockSpec((2, chunk, tc), ep_map(u_cb)),
             pl.BlockSpec((2, chunk, tc), ep_map(g_cb))]
    if gate is not None:
        ops.append(gate)
        specs.append(pl.BlockSpec((2, chunk, tc), ep_map(gate_cb)))
    ops += [bias.reshape(HY_ORDER, 1, C), kf, f1, f2, g2, f3]
    specs += [pl.BlockSpec((None, 1, tc), lambda c, p, j: (order, 0, c)),
              pl.BlockSpec((1, 2, 2 * L, tc), lambda c, p, j: (order, 0, 0, c), pipeline_mode=once),
              pl.BlockSpec(f1.shape, const(2), pipeline_mode=once),
              pl.BlockSpec(f2.shape, const(3), pipeline_mode=once),
              pl.BlockSpec(g2.shape, const(3), pipeline_mode=once),
              pl.BlockSpec(f3.shape, const(2), pipeline_mode=once)]
    return pl.pallas_call(
        functools.partial(_fftconv_kernel, L=L, chunk=chunk, has_gate=gate is not None),
        grid=(C // tc, B // 2, 2 * J),
        in_specs=specs,
        out_specs=pl.BlockSpec((2, chunk, tc), ep_map(0)),
        out_shape=jax.ShapeDtypeStruct((B, L, C), bf16),
        scratch_shapes=[pltpu.VMEM((2, T, N1h * P, LANES), f32),
                        pltpu.VMEM((2, T, N1 * P, LANES), f32)],
        compiler_params=_params(("parallel", "parallel", "arbitrary")),
    )(*ops)


def _merge_kernel(hy_ref, at_ref, gm_ref, mg_ref, x_ref, gate_ref, gpost_ref,
                  why_ref, wat_ref, wgm_ref, wout_ref, o_ref):
    D = x_ref.shape[2]
    acc = None
    for i, (br, w) in enumerate(((hy_ref, why_ref), (at_ref, wat_ref), (gm_ref, wgm_ref))):
        y = jnp.dot(br[0], w[...], preferred_element_type=f32)
        y = jax.nn.sigmoid(mg_ref[0, :, i * D:(i + 1) * D].astype(f32)) * y
        acc = y if acc is None else acc + y
    o = jnp.dot(acc.astype(bf16), wout_ref[...], preferred_element_type=f32)
    r = o * lax.rsqrt(jnp.mean(o * o, axis=-1, keepdims=True) + RMS_EPS) * gpost_ref[...]
    o_ref[0] = x_ref[0] + gate_ref[0] * r


def _merge(hy, at, gm, proj, x, mod, g_post, w_hy, w_at, w_gm, w_out, tm):
    B, L, D = x.shape
    mb = P_MERGE // (3 * D)
    row = lambda b, m: (b, m, 0)
    wspec = pl.BlockSpec((D, D), lambda b, m: (0, 0))
    return pl.pallas_call(
        _merge_kernel,
        grid=(B, L // tm),
        in_specs=[pl.BlockSpec((1, tm, D), row), pl.BlockSpec((1, tm, D), row), pl.BlockSpec((1, tm, D), row),
                  pl.BlockSpec((1, tm, 3 * D), lambda b, m: (b, m, mb)),
                  pl.BlockSpec((1, tm, D), row),
                  pl.BlockSpec((1, 1, D), lambda b, m: (b, 0, 2)),
                  pl.BlockSpec((1, D), lambda b, m: (0, 0)),
                  wspec, wspec, wspec, wspec],
        out_specs=pl.BlockSpec((1, tm, D), row),
        out_shape=jax.ShapeDtypeStruct((B, L, D), f32),
        compiler_params=_params(("parallel", "parallel")),
    )(hy, at, gm, proj, x, mod, g_post.reshape(1, D), w_hy, w_at, w_gm, w_out)


def _rope_tables(L):
    pos = np.arange(L)
    inv = np.power(ROPE_THETA, -np.arange(0, AXIS_DIM, 2) / AXIS_DIM)
    ar = (pos // GRID_W)[:, None] * inv
    ac = (pos % GRID_W)[:, None] * inv
    cos = np.concatenate([np.cos(ar), np.cos(ar), np.cos(ac), np.cos(ac)], axis=1)
    sin = np.concatenate([-np.sin(ar), np.sin(ar), -np.sin(ac), np.sin(ac)], axis=1)
    return jnp.asarray(cos, f32), jnp.asarray(sin, f32)


def _identity_rope(L):
    return jnp.ones((L, HEAD_DIM), f32), jnp.zeros((L, HEAD_DIM), f32)


def _hyena(proj, lp):
    B, L, _ = proj.shape
    tc = FFT_COLS
    nct = HY_WIDTH // tc
    hyc = _shortconv(proj, lp['hy_conv_w'], lp['hy_conv_b'], tc)
    k = _filters(L, lp['hy_w1'], lp['hy_b1'], lp['hy_w2'], lp['hy_b2'], lp['hy_w3'], lp['hy_freq'],
                 lp['hy_decay'], min(2 * L, 256))
    kf = _spectrum(k, LANES)
    z1 = _fftconv(hyc, 2 * nct, hyc, 0, None, 0, lp['hy_bias'], kf, 0, tc)
    return _fftconv(z1, 0, hyc, nct, proj, P_HY_GATE // tc, lp['hy_bias'], kf, 1, tc)


def _values_with_ones(v):
    B, Lk, _ = v.shape
    v4 = v.reshape(B, Lk, ATT_KV_HEADS, HEAD_DIM)
    return jnp.concatenate([v4, jnp.ones_like(v4)], axis=3).reshape(B, Lk, 2 * KV_WIDTH)


def _mixer(proj, k_all, v_all, rope_q, lp, tq, tm_gm):
    att = _attention(proj, k_all, _values_with_ones(v_all), rope_q[0], rope_q[1], lp['q_gain'], tq)
    hy = _hyena(proj, lp)
    gm = _gmlp(proj, lp['gm_ln_g'], lp['gm_ln_b'], lp['gm_ws'], lp['gm_bs_full'], tm_gm)
    return hy, att, gm


def _layer(x, xc, mod_x, mod_c, rope, lp, ctx_out):
    B, L, D = x.shape
    C = xc.shape[1]
    w_in = lp['w_in']
    proj = _inproj(x, mod_x, lp['g_pre'], w_in, min(L, 1024), 1280)
    wc = w_in if ctx_out else w_in[:, P_K:]
    projc = _inproj(xc, mod_c, lp['g_pre'], wc, C, min(wc.shape[1], 1280))
    kv_off = P_K if ctx_out else 0
    ident = _identity_rope(C)
    kc = _kprep(projc, kv_off, ident[0], ident[1], lp['k_gain'], C)
    vc = projc[:, :, kv_off + KV_WIDTH:kv_off + 2 * KV_WIDTH]
    k = _kprep(proj, P_K, rope[0], rope[1], lp['k_gain'], min(L, 1024))
    k_all = jnp.concatenate([kc, k], axis=1)
    v_all = jnp.concatenate([vc, proj[:, :, P_V:P_V + KV_WIDTH]], axis=1)

    hy, att, gm = _mixer(proj, k_all, v_all, rope, lp, 256, 256)
    x_new = _merge(hy, att, gm, proj, x, mod_x, lp['g_post'], lp['w_hy_o'], lp['w_att_o'], lp['w_gm_o'],
                   lp['w_out'], min(L, 512))
    if not ctx_out:
        return x_new, xc
    hyc, attc, gmc = _mixer(projc, kc, vc, ident, lp, C, min(C, 256))
    xc_new = _merge(hyc, attc, gmc, projc, xc, mod_c, lp['g_post'], lp['w_hy_o'], lp['w_att_o'], lp['w_gm_o'],
                    lp['w_out'], C)
    return x_new, xc_new


def kernel(x, c, ctx, c_ctx, w_mod, b_mod, g_pre, g_post, w_in, hy_conv_w, hy_conv_b, hy_w1, hy_b1, hy_w2, hy_b2, hy_w3, hy_freq, hy_decay, hy_bias, q_gain, k_gain, gm_ln_g, gm_ln_b, gm_ws, gm_bs, w_hy_o, w_att_o, w_gm_o, w_out):
    B, L, D = x.shape
    rope = _rope_tables(L)

    R = -(-(B + 1) // (2 * SUBLANES)) * (2 * SUBLANES)
    cc = jnp.zeros((R, D), f32).at[:B].set(c).at[B].set(c_ctx)
    mod = _modulation(cc, w_mod, b_mod)

    w_in_p = jnp.concatenate([w_in[:, :, a:b].astype(bf16) for a, b in _COL_RANGES], axis=2)
    gm_bs_full = jnp.repeat(jnp.swapaxes(gm_bs, 1, 2), GM_GROUP_DIM, axis=2)

    xc = ctx
    for i in range(DEPTH):
        lp = {
            'g_pre': g_pre[i], 'g_post': g_post[i], 'w_in': w_in_p[i],
            'hy_conv_w': hy_conv_w[i], 'hy_conv_b': hy_conv_b[i],
            'hy_w1': hy_w1[i], 'hy_b1': hy_b1[i], 'hy_w2': hy_w2[i], 'hy_b2': hy_b2[i],
            'hy_w3': hy_w3[i], 'hy_freq': hy_freq[i], 'hy_decay': hy_decay[i], 'hy_bias': hy_bias[i],
            'q_gain': q_gain[i], 'k_gain': k_gain[i], 'gm_ln_g': gm_ln_g[i], 'gm_ln_b': gm_ln_b[i],
            'gm_ws': gm_ws[i].astype(bf16), 'gm_bs_full': gm_bs_full[i],
            'w_hy_o': w_hy_o[i].astype(bf16), 'w_att_o': w_att_o[i].astype(bf16),
            'w_gm_o': w_gm_o[i].astype(bf16), 'w_out': w_out[i].astype(bf16),
        }
        mod_x = mod[i, :B].reshape(B, 1, 3 * D)
        mod_c = jnp.broadcast_to(mod[i, B].reshape(1, 1, 3 * D), (B, 1, 3 * D))
        x, xc = _layer(x, xc, mod_x, mod_c, rope, lp, i < DEPTH - 1)
    return x
```

```python
import functools
import math

import numpy as np
import jax
import jax.numpy as jnp
from jax import lax
from jax.experimental import pallas as pl
from jax.experimental.pallas import tpu as pltpu

f32 = jnp.float32
bf16 = jnp.bfloat16

D_MODEL = 1024
DEPTH = 2
GRID_W = 64
RMS_EPS = 1e-6
LN_EPS = 1e-5

HY_WIDTH = D_MODEL
HY_ORDER = 2
HY_BANDS = 16
HY_EMB = 1 + 2 * HY_BANDS
HY_FILTER_HIDDEN = 64

HEAD_DIM = 128
ATT_HEADS = D_MODEL // HEAD_DIM
ATT_KV_HEADS = 2
ATT_GROUP = ATT_HEADS // ATT_KV_HEADS
ATT_WIDTH = ATT_HEADS * HEAD_DIM
KV_WIDTH = ATT_KV_HEADS * HEAD_DIM
AXIS_DIM = HEAD_DIM // 2
ROPE_THETA = 10000.0

GM_WIDTH = D_MODEL
GM_GROUPS = 8
GM_GROUP_DIM = GM_WIDTH // GM_GROUPS
CHUNK = 128

OFF_HY = 0
OFF_HY_GATE = OFF_HY + (HY_ORDER + 1) * HY_WIDTH
OFF_Q = OFF_HY_GATE + HY_WIDTH
OFF_K = OFF_Q + ATT_WIDTH
OFF_V = OFF_K + KV_WIDTH
OFF_ATT_GATE = OFF_V + KV_WIDTH
OFF_GM = OFF_ATT_GATE + ATT_WIDTH
OFF_GM_GATE = OFF_GM + 2 * GM_WIDTH
OFF_MERGE = OFF_GM_GATE + GM_WIDTH
IN_WIDTH = OFF_MERGE + 3 * D_MODEL

P_HY = 0
P_HY_GATE = 3 * D_MODEL
P_Q = 4 * D_MODEL
P_ATT_GATE = 5 * D_MODEL
P_GM = 6 * D_MODEL
P_GM_GATE = 8 * D_MODEL
P_MERGE = 9 * D_MODEL
P_K = 12 * D_MODEL
P_V = P_K + KV_WIDTH
_COL_RANGES = ((OFF_HY, OFF_K), (OFF_ATT_GATE, IN_WIDTH), (OFF_K, OFF_ATT_GATE))

LANES = 128
SUBLANES = 8
FFT_COLS = 256
FFT_GROUP = 16
INPROJ_COLS = 2560
FFT_CHUNK = 512
VMEM_LIMIT = 52 * 1024 * 1024

_SM_SCALE_LOG2E = (HEAD_DIM ** -0.5) * math.log2(math.e)


def _params(sem, vmem=VMEM_LIMIT):
    return pltpu.CompilerParams(dimension_semantics=sem, vmem_limit_bytes=vmem)


def _silu(x):
    return x * jax.nn.sigmoid(x)


def _mod_kernel(cc_ref, w_ref, b_ref, o_ref):
    s = _silu(cc_ref[...]).astype(bf16)
    o_ref[0] = jnp.dot(s, w_ref[0].astype(bf16), preferred_element_type=f32) + b_ref[0]


def _modulation(cc, w_mod, b_mod):
    R, D = cc.shape
    tn = D
    return pl.pallas_call(
        _mod_kernel,
        grid=(DEPTH, 3 * D // tn),
        in_specs=[pl.BlockSpec((R, D), lambda i, n: (0, 0)),
                  pl.BlockSpec((1, D, tn), lambda i, n: (i, 0, n)),
                  pl.BlockSpec((1, 1, tn), lambda i, n: (i, 0, n))],
        out_specs=pl.BlockSpec((1, R, tn), lambda i, n: (i, 0, n)),
        out_shape=jax.ShapeDtypeStruct((DEPTH, R, 3 * D), f32),
        compiler_params=_params(("parallel", "parallel")),
    )(cc, w_mod, b_mod.reshape(DEPTH, 1, 3 * D))


def _inproj_kernel(x_ref, mod_ref, g_ref, w_ref, o_ref, h_scr):
    D = x_ref.shape[2]

    @pl.when(pl.program_id(2) == 0)
    def _():
        x = x_ref[0]
        r = lax.rsqrt(jnp.mean(x * x, axis=-1, keepdims=True) + RMS_EPS)
        shift = mod_ref[0, :, 0:D]
        scale = mod_ref[0, :, D:2 * D]
        h_scr[...] = ((x * r) * g_ref[...] * (1.0 + scale) + shift).astype(bf16)

    o_ref[0] = jnp.dot(h_scr[...], w_ref[...], preferred_element_type=f32).astype(o_ref.dtype)


def _inproj(x, mod, g_pre, w, tm, tn):
    B, L, D = x.shape
    N = w.shape[1]
    return pl.pallas_call(
        _inproj_kernel,
        grid=(B, L // tm, N // tn),
        in_specs=[pl.BlockSpec((1, tm, D), lambda b, m, n: (b, m, 0)),
                  pl.BlockSpec((1, 1, 3 * D), lambda b, m, n: (b, 0, 0)),
                  pl.BlockSpec((1, D), lambda b, m, n: (0, 0)),
                  pl.BlockSpec((D, tn), lambda b, m, n: (0, n))],
        out_specs=pl.BlockSpec((1, tm, tn), lambda b, m, n: (b, m, n)),
        out_shape=jax.ShapeDtypeStruct((B, L, N), bf16),
        scratch_shapes=[pltpu.VMEM((tm, D), bf16)],
        compiler_params=_params(("parallel", "parallel", "arbitrary")),
    )(x, mod, g_pre.reshape(1, D), w)


def _rope(x, cos, sin):
    lane = lax.broadcasted_iota(jnp.int32, x.shape, 1)
    first = (lane & (AXIS_DIM // 2)) == 0
    rot = jnp.where(first, pltpu.roll(x, HEAD_DIM - AXIS_DIM // 2, 1), pltpu.roll(x, AXIS_DIM // 2, 1))
    return x * cos + rot * sin


def _head_norm(x, gain):
    r = lax.rsqrt(jnp.mean(x * x, axis=-1, keepdims=True) + RMS_EPS)
    return x * r * gain


def _kprep_kernel(k_ref, cos_ref, sin_ref, gain_ref, o_ref):
    for h in range(ATT_KV_HEADS):
        sl = slice(h * HEAD_DIM, (h + 1) * HEAD_DIM)
        kn = _head_norm(k_ref[0, :, sl].astype(f32), gain_ref[...])
        o_ref[0, :, sl] = _rope(kn, cos_ref[...], sin_ref[...]).astype(bf16)


def _kprep(proj, col_off, cos, sin, gain, tm):
    B, L, _ = proj.shape
    cb = col_off // KV_WIDTH
    return pl.pallas_call(
        _kprep_kernel,
        grid=(B, L // tm),
        in_specs=[pl.BlockSpec((1, tm, KV_WIDTH), lambda b, m: (b, m, cb)),
                  pl.BlockSpec((tm, HEAD_DIM), lambda b, m: (m, 0)),
                  pl.BlockSpec((tm, HEAD_DIM), lambda b, m: (m, 0)),
                  pl.BlockSpec((1, HEAD_DIM), lambda b, m: (0, 0))],
        out_specs=pl.BlockSpec((1, tm, KV_WIDTH), lambda b, m: (b, m, 0)),
        out_shape=jax.ShapeDtypeStruct((B, L, KV_WIDTH), bf16),
        compiler_params=_params(("parallel", "parallel")),
    )(proj, cos, sin, gain.reshape(1, HEAD_DIM))


def _attn_kernel(q_ref, k_ref, v1_ref, cos_ref, sin_ref, gain_ref, gate_ref, o_ref):
    k = k_ref[0]
    v1 = v1_ref[0]
    def scores(h):
        sl = slice(h * HEAD_DIM, (h + 1) * HEAD_DIM)
        qn = _head_norm(q_ref[0, :, sl].astype(f32), gain_ref[...])
        q = (_rope(qn, cos_ref[...], sin_ref[...]) * _SM_SCALE_LOG2E).astype(bf16)
        return lax.dot_general(q, k, (((1,), (1,)), ((), ())), preferred_element_type=f32)

    s_next = scores(0)
    for h in range(ATT_GROUP):
        sl = slice(h * HEAD_DIM, (h + 1) * HEAD_DIM)
        s = s_next
        if h + 1 < ATT_GROUP:
            s_next = scores(h + 1)
        p = jnp.exp2(s - jnp.max(s, axis=-1, keepdims=True))
        ol = jnp.dot(p.astype(bf16), v1, preferred_element_type=f32)
        o = ol[:, :HEAD_DIM] / ol[:, HEAD_DIM:HEAD_DIM + 1]
        o_ref[0, :, sl] = (o * _silu(gate_ref[0, :, sl].astype(f32))).astype(bf16)


def _attention(proj, k_all, vt_all, cos, sin, gain, tq):
    B, L, _ = proj.shape
    Lk = k_all.shape[1]
    gw = ATT_GROUP * HEAD_DIM
    qb, gb = P_Q // gw, P_ATT_GATE // gw
    return pl.pallas_call(
        _attn_kernel,
        grid=(B, ATT_KV_HEADS, L // tq),
        in_specs=[pl.BlockSpec((1, tq, gw), lambda b, h, i: (b, i, qb + h)),
                  pl.BlockSpec((1, Lk, HEAD_DIM), lambda b, h, i: (b, 0, h)),
                  pl.BlockSpec((1, Lk, 2 * HEAD_DIM), lambda b, h, i: (b, 0, h)),
                  pl.BlockSpec((tq, HEAD_DIM), lambda b, h, i: (i, 0)),
                  pl.BlockSpec((tq, HEAD_DIM), lambda b, h, i: (i, 0)),
                  pl.BlockSpec((1, HEAD_DIM), lambda b, h, i: (0, 0)),
                  pl.BlockSpec((1, tq, gw), lambda b, h, i: (b, i, gb + h))],
        out_specs=pl.BlockSpec((1, tq, gw), lambda b, h, i: (b, i, h)),
        out_shape=jax.ShapeDtypeStruct((B, L, ATT_WIDTH), bf16),
        compiler_params=_params(("parallel", "parallel", "parallel")),
    )(proj, k_all, vt_all, cos, sin, gain.reshape(1, HEAD_DIM), proj)


def _gmlp_kernel(p_ref, gate_ref, lng_ref, lnb_ref, ws_ref, bs_ref, o_ref):
    tm = p_ref.shape[1]
    p = p_ref[0].astype(f32)
    uv = 0.5 * p * (1.0 + lax.erf(p * (2.0 ** -0.5)))
    u = uv[:, :GM_WIDTH]
    v = uv[:, GM_WIDTH:]
    vc = v - jnp.mean(v, axis=-1, keepdims=True)
    var = jnp.mean(vc * vc, axis=-1, keepdims=True)
    vn = (vc * lax.rsqrt(var + LN_EPS) * lng_ref[...] + lnb_ref[...]).astype(bf16)
    ug = u * _silu(gate_ref[0].astype(f32))
    for n in range(tm // CHUNK):
        rows = slice(n * CHUNK, (n + 1) * CHUNK)
        for g in range(GM_GROUPS):
            cols = slice(g * GM_GROUP_DIM, (g + 1) * GM_GROUP_DIM)
            y = jnp.dot(ws_ref[g], vn[rows, cols], preferred_element_type=f32) + bs_ref[:, cols]
            o_ref[0, rows, cols] = (ug[rows, cols] * y).astype(bf16)


def _gmlp(proj, ln_g, ln_b, ws, bs_full, tm):
    B, L, _ = proj.shape
    pb, gb = P_GM // (2 * GM_WIDTH), P_GM_GATE // GM_WIDTH
    return pl.pallas_call(
        _gmlp_kernel,
        grid=(B, L // tm),
        in_specs=[pl.BlockSpec((1, tm, 2 * GM_WIDTH), lambda b, m: (b, m, pb)),
                  pl.BlockSpec((1, tm, GM_WIDTH), lambda b, m: (b, m, gb)),
                  pl.BlockSpec((1, GM_WIDTH), lambda b, m: (0, 0)),
                  pl.BlockSpec((1, GM_WIDTH), lambda b, m: (0, 0)),
                  pl.BlockSpec((GM_GROUPS, CHUNK, CHUNK), lambda b, m: (0, 0, 0)),
                  pl.BlockSpec((CHUNK, GM_WIDTH), lambda b, m: (0, 0))],
        out_specs=pl.BlockSpec((1, tm, GM_WIDTH), lambda b, m: (b, m, 0)),
        out_shape=jax.ShapeDtypeStruct((B, L, GM_WIDTH), bf16),
        compiler_params=_params(("parallel", "parallel")),
    )(proj, proj, ln_g.reshape(1, GM_WIDTH), ln_b.reshape(1, GM_WIDTH), ws, bs_full)


def _shortconv_kernel(p_ref, w_ref, b_ref, o_ref):
    L = p_ref.shape[1]
    x = p_ref[0].astype(f32)
    row = lax.broadcasted_iota(jnp.int32, x.shape, 0)
    xm = jnp.where(row == 0, 0.0, pltpu.roll(x, 1, 0))
    xp = jnp.where(row == L - 1, 0.0, pltpu.roll(x, L - 1, 0))
    o_ref[0] = (xm * w_ref[0:1, :] + x * w_ref[1:2, :] + xp * w_ref[2:3, :] + b_ref[...]).astype(bf16)


def _shortconv(proj, w, b, tc):
    B, L, _ = proj.shape
    W = w.shape[1]
    return pl.pallas_call(
        _shortconv_kernel,
        grid=(B, W // tc),
        in_specs=[pl.BlockSpec((1, L, tc), lambda b, c: (b, 0, c)),
                  pl.BlockSpec((3, tc), lambda b, c: (0, c)),
                  pl.BlockSpec((1, tc), lambda b, c: (0, c))],
        out_specs=pl.BlockSpec((1, L, tc), lambda b, c: (b, 0, c)),
        out_shape=jax.ShapeDtypeStruct((B, L, W), bf16),
        compiler_params=_params(("parallel", "parallel")),
    )(proj, w, b.reshape(1, W))


def _filter_embedding(L):
    n = np.arange(2 * L)
    pos = np.where(n < L, n, 2 * L - n)
    pos = np.where(n == L, 0, pos)
    t = np.linspace(0.0, 1.0, L)[pos]
    bands = np.linspace(1e-4, HY_BANDS - 1, HY_BANDS)
    ang = (2.0 * math.pi / L) * pos[:, None] * bands[None, :]
    z = np.zeros((2 * L, LANES), np.float64)
    z[:, 0] = t
    z[:, 1:1 + HY_BANDS] = np.cos(ang)
    z[:, 1 + HY_BANDS:HY_EMB] = -np.sin(ang)
    z[:, HY_EMB] = (n < L)
    z[:, HY_EMB + 1] = (n > L) | (n == 0)
    return jnp.asarray(z, f32)


def _filter_kernel(z_ref, w1_ref, b1_ref, w2_ref, b2_ref, freq_ref, w3_ref, dec_ref, o_ref):
    hp = lax.Precision.HIGHEST
    C = o_ref.shape[2]
    z = z_ref[...]
    t = z[:, 0:1]
    mf = z[:, HY_EMB:HY_EMB + 1]
    mb = z[:, HY_EMB + 1:HY_EMB + 2]
    h = jnp.sin(freq_ref[0:1, :] * (jnp.dot(z, w1_ref[...], precision=hp, preferred_element_type=f32) + b1_ref[...]))
    h = jnp.sin(freq_ref[1:2, :] * (jnp.dot(h, w2_ref[...], precision=hp, preferred_element_type=f32) + b2_ref[...]))
    for blk in range(2 * HY_ORDER):
        cols = slice(blk * C, (blk + 1) * C)
        taps = jnp.dot(h, w3_ref[:, cols], precision=hp, preferred_element_type=f32)
        taps = (mb if blk % 2 else mf) * taps * jnp.exp(-t * jnp.abs(dec_ref[:, cols]))
        if blk % 2:
            o_ref[blk // 2] = o_ref[blk // 2] + taps
        else:
            o_ref[blk // 2] = taps


def _filters(L, w1, b1, w2, b2, w3, freq, decay, tr):
    H = HY_FILTER_HIDDEN
    C = HY_WIDTH
    W = 2 * HY_ORDER * C
    z = _filter_embedding(L)
    w1p = jnp.zeros((LANES, H), f32).at[:HY_EMB].set(w1)
    full = lambda shape: pl.BlockSpec(shape, lambda r: (0,) * len(shape))
    return pl.pallas_call(
        _filter_kernel,
        grid=(2 * L // tr,),
        in_specs=[pl.BlockSpec((tr, LANES), lambda r: (r, 0)),
                  full((LANES, H)), full((1, H)), full((H, H)), full((1, H)), full((2, H)),
                  full((H, W)), full((1, W))],
        out_specs=pl.BlockSpec((HY_ORDER, tr, C), lambda r: (0, r, 0)),
        out_shape=jax.ShapeDtypeStruct((HY_ORDER, 2 * L, C), f32),
        compiler_params=_params(("parallel",)),
    )(z, w1p, b1.reshape(1, H), w2, b2.reshape(1, H), freq, w3, decay.reshape(1, W))


def _fft_dims(L):
    n2 = SUBLANES
    while (2 * n2) * (2 * n2) <= L:
        n2 *= 2
    return n2, L // n2, 2 * (L // n2), n2 + SUBLANES


def _stack(fr, fi):
    return np.block([[fr, -fi], [fi, fr]])


@functools.lru_cache(maxsize=None)
def _dft_tables(L):
    N2, N1h, N1, _ = _fft_dims(L)
    N = 2 * L
    k1 = np.arange(N1)
    a1 = -2.0 * np.pi * np.outer(k1, np.arange(N1)) / N1
    f1 = _stack(np.cos(a1[:, :N1h]), np.sin(a1[:, :N1h]))
    f1k = np.concatenate([np.cos(a1), np.sin(a1)], axis=0)
    f3 = _stack(np.cos(-a1[:, :N1h].T), np.sin(-a1[:, :N1h].T))
    n2 = np.arange(N2)
    a2 = -2.0 * np.pi * (np.outer(n2, n2)[None] / N2 + (k1[:, None, None] * n2[None, None, :]) / N)
    f2 = np.stack([_stack(np.cos(a), np.sin(a)) for a in a2])
    a2t = -np.transpose(a2, (0, 2, 1))
    g2 = np.stack([_stack(np.cos(a), np.sin(a)) for a in a2t])
    return tuple(jnp.asarray(m, bf16) for m in (f1, f1k, f2, g2, f3))


def _load_cols(ref, lead, rows):
    return jnp.concatenate([ref[lead + (t, rows, slice(None))] for t in range(ref.shape[-3])], axis=1)


def _store_cols(ref, lead, rows, val):
    for t in range(ref.shape[-3]):
        ref[lead + (t, rows, slice(None))] = val[:, t * LANES:(t + 1) * LANES]


def _grouped_loop(n, group, load, compute, store):
    def body(i, c):
        idx = [i * group + u for u in range(group)]
        vals = [load(ix) for ix in idx]
        outs = [compute(ix, v) for ix, v in zip(idx, vals)]
        for ix, o in zip(idx, outs):
            store(ix, o)
        return c

    lax.fori_loop(0, n // group, body, 0)


def _load_ri(ref, rows):
    return jnp.concatenate([_load_cols(ref, (ri,), rows) for ri in range(2)], axis=0).astype(bf16)


def _store_ri(ref, rows, val):
    h = val.shape[0] // 2
    for ri in range(2):
        _store_cols(ref, (ri,), rows, val[ri * h:(ri + 1) * h])


def _spectrum_kernel(k_ref, f1k_ref, f2_ref, o_ref, a_scr, *, L):
    N2, N1h, N1, P = _fft_dims(L)
    T = a_scr.shape[1]
    inv_n = 1.0 / (2 * L)

    def load1(n2):
        return jnp.concatenate([k_ref[0, pl.ds(n2, N1, stride=N2), t * LANES:(t + 1) * LANES]
                                for t in range(T)], axis=1).astype(bf16)

    _grouped_loop(N2, FFT_GROUP, load1,
                  lambda n2, w: jnp.dot(f1k_ref[...], w, preferred_element_type=f32),
                  lambda n2, r: _store_ri(a_scr, pl.ds(n2, N1, stride=P), r))

    def store2(k1, x):
        out_rows = pl.ds(pl.multiple_of(k1 * N2, N2), N2)
        o_ref[0, 0, out_rows, :] = x[:N2].astype(bf16)
        o_ref[0, 1, out_rows, :] = x[N2:].astype(bf16)

    _grouped_loop(N1, FFT_GROUP,
                  lambda k1: _load_ri(a_scr, pl.ds(pl.multiple_of(k1 * P, SUBLANES), N2)),
                  lambda k1, a: jnp.dot(f2_ref[k1], a, preferred_element_type=f32) * inv_n,
                  store2)


def _spectrum(k, tc):
    _, N, C = k.shape
    L = N // 2
    N2, N1h, N1, P = _fft_dims(L)
    _, f1k, f2, _, _ = _dft_tables(L)
    return pl.pallas_call(
        functools.partial(_spectrum_kernel, L=L),
        grid=(HY_ORDER, C // tc),
        in_specs=[pl.BlockSpec((1, N, tc), lambda o, c: (o, 0, c)),
                  pl.BlockSpec(f1k.shape, lambda o, c: (0, 0)),
                  pl.BlockSpec(f2.shape, lambda o, c: (0, 0, 0))],
        out_specs=pl.BlockSpec((1, 2, N, tc), lambda o, c: (o, 0, 0, c)),
        out_shape=jax.ShapeDtypeStruct((HY_ORDER, 2, N, C), bf16),
        scratch_shapes=[pltpu.VMEM((2, tc // LANES, N1 * P, LANES), f32)],
        compiler_params=_params(("parallel", "parallel")),
    )(k, f1k, f2)


def _fftconv_kernel(*refs, L, chunk, has_gate):
    if has_gate:
        u_in, u_ep, g_ref, gate_ref, bias_ref, kf_ref, f1_ref, f2_ref, g2_ref, f3_ref, o_ref, z_scr, a_scr = refs
    else:
        u_in, u_ep, g_ref, bias_ref, kf_ref, f1_ref, f2_ref, g2_ref, f3_ref, o_ref, z_scr, a_scr = refs
    N2, N1h, N1, P = _fft_dims(L)
    J = L // chunk
    nb = chunk // N2
    j = pl.program_id(2)

    @pl.when(j < J)
    def _load():
        for ri in range(2):
            for blk in range(nb):
                rows = pl.ds(pl.multiple_of((j * nb + blk) * P, SUBLANES), N2)
                _store_cols(z_scr, (ri,), rows, u_in[ri, blk * N2:(blk + 1) * N2, :].astype(f32))

    @pl.when(j == J - 1)
    def _transform():
        _grouped_loop(N2, FFT_GROUP,
                      lambda n2: _load_ri(z_scr, pl.ds(n2, N1h, stride=P)),
                      lambda n2, w: jnp.dot(f1_ref[...], w, preferred_element_type=f32),
                      lambda n2, r: _store_ri(a_scr, pl.ds(n2, N1, stride=P), r))

        def block_rows(k1):
            return pl.ds(pl.multiple_of(k1 * P, SUBLANES), N2)

        def middle(k1, a):
            x = jnp.dot(f2_ref[k1], a, preferred_element_type=f32)
            krows = pl.ds(pl.multiple_of(k1 * N2, N2), N2)
            kr = kf_ref[0, 0, krows, :].astype(f32)
            ki = kf_ref[0, 1, krows, :].astype(f32)
            xr, xi = x[:N2], x[N2:]
            y = jnp.concatenate([xr * kr - xi * ki, xr * ki + xi * kr], axis=0).astype(bf16)
            return jnp.dot(g2_ref[k1], y, preferred_element_type=f32)

        _grouped_loop(N1, FFT_GROUP,
                      lambda k1: _load_ri(a_scr, block_rows(k1)),
                      middle,
                      lambda k1, b: _store_ri(a_scr, block_rows(k1), b))

        _grouped_loop(N2, FFT_GROUP,
                      lambda n2: _load_ri(a_scr, pl.ds(n2, N1, stride=P)),
                      lambda n2, b: jnp.dot(f3_ref[...], b, preferred_element_type=f32),
                      lambda n2, y: _store_ri(z_scr, pl.ds(n2, N1h, stride=P), y))

    @pl.when(j >= J)
    def _epilogue():
        for ri in range(2):
            for blk in range(nb):
                rows = pl.ds(pl.multiple_of(((j - J) * nb + blk) * P, SUBLANES), N2)
                y = _load_cols(z_scr, (ri,), rows)
                sl = slice(blk * N2, (blk + 1) * N2)
                u = u_ep[ri, sl, :].astype(f32)
                o = g_ref[ri, sl, :].astype(f32) * (y + bias_ref[...] * u)
                if has_gate:
                    o = o * _silu(gate_ref[ri, sl, :].astype(f32))
                o_ref[ri, sl, :] = o.astype(bf16)


def _fftconv(u, u_cb, g, g_cb, gate, gate_cb, bias, kf, order, tc):
    B, L, _ = u.shape
    C = HY_WIDTH
    N2, N1h, N1, P = _fft_dims(L)
    chunk = min(L, FFT_CHUNK)
    J = L // chunk
    f1, _, f2, g2, f3 = _dft_tables(L)
    T = tc // LANES
    in_map = lambda cb: (lambda c, p, j: (p, jnp.minimum(j, J - 1), cb + c))
    ep_map = lambda cb: (lambda c, p, j: (p, jnp.maximum(j - J, 0), cb + c))
    const = lambda nd: (lambda c, p, j: (0,) * nd)
    once = pl.Buffered(1)
    ops = [u, u, g]
    specs = [pl.BlockSpec((2, chunk, tc), in_map(u_cb)),
             pl.BlockSpec((2, chunk, tc), ep_map(u_cb)),
             pl.BlockSpec((2, chunk, tc), ep_map(g_cb))]
    if gate is not None:
        ops.append(gate)
        specs.append(pl.BlockSpec((2, chunk, tc), ep_map(gate_cb)))
    ops += [bias.reshape(HY_ORDER, 1, C), kf, f1, f2, g2, f3]
    specs += [pl.BlockSpec((None, 1, tc), lambda c, p, j: (order, 0, c)),
              pl.BlockSpec((1, 2, 2 * L, tc), lambda c, p, j: (order, 0, 0, c), pipeline_mode=once),
              pl.BlockSpec(f1.shape, const(2), pipeline_mode=once),
              pl.BlockSpec(f2.shape, const(3), pipeline_mode=once),
              pl.BlockSpec(g2.shape, const(3), pipeline_mode=once),
              pl.BlockSpec(f3.shape, const(2), pipeline_mode=once)]
    return pl.pallas_call(
        functools.partial(_fftconv_kernel, L=L, chunk=chunk, has_gate=gate is not None),
        grid=(C // tc, B // 2, 2 * J),
        in_specs=specs,
        out_specs=pl.BlockSpec((2, chunk, tc), ep_map(0)),
        out_shape=jax.ShapeDtypeStruct((B, L, C), bf16),
        scratch_shapes=[pltpu.VMEM((2, T, N1h * P, LANES), f32),
                        pltpu.VMEM((2, T, N1 * P, LANES), f32)],
        compiler_params=_params(("parallel", "parallel", "arbitrary")),
    )(*ops)


def _merge_kernel(hy_ref, at_ref, gm_ref, mg_ref, x_ref, gate_ref, gpost_ref,
                  why_ref, wat_ref, wgm_ref, wout_ref, o_ref):
    D = x_ref.shape[2]
    acc = None
    for i, (br, w) in enumerate(((hy_ref, why_ref), (at_ref, wat_ref), (gm_ref, wgm_ref))):
        y = jnp.dot(br[0], w[...], preferred_element_type=f32)
        y = jax.nn.sigmoid(mg_ref[0, :, i * D:(i + 1) * D].astype(f32)) * y
        acc = y if acc is None else acc + y
    o = jnp.dot(acc.astype(bf16), wout_ref[...], preferred_element_type=f32)
    r = o * lax.rsqrt(jnp.mean(o * o, axis=-1, keepdims=True) + RMS_EPS) * gpost_ref[...]
    o_ref[0] = x_ref[0] + gate_ref[0] * r


def _merge(hy, at, gm, proj, x, mod, g_post, w_hy, w_at, w_gm, w_out, tm):
    B, L, D = x.shape
    mb = P_MERGE // (3 * D)
    row = lambda b, m: (b, m, 0)
    wspec = pl.BlockSpec((D, D), lambda b, m: (0, 0))
    return pl.pallas_call(
        _merge_kernel,
        grid=(B, L // tm),
        in_specs=[pl.BlockSpec((1, tm, D), row), pl.BlockSpec((1, tm, D), row), pl.BlockSpec((1, tm, D), row),
                  pl.BlockSpec((1, tm, 3 * D), lambda b, m: (b, m, mb)),
                  pl.BlockSpec((1, tm, D), row),
                  pl.BlockSpec((1, 1, D), lambda b, m: (b, 0, 2)),
                  pl.BlockSpec((1, D), lambda b, m: (0, 0)),
                  wspec, wspec, wspec, wspec],
        out_specs=pl.BlockSpec((1, tm, D), row),
        out_shape=jax.ShapeDtypeStruct((B, L, D), f32),
        compiler_params=_params(("parallel", "parallel")),
    )(hy, at, gm, proj, x, mod, g_post.reshape(1, D), w_hy, w_at, w_gm, w_out)


def _rope_tables(L):
    pos = np.arange(L)
    inv = np.power(ROPE_THETA, -np.arange(0, AXIS_DIM, 2) / AXIS_DIM)
    ar = (pos // GRID_W)[:, None] * inv
    ac = (pos % GRID_W)[:, None] * inv
    cos = np.concatenate([np.cos(ar), np.cos(ar), np.cos(ac), np.cos(ac)], axis=1)
    sin = np.concatenate([-np.sin(ar), np.sin(ar), -np.sin(ac), np.sin(ac)], axis=1)
    return jnp.asarray(cos, f32), jnp.asarray(sin, f32)


def _identity_rope(L):
    return jnp.ones((L, HEAD_DIM), f32), jnp.zeros((L, HEAD_DIM), f32)


def _hyena(proj, lp):
    B, L, _ = proj.shape
    tc = FFT_COLS
    nct = HY_WIDTH // tc
    hyc = _shortconv(proj, lp['hy_conv_w'], lp['hy_conv_b'], tc)
    k = _filters(L, lp['hy_w1'], lp['hy_b1'], lp['hy_w2'], lp['hy_b2'], lp['hy_w3'], lp['hy_freq'],
                 lp['hy_decay'], min(2 * L, 256))
    kf = _spectrum(k, LANES)
    z1 = _fftconv(hyc, 2 * nct, hyc, 0, None, 0, lp['hy_bias'], kf, 0, tc)
    return _fftconv(z1, 0, hyc, nct, proj, P_HY_GATE // tc, lp['hy_bias'], kf, 1, tc)


def _values_with_ones(v):
    B, Lk, _ = v.shape
    v4 = v.reshape(B, Lk, ATT_KV_HEADS, HEAD_DIM)
    return jnp.concatenate([v4, jnp.ones_like(v4)], axis=3).reshape(B, Lk, 2 * KV_WIDTH)


def _mixer(proj, k_all, v_all, rope_q, lp, tq, tm_gm):
    att = _attention(proj, k_all, _values_with_ones(v_all), rope_q[0], rope_q[1], lp['q_gain'], tq)
    hy = _hyena(proj, lp)
    gm = _gmlp(proj, lp['gm_ln_g'], lp['gm_ln_b'], lp['gm_ws'], lp['gm_bs_full'], tm_gm)
    return hy, att, gm


def _layer(x, xc, mod_x, mod_c, rope, lp, ctx_out):
    B, L, D = x.shape
    C = xc.shape[1]
    w_in = lp['w_in']
    proj = _inproj(x, mod_x, lp['g_pre'], w_in, min(L, 1024), INPROJ_COLS)
    wc = w_in if ctx_out else w_in[:, P_K:]
    projc = _inproj(xc, mod_c, lp['g_pre'], wc, C, min(wc.shape[1], INPROJ_COLS))
    kv_off = P_K if ctx_out else 0
    ident = _identity_rope(C)
    kc = _kprep(projc, kv_off, ident[0], ident[1], lp['k_gain'], C)
    vc = projc[:, :, kv_off + KV_WIDTH:kv_off + 2 * KV_WIDTH]
    k = _kprep(proj, P_K, rope[0], rope[1], lp['k_gain'], min(L, 1024))
    k_all = jnp.concatenate([kc, k], axis=1)
    v_all = jnp.concatenate([vc, proj[:, :, P_V:P_V + KV_WIDTH]], axis=1)

    hy, att, gm = _mixer(proj, k_all, v_all, rope, lp, 256, 256)
    x_new = _merge(hy, att, gm, proj, x, mod_x, lp['g_post'], lp['w_hy_o'], lp['w_att_o'], lp['w_gm_o'],
                   lp['w_out'], min(L, 512))
    if not ctx_out:
        return x_new, xc
    hyc, attc, gmc = _mixer(projc, kc, vc, ident, lp, C, min(C, 256))
    xc_new = _merge(hyc, attc, gmc, projc, xc, mod_c, lp['g_post'], lp['w_hy_o'], lp['w_att_o'], lp['w_gm_o'],
                    lp['w_out'], C)
    return x_new, xc_new


def kernel(x, c, ctx, c_ctx, w_mod, b_mod, g_pre, g_post, w_in, hy_conv_w, hy_conv_b, hy_w1, hy_b1, hy_w2, hy_b2, hy_w3, hy_freq, hy_decay, hy_bias, q_gain, k_gain, gm_ln_g, gm_ln_b, gm_ws, gm_bs, w_hy_o, w_att_o, w_gm_o, w_out):
    B, L, D = x.shape
    rope = _rope_tables(L)

    R = -(-(B + 1) // (2 * SUBLANES)) * (2 * SUBLANES)
    cc = jnp.zeros((R, D), f32).at[:B].set(c).at[B].set(c_ctx)
    mod = _modulation(cc, w_mod, b_mod)

    w_in_p = jnp.concatenate([w_in[:, :, a:b].astype(bf16) for a, b in _COL_RANGES], axis=2)
    gm_bs_full = jnp.repeat(jnp.swapaxes(gm_bs, 1, 2), GM_GROUP_DIM, axis=2)

    xc = ctx
    for i in range(DEPTH):
        lp = {
            'g_pre': g_pre[i], 'g_post': g_post[i], 'w_in': w_in_p[i],
            'hy_conv_w': hy_conv_w[i], 'hy_conv_b': hy_conv_b[i],
            'hy_w1': hy_w1[i], 'hy_b1': hy_b1[i], 'hy_w2': hy_w2[i], 'hy_b2': hy_b2[i],
            'hy_w3': hy_w3[i], 'hy_freq': hy_freq[i], 'hy_decay': hy_decay[i], 'hy_bias': hy_bias[i],
            'q_gain': q_gain[i], 'k_gain': k_gain[i], 'gm_ln_g': gm_ln_g[i], 'gm_ln_b': gm_ln_b[i],
            'gm_ws': gm_ws[i].astype(bf16), 'gm_bs_full': gm_bs_full[i],
            'w_hy_o': w_hy_o[i].astype(bf16), 'w_att_o': w_att_o[i].astype(bf16),
            'w_gm_o': w_gm_o[i].astype(bf16), 'w_out': w_out[i].astype(bf16),
        }
        mod_x = mod[i, :B].reshape(B, 1, 3 * D)
        mod_c = jnp.broadcast_to(mod[i, B].reshape(1, 1, 3 * D), (B, 1, 3 * D))
        x, xc = _layer(x, xc, mod_x, mod_c, rope, lp, i < DEPTH - 1)
    return x
```

```python
import functools
import math

import numpy as np
import jax
import jax.numpy as jnp
from jax import lax
from jax.experimental import pallas as pl
from jax.experimental.pallas import tpu as pltpu

f32 = jnp.float32
bf16 = jnp.bfloat16

D_MODEL = 1024
DEPTH = 2
GRID_W = 64
RMS_EPS = 1e-6
LN_EPS = 1e-5

HY_WIDTH = D_MODEL
HY_ORDER = 2
HY_BANDS = 16
HY_EMB = 1 + 2 * HY_BANDS
HY_FILTER_HIDDEN = 64

HEAD_DIM = 128
ATT_HEADS = D_MODEL // HEAD_DIM
ATT_KV_HEADS = 2
ATT_GROUP = ATT_HEADS // ATT_KV_HEADS
ATT_WIDTH = ATT_HEADS * HEAD_DIM
KV_WIDTH = ATT_KV_HEADS * HEAD_DIM
AXIS_DIM = HEAD_DIM // 2
ROPE_THETA = 10000.0

GM_WIDTH = D_MODEL
GM_GROUPS = 8
GM_GROUP_DIM = GM_WIDTH // GM_GROUPS
CHUNK = 128

OFF_HY = 0
OFF_HY_GATE = OFF_HY + (HY_ORDER + 1) * HY_WIDTH
OFF_Q = OFF_HY_GATE + HY_WIDTH
OFF_K = OFF_Q + ATT_WIDTH
OFF_V = OFF_K + KV_WIDTH
OFF_ATT_GATE = OFF_V + KV_WIDTH
OFF_GM = OFF_ATT_GATE + ATT_WIDTH
OFF_GM_GATE = OFF_GM + 2 * GM_WIDTH
OFF_MERGE = OFF_GM_GATE + GM_WIDTH
IN_WIDTH = OFF_MERGE + 3 * D_MODEL

P_HY = 0
P_HY_GATE = 3 * D_MODEL
P_Q = 4 * D_MODEL
P_ATT_GATE = 5 * D_MODEL
P_GM = 6 * D_MODEL
P_GM_GATE = 8 * D_MODEL
P_MERGE = 9 * D_MODEL
P_K = 12 * D_MODEL
P_V = P_K + KV_WIDTH
_COL_RANGES = ((OFF_HY, OFF_K), (OFF_ATT_GATE, IN_WIDTH), (OFF_K, OFF_ATT_GATE))

LANES = 128
SUBLANES = 8
FFT_COLS = 256
FFT_GROUP = 16
INPROJ_COLS = 2560
FFT_CHUNK = 1024
VMEM_LIMIT = 52 * 1024 * 1024
FFT_VMEM_LIMIT = 58 * 1024 * 1024

_SM_SCALE_LOG2E = (HEAD_DIM ** -0.5) * math.log2(math.e)


def _params(sem, vmem=VMEM_LIMIT):
    return pltpu.CompilerParams(dimension_semantics=sem, vmem_limit_bytes=vmem)


def _silu(x):
    return x * jax.nn.sigmoid(x)


def _mod_kernel(cc_ref, w_ref, b_ref, o_ref):
    s = _silu(cc_ref[...]).astype(bf16)
    o_ref[0] = jnp.dot(s, w_ref[0].astype(bf16), preferred_element_type=f32) + b_ref[0]


def _modulation(cc, w_mod, b_mod):
    R, D = cc.shape
    tn = D
    return pl.pallas_call(
        _mod_kernel,
        grid=(DEPTH, 3 * D // tn),
        in_specs=[pl.BlockSpec((R, D), lambda i, n: (0, 0)),
                  pl.BlockSpec((1, D, tn), lambda i, n: (i, 0, n)),
                  pl.BlockSpec((1, 1, tn), lambda i, n: (i, 0, n))],
        out_specs=pl.BlockSpec((1, R, tn), lambda i, n: (i, 0, n)),
        out_shape=jax.ShapeDtypeStruct((DEPTH, R, 3 * D), f32),
        compiler_params=_params(("parallel", "parallel")),
    )(cc, w_mod, b_mod.reshape(DEPTH, 1, 3 * D))


def _inproj_kernel(x_ref, mod_ref, g_ref, w_ref, o_ref, h_scr):
    D = x_ref.shape[2]

    @pl.when(pl.program_id(2) == 0)
    def _():
        x = x_ref[0]
        r = lax.rsqrt(jnp.mean(x * x, axis=-1, keepdims=True) + RMS_EPS)
        shift = mod_ref[0, :, 0:D]
        scale = mod_ref[0, :, D:2 * D]
        h_scr[...] = ((x * r) * g_ref[...] * (1.0 + scale) + shift).astype(bf16)

    o_ref[0] = jnp.dot(h_scr[...], w_ref[...], preferred_element_type=f32).astype(o_ref.dtype)


def _inproj(x, mod, g_pre, w, tm, tn):
    B, L, D = x.shape
    N = w.shape[1]
    return pl.pallas_call(
        _inproj_kernel,
        grid=(B, L // tm, N // tn),
        in_specs=[pl.BlockSpec((1, tm, D), lambda b, m, n: (b, m, 0)),
                  pl.BlockSpec((1, 1, 3 * D), lambda b, m, n: (b, 0, 0)),
                  pl.BlockSpec((1, D), lambda b, m, n: (0, 0)),
                  pl.BlockSpec((D, tn), lambda b, m, n: (0, n))],
        out_specs=pl.BlockSpec((1, tm, tn), lambda b, m, n: (b, m, n)),
        out_shape=jax.ShapeDtypeStruct((B, L, N), bf16),
        scratch_shapes=[pltpu.VMEM((tm, D), bf16)],
        compiler_params=_params(("parallel", "parallel", "arbitrary")),
    )(x, mod, g_pre.reshape(1, D), w)


def _rope(x, cos, sin):
    lane = lax.broadcasted_iota(jnp.int32, x.shape, 1)
    first = (lane & (AXIS_DIM // 2)) == 0
    rot = jnp.where(first, pltpu.roll(x, HEAD_DIM - AXIS_DIM // 2, 1), pltpu.roll(x, AXIS_DIM // 2, 1))
    return x * cos + rot * sin


def _head_norm(x, gain):
    r = lax.rsqrt(jnp.mean(x * x, axis=-1, keepdims=True) + RMS_EPS)
    return x * r * gain


def _kvprep_kernel(c_ref, p_ref, cos_ref, sin_ref, gain_ref, k_ref, v1_ref):
    kv = jnp.where(pl.program_id(1) == 0, c_ref[0], p_ref[0])
    for h in range(ATT_KV_HEADS):
        sl = slice(h * HEAD_DIM, (h + 1) * HEAD_DIM)
        kn = _head_norm(kv[:, sl].astype(f32), gain_ref[...])
        k_ref[0, :, sl] = _rope(kn, cos_ref[...], sin_ref[...]).astype(bf16)
        v = kv[:, KV_WIDTH + h * HEAD_DIM:KV_WIDTH + (h + 1) * HEAD_DIM]
        v1_ref[0, :, 2 * h * HEAD_DIM:(2 * h + 1) * HEAD_DIM] = v
        v1_ref[0, :, (2 * h + 1) * HEAD_DIM:(2 * h + 2) * HEAD_DIM] = jnp.ones_like(v)


def _kvprep(projc, c_off, proj, cos, sin, gain):
    B, C, _ = projc.shape
    L = proj.shape[1]
    w = 2 * KV_WIDTH
    cb, pb = c_off // w, P_K // w
    return pl.pallas_call(
        _kvprep_kernel,
        grid=(B, (C + L) // C),
        in_specs=[pl.BlockSpec((1, C, w), lambda b, m: (b, 0, cb)),
                  pl.BlockSpec((1, C, w), lambda b, m: (b, jnp.maximum(m - 1, 0), pb)),
                  pl.BlockSpec((C, HEAD_DIM), lambda b, m: (m, 0)),
                  pl.BlockSpec((C, HEAD_DIM), lambda b, m: (m, 0)),
                  pl.BlockSpec((1, HEAD_DIM), lambda b, m: (0, 0))],
        out_specs=[pl.BlockSpec((1, C, KV_WIDTH), lambda b, m: (b, m, 0)),
                   pl.BlockSpec((1, C, w), lambda b, m: (b, m, 0))],
        out_shape=[jax.ShapeDtypeStruct((B, C + L, KV_WIDTH), bf16),
                   jax.ShapeDtypeStruct((B, C + L, w), bf16)],
        compiler_params=_params(("parallel", "parallel")),
    )(projc, proj, cos, sin, gain.reshape(1, HEAD_DIM))


def _attn_kernel(q_ref, k_ref, v1_ref, cos_ref, sin_ref, gain_ref, gate_ref, o_ref):
    k = k_ref[0]
    v1 = v1_ref[0]
    def scores(h):
        sl = slice(h * HEAD_DIM, (h + 1) * HEAD_DIM)
        qn = _head_norm(q_ref[0, :, sl].astype(f32), gain_ref[...])
        q = (_rope(qn, cos_ref[...], sin_ref[...]) * _SM_SCALE_LOG2E).astype(bf16)
        return lax.dot_general(q, k, (((1,), (1,)), ((), ())), preferred_element_type=f32)

    s_next = scores(0)
    for h in range(ATT_GROUP):
        sl = slice(h * HEAD_DIM, (h + 1) * HEAD_DIM)
        s = s_next
        if h + 1 < ATT_GROUP:
            s_next = scores(h + 1)
        p = jnp.exp2(s - jnp.max(s, axis=-1, keepdims=True))
        ol = jnp.dot(p.astype(bf16), v1, preferred_element_type=f32)
        o = ol[:, :HEAD_DIM] / ol[:, HEAD_DIM:HEAD_DIM + 1]
        o_ref[0, :, sl] = (o * _silu(gate_ref[0, :, sl].astype(f32))).astype(bf16)


def _attention(proj, k_all, vt_all, Lk, cos, sin, gain, tq):
    B, L, _ = proj.shape
    gw = ATT_GROUP * HEAD_DIM
    qb, gb = P_Q // gw, P_ATT_GATE // gw
    return pl.pallas_call(
        _attn_kernel,
        grid=(B, ATT_KV_HEADS, L // tq),
        in_specs=[pl.BlockSpec((1, tq, gw), lambda b, h, i: (b, i, qb + h)),
                  pl.BlockSpec((1, Lk, HEAD_DIM), lambda b, h, i: (b, 0, h)),
                  pl.BlockSpec((1, Lk, 2 * HEAD_DIM), lambda b, h, i: (b, 0, h)),
                  pl.BlockSpec((tq, HEAD_DIM), lambda b, h, i: (i, 0)),
                  pl.BlockSpec((tq, HEAD_DIM), lambda b, h, i: (i, 0)),
                  pl.BlockSpec((1, HEAD_DIM), lambda b, h, i: (0, 0)),
                  pl.BlockSpec((1, tq, gw), lambda b, h, i: (b, i, gb + h))],
        out_specs=pl.BlockSpec((1, tq, gw), lambda b, h, i: (b, i, h)),
        out_shape=jax.ShapeDtypeStruct((B, L, ATT_WIDTH), bf16),
        compiler_params=_params(("parallel", "parallel", "parallel")),
    )(proj, k_all, vt_all, cos, sin, gain.reshape(1, HEAD_DIM), proj)


def _gmlp_kernel(p_ref, gate_ref, lng_ref, lnb_ref, ws_ref, bs_ref, o_ref):
    tm = p_ref.shape[1]
    p = p_ref[0].astype(f32)
    uv = 0.5 * p * (1.0 + lax.erf(p * (2.0 ** -0.5)))
    u = uv[:, :GM_WIDTH]
    v = uv[:, GM_WIDTH:]
    vc = v - jnp.mean(v, axis=-1, keepdims=True)
    var = jnp.mean(vc * vc, axis=-1, keepdims=True)
    vn = (vc * lax.rsqrt(var + LN_EPS) * lng_ref[...] + lnb_ref[...]).astype(bf16)
    ug = u * _silu(gate_ref[0].astype(f32))
    for n in range(tm // CHUNK):
        rows = slice(n * CHUNK, (n + 1) * CHUNK)
        for g in range(GM_GROUPS):
            cols = slice(g * GM_GROUP_DIM, (g + 1) * GM_GROUP_DIM)
            y = jnp.dot(ws_ref[g], vn[rows, cols], preferred_element_type=f32) + bs_ref[:, cols]
            o_ref[0, rows, cols] = (ug[rows, cols] * y).astype(bf16)


def _gmlp(proj, ln_g, ln_b, ws, bs_full, tm):
    B, L, _ = proj.shape
    pb, gb = P_GM // (2 * GM_WIDTH), P_GM_GATE // GM_WIDTH
    return pl.pallas_call(
        _gmlp_kernel,
        grid=(B, L // tm),
        in_specs=[pl.BlockSpec((1, tm, 2 * GM_WIDTH), lambda b, m: (b, m, pb)),
                  pl.BlockSpec((1, tm, GM_WIDTH), lambda b, m: (b, m, gb)),
                  pl.BlockSpec((1, GM_WIDTH), lambda b, m: (0, 0)),
                  pl.BlockSpec((1, GM_WIDTH), lambda b, m: (0, 0)),
                  pl.BlockSpec((GM_GROUPS, CHUNK, CHUNK), lambda b, m: (0, 0, 0)),
                  pl.BlockSpec((CHUNK, GM_WIDTH), lambda b, m: (0, 0))],
        out_specs=pl.BlockSpec((1, tm, GM_WIDTH), lambda b, m: (b, m, 0)),
        out_shape=jax.ShapeDtypeStruct((B, L, GM_WIDTH), bf16),
        compiler_params=_params(("parallel", "parallel")),
    )(proj, proj, ln_g.reshape(1, GM_WIDTH), ln_b.reshape(1, GM_WIDTH), ws, bs_full)


def _shortconv_kernel(p_ref, w_ref, b_ref, o_ref):
    L = p_ref.shape[1]
    x = p_ref[0].astype(f32)
    row = lax.broadcasted_iota(jnp.int32, x.shape, 0)
    xm = jnp.where(row == 0, 0.0, pltpu.roll(x, 1, 0))
    xp = jnp.where(row == L - 1, 0.0, pltpu.roll(x, L - 1, 0))
    o_ref[0] = (xm * w_ref[0:1, :] + x * w_ref[1:2, :] + xp * w_ref[2:3, :] + b_ref[...]).astype(bf16)


def _shortconv(proj, w, b, tc):
    B, L, _ = proj.shape
    W = w.shape[1]
    return pl.pallas_call(
        _shortconv_kernel,
        grid=(B, W // tc),
        in_specs=[pl.BlockSpec((1, L, tc), lambda b, c: (b, 0, c)),
                  pl.BlockSpec((3, tc), lambda b, c: (0, c)),
                  pl.BlockSpec((1, tc), lambda b, c: (0, c))],
        out_specs=pl.BlockSpec((1, L, tc), lambda b, c: (b, 0, c)),
        out_shape=jax.ShapeDtypeStruct((B, L, W), bf16),
        compiler_params=_params(("parallel", "parallel")),
    )(proj, w, b.reshape(1, W))


def _filter_embedding(L):
    n = np.arange(2 * L)
    pos = np.where(n < L, n, 2 * L - n)
    pos = np.where(n == L, 0, pos)
    t = np.linspace(0.0, 1.0, L)[pos]
    bands = np.linspace(1e-4, HY_BANDS - 1, HY_BANDS)
    ang = (2.0 * math.pi / L) * pos[:, None] * bands[None, :]
    z = np.zeros((2 * L, LANES), np.float64)
    z[:, 0] = t
    z[:, 1:1 + HY_BANDS] = np.cos(ang)
    z[:, 1 + HY_BANDS:HY_EMB] = -np.sin(ang)
    z[:, HY_EMB] = (n < L)
    z[:, HY_EMB + 1] = (n > L) | (n == 0)
    return jnp.asarray(z, f32)


def _filter_kernel(z_ref, w1_ref, b1_ref, w2_ref, b2_ref, freq_ref, w3_ref, dec_ref, o_ref):
    hp = lax.Precision.HIGHEST
    C = o_ref.shape[2]
    z = z_ref[...]
    t = z[:, 0:1]
    mf = z[:, HY_EMB:HY_EMB + 1]
    mb = z[:, HY_EMB + 1:HY_EMB + 2]
    h = jnp.sin(freq_ref[0:1, :] * (jnp.dot(z, w1_ref[...], precision=hp, preferred_element_type=f32) + b1_ref[...]))
    h = jnp.sin(freq_ref[1:2, :] * (jnp.dot(h, w2_ref[...], precision=hp, preferred_element_type=f32) + b2_ref[...]))
    for blk in range(2 * HY_ORDER):
        cols = slice(blk * C, (blk + 1) * C)
        taps = jnp.dot(h, w3_ref[:, cols], precision=hp, preferred_element_type=f32)
        taps = (mb if blk % 2 else mf) * taps * jnp.exp(-t * jnp.abs(dec_ref[:, cols]))
        if blk % 2:
            o_ref[blk // 2] = o_ref[blk // 2] + taps
        else:
            o_ref[blk // 2] = taps


def _filters(L, w1, b1, w2, b2, w3, freq, decay, tr):
    H = HY_FILTER_HIDDEN
    C = HY_WIDTH
    W = 2 * HY_ORDER * C
    z = _filter_embedding(L)
    w1p = jnp.zeros((LANES, H), f32).at[:HY_EMB].set(w1)
    full = lambda shape: pl.BlockSpec(shape, lambda r: (0,) * len(shape))
    return pl.pallas_call(
        _filter_kernel,
        grid=(2 * L // tr,),
        in_specs=[pl.BlockSpec((tr, LANES), lambda r: (r, 0)),
                  full((LANES, H)), full((1, H)), full((H, H)), full((1, H)), full((2, H)),
                  full((H, W)), full((1, W))],
        out_specs=pl.BlockSpec((HY_ORDER, tr, C), lambda r: (0, r, 0)),
        out_shape=jax.ShapeDtypeStruct((HY_ORDER, 2 * L, C), f32),
        compiler_params=_params(("parallel",)),
    )(z, w1p, b1.reshape(1, H), w2, b2.reshape(1, H), freq, w3, decay.reshape(1, W))


def _fft_dims(L):
    n2 = SUBLANES
    while (2 * n2) * (2 * n2) <= L:
        n2 *= 2
    return n2, L // n2, 2 * (L // n2), n2 + SUBLANES


def _stack(fr, fi):
    return np.block([[fr, -fi], [fi, fr]])


@functools.lru_cache(maxsize=None)
def _dft_tables(L):
    N2, N1h, N1, _ = _fft_dims(L)
    N = 2 * L
    k1 = np.arange(N1)
    a1 = -2.0 * np.pi * np.outer(k1, np.arange(N1)) / N1
    f1 = _stack(np.cos(a1[:, :N1h]), np.sin(a1[:, :N1h]))
    f1k = np.concatenate([np.cos(a1), np.sin(a1)], axis=0)
    f3 = _stack(np.cos(-a1[:, :N1h].T), np.sin(-a1[:, :N1h].T))
    n2 = np.arange(N2)
    a2 = -2.0 * np.pi * (np.outer(n2, n2)[None] / N2 + (k1[:, None, None] * n2[None, None, :]) / N)
    f2 = np.stack([_stack(np.cos(a), np.sin(a)) for a in a2])
    a2t = -np.transpose(a2, (0, 2, 1))
    g2 = np.stack([_stack(np.cos(a), np.sin(a)) for a in a2t])
    return tuple(jnp.asarray(m, bf16) for m in (f1, f1k, f2, g2, f3))


def _load_cols(ref, lead, rows):
    return jnp.concatenate([ref[lead + (t, rows, slice(None))] for t in range(ref.shape[-3])], axis=1)


def _store_cols(ref, lead, rows, val):
    for t in range(ref.shape[-3]):
        ref[lead + (t, rows, slice(None))] = val[:, t * LANES:(t + 1) * LANES]


def _grouped_loop(n, group, load, compute, store):
    def body(i, c):
        idx = [i * group + u for u in range(group)]
        vals = [load(ix) for ix in idx]
        outs = [compute(ix, v) for ix, v in zip(idx, vals)]
        for ix, o in zip(idx, outs):
            store(ix, o)
        return c

    lax.fori_loop(0, n // group, body, 0)


def _load_ri(ref, rows):
    return jnp.concatenate([_load_cols(ref, (ri,), rows) for ri in range(2)], axis=0).astype(bf16)


def _store_ri(ref, rows, val):
    h = val.shape[0] // 2
    for ri in range(2):
        _store_cols(ref, (ri,), rows, val[ri * h:(ri + 1) * h])


def _spectrum_kernel(k_ref, f1k_ref, f2_ref, o_ref, a_scr, *, L):
    N2, N1h, N1, P = _fft_dims(L)
    T = a_scr.shape[1]
    inv_n = 1.0 / (2 * L)

    def load1(n2):
        return jnp.concatenate([k_ref[0, pl.ds(n2, N1, stride=N2), t * LANES:(t + 1) * LANES]
                                for t in range(T)], axis=1).astype(bf16)

    _grouped_loop(N2, FFT_GROUP, load1,
                  lambda n2, w: jnp.dot(f1k_ref[...], w, preferred_element_type=f32),
                  lambda n2, r: _store_ri(a_scr, pl.ds(n2, N1, stride=P), r))

    def store2(k1, x):
        out_rows = pl.ds(pl.multiple_of(k1 * N2, N2), N2)
        o_ref[0, 0, out_rows, :] = x[:N2].astype(bf16)
        o_ref[0, 1, out_rows, :] = x[N2:].astype(bf16)

    _grouped_loop(N1, FFT_GROUP,
                  lambda k1: _load_ri(a_scr, pl.ds(pl.multiple_of(k1 * P, SUBLANES), N2)),
                  lambda k1, a: jnp.dot(f2_ref[k1], a, preferred_element_type=f32) * inv_n,
                  store2)


def _spectrum(k, tc):
    _, N, C = k.shape
    L = N // 2
    N2, N1h, N1, P = _fft_dims(L)
    _, f1k, f2, _, _ = _dft_tables(L)
    return pl.pallas_call(
        functools.partial(_spectrum_kernel, L=L),
        grid=(HY_ORDER, C // tc),
        in_specs=[pl.BlockSpec((1, N, tc), lambda o, c: (o, 0, c)),
                  pl.BlockSpec(f1k.shape, lambda o, c: (0, 0)),
                  pl.BlockSpec(f2.shape, lambda o, c: (0, 0, 0))],
        out_specs=pl.BlockSpec((1, 2, N, tc), lambda o, c: (o, 0, 0, c)),
        out_shape=jax.ShapeDtypeStruct((HY_ORDER, 2, N, C), bf16),
        scratch_shapes=[pltpu.VMEM((2, tc // LANES, N1 * P, LANES), f32)],
        compiler_params=_params(("parallel", "parallel")),
    )(k, f1k, f2)


def _fftconv_kernel(*refs, L, chunk, has_gate):
    if has_gate:
        u_in, u_ep, g_ref, gate_ref, bias_ref, kf_ref, f1_ref, f2_ref, g2_ref, f3_ref, o_ref, z_scr, a_scr = refs
    else:
        u_in, u_ep, g_ref, bias_ref, kf_ref, f1_ref, f2_ref, g2_ref, f3_ref, o_ref, z_scr, a_scr = refs
    N2, N1h, N1, P = _fft_dims(L)
    J = L // chunk
    nb = chunk // N2
    j = pl.program_id(2)

    @pl.when(j < J)
    def _load():
        for ri in range(2):
            for blk in range(nb):
                rows = pl.ds(pl.multiple_of((j * nb + blk) * P, SUBLANES), N2)
                _store_cols(z_scr, (ri,), rows, u_in[ri, blk * N2:(blk + 1) * N2, :].astype(f32))

    @pl.when(j == J - 1)
    def _transform():
        _grouped_loop(N2, FFT_GROUP,
                      lambda n2: _load_ri(z_scr, pl.ds(n2, N1h, stride=P)),
                      lambda n2, w: jnp.dot(f1_ref[...], w, preferred_element_type=f32),
                      lambda n2, r: _store_ri(a_scr, pl.ds(n2, N1, stride=P), r))

        def block_rows(k1):
            return pl.ds(pl.multiple_of(k1 * P, SUBLANES), N2)

        def middle(k1, a):
            x = jnp.dot(f2_ref[k1], a, preferred_element_type=f32)
            krows = pl.ds(pl.multiple_of(k1 * N2, N2), N2)
            kr = kf_ref[0, 0, krows, :].astype(f32)
            ki = kf_ref[0, 1, krows, :].astype(f32)
            xr, xi = x[:N2], x[N2:]
            y = jnp.concatenate([xr * kr - xi * ki, xr * ki + xi * kr], axis=0).astype(bf16)
            return jnp.dot(g2_ref[k1], y, preferred_element_type=f32)

        _grouped_loop(N1, FFT_GROUP,
                      lambda k1: _load_ri(a_scr, block_rows(k1)),
                      middle,
                      lambda k1, b: _store_ri(a_scr, block_rows(k1), b))

        _grouped_loop(N2, FFT_GROUP,
                      lambda n2: _load_ri(a_scr, pl.ds(n2, N1, stride=P)),
                      lambda n2, b: jnp.dot(f3_ref[...], b, preferred_element_type=f32),
                      lambda n2, y: _store_ri(z_scr, pl.ds(n2, N1h, stride=P), y))

    @pl.when(j >= J)
    def _epilogue():
        for ri in range(2):
            for blk in range(nb):
                rows = pl.ds(pl.multiple_of(((j - J) * nb + blk) * P, SUBLANES), N2)
                y = _load_cols(z_scr, (ri,), rows)
                sl = slice(blk * N2, (blk + 1) * N2)
                u = u_ep[ri, sl, :].astype(f32)
                o = g_ref[ri, sl, :].astype(f32) * (y + bias_ref[...] * u)
                if has_gate:
                    o = o * _silu(gate_ref[ri, sl, :].astype(f32))
                o_ref[ri, sl, :] = o.astype(bf16)


def _fftconv(u, u_cb, g, g_cb, gate, gate_cb, bias, kf, order, tc):
    B, L, _ = u.shape
    C = HY_WIDTH
    N2, N1h, N1, P = _fft_dims(L)
    chunk = min(L, FFT_CHUNK)
    J = L // chunk
    f1, _, f2, g2, f3 = _dft_tables(L)
    T = tc // LANES
    in_map = lambda cb: (lambda c, p, j: (p, jnp.minimum(j, J - 1), cb + c))
    ep_map = lambda cb: (lambda c, p, j: (p, jnp.maximum(j - J, 0), cb + c))
    const = lambda nd: (lambda c, p, j: (0,) * nd)
    once = pl.Buffered(1)
    ops = [u, u, g]
    specs = [pl.BlockSpec((2, chunk, tc), in_map(u_cb)),
             pl.BlockSpec((2, chunk, tc), ep_map(u_cb)),
             pl.BlockSpec((2, chunk, tc), ep_map(g_cb))]
    if gate is not None:
        ops.append(gate)
        specs.append(pl.BlockSpec((2, chunk, tc), ep_map(gate_cb)))
    ops += [bias.reshape(HY_ORDER, 1, C), kf, f1, f2, g2, f3]
    specs += [pl.BlockSpec((None, 1, tc), lambda c, p, j: (order, 0, c)),
              pl.BlockSpec((1, 2, 2 * L, tc), lambda c, p, j: (order, 0, 0, c), pipeline_mode=once),
              pl.BlockSpec(f1.shape, const(2), pipeline_mode=once),
              pl.BlockSpec(f2.shape, const(3), pipeline_mode=once),
              pl.BlockSpec(g2.shape, const(3), pipeline_mode=once),
              pl.BlockSpec(f3.shape, const(2), pipeline_mode=once)]
    return pl.pallas_call(
        functools.partial(_fftconv_kernel, L=L, chunk=chunk, has_gate=gate is not None),
        grid=(C // tc, B // 2, 2 * J),
        in_specs=specs,
        out_specs=pl.BlockSpec((2, chunk, tc), ep_map(0)),
        out_shape=jax.ShapeDtypeStruct((B, L, C), bf16),
        scratch_shapes=[pltpu.VMEM((2, T, N1h * P, LANES), f32),
                        pltpu.VMEM((2, T, N1 * P, LANES), f32)],
        compiler_params=_params(("parallel", "parallel", "arbitrary"), FFT_VMEM_LIMIT),
    )(*ops)


def _merge_kernel(hy_ref, at_ref, gm_ref, mg_ref, x_ref, gate_ref, gpost_ref,
                  why_ref, wat_ref, wgm_ref, wout_ref, o_ref):
    D = x_ref.shape[2]
    acc = None
    for i, (br, w) in enumerate(((hy_ref, why_ref), (at_ref, wat_ref), (gm_ref, wgm_ref))):
        y = jnp.dot(br[0], w[...], preferred_element_type=f32)
        y = jax.nn.sigmoid(mg_ref[0, :, i * D:(i + 1) * D].astype(f32)) * y
        acc = y if acc is None else acc + y
    o = jnp.dot(acc.astype(bf16), wout_ref[...], preferred_element_type=f32)
    r = o * lax.rsqrt(jnp.mean(o * o, axis=-1, keepdims=True) + RMS_EPS) * gpost_ref[...]
    o_ref[0] = x_ref[0] + gate_ref[0] * r


def _merge(hy, at, gm, proj, x, mod, g_post, w_hy, w_at, w_gm, w_out, tm):
    B, L, D = x.shape
    mb = P_MERGE // (3 * D)
    row = lambda b, m: (b, m, 0)
    wspec = pl.BlockSpec((D, D), lambda b, m: (0, 0))
    return pl.pallas_call(
        _merge_kernel,
        grid=(B, L // tm),
        in_specs=[pl.BlockSpec((1, tm, D), row), pl.BlockSpec((1, tm, D), row), pl.BlockSpec((1, tm, D), row),
                  pl.BlockSpec((1, tm, 3 * D), lambda b, m: (b, m, mb)),
                  pl.BlockSpec((1, tm, D), row),
                  pl.BlockSpec((1, 1, D), lambda b, m: (b, 0, 2)),
                  pl.BlockSpec((1, D), lambda b, m: (0, 0)),
                  wspec, wspec, wspec, wspec],
        out_specs=pl.BlockSpec((1, tm, D), row),
        out_shape=jax.ShapeDtypeStruct((B, L, D), f32),
        compiler_params=_params(("parallel", "parallel")),
    )(hy, at, gm, proj, x, mod, g_post.reshape(1, D), w_hy, w_at, w_gm, w_out)


def _rope_tables(L):
    pos = np.arange(L)
    inv = np.power(ROPE_THETA, -np.arange(0, AXIS_DIM, 2) / AXIS_DIM)
    ar = (pos // GRID_W)[:, None] * inv
    ac = (pos % GRID_W)[:, None] * inv
    cos = np.concatenate([np.cos(ar), np.cos(ar), np.cos(ac), np.cos(ac)], axis=1)
    sin = np.concatenate([-np.sin(ar), np.sin(ar), -np.sin(ac), np.sin(ac)], axis=1)
    return jnp.asarray(cos, f32), jnp.asarray(sin, f32)


def _identity_rope(L):
    return jnp.ones((L, HEAD_DIM), f32), jnp.zeros((L, HEAD_DIM), f32)


def _hyena(proj, lp):
    B, L, _ = proj.shape
    tc = FFT_COLS
    nct = HY_WIDTH // tc
    hyc = _shortconv(proj, lp['hy_conv_w'], lp['hy_conv_b'], tc)
    k = _filters(L, lp['hy_w1'], lp['hy_b1'], lp['hy_w2'], lp['hy_b2'], lp['hy_w3'], lp['hy_freq'],
                 lp['hy_decay'], min(2 * L, 256))
    kf = _spectrum(k, LANES)
    z1 = _fftconv(hyc, 2 * nct, hyc, 0, None, 0, lp['hy_bias'], kf, 0, tc)
    return _fftconv(z1, 0, hyc, nct, proj, P_HY_GATE // tc, lp['hy_bias'], kf, 1, tc)


def _mixer(proj, k_all, v1_all, Lk, rope_q, lp, tq, tm_gm):
    att = _attention(proj, k_all, v1_all, Lk, rope_q[0], rope_q[1], lp['q_gain'], tq)
    hy = _hyena(proj, lp)
    gm = _gmlp(proj, lp['gm_ln_g'], lp['gm_ln_b'], lp['gm_ws'], lp['gm_bs_full'], tm_gm)
    return hy, att, gm


def _layer(x, xc, mod_x, mod_c, rope, lp, ctx_out):
    B, L, D = x.shape
    C = xc.shape[1]
    w_in = lp['w_in']
    proj = _inproj(x, mod_x, lp['g_pre'], w_in, min(L, 1024), INPROJ_COLS)
    wc = w_in if ctx_out else w_in[:, P_K:]
    projc = _inproj(xc, mod_c, lp['g_pre'], wc, C, min(wc.shape[1], INPROJ_COLS))
    ident = _identity_rope(C)
    k_all, v1_all = _kvprep(projc, P_K if ctx_out else 0, proj,
                            jnp.concatenate([ident[0], rope[0]]), jnp.concatenate([ident[1], rope[1]]),
                            lp['k_gain'])

    hy, att, gm = _mixer(proj, k_all, v1_all, C + L, rope, lp, 256, 256)
    x_new = _merge(hy, att, gm, proj, x, mod_x, lp['g_post'], lp['w_hy_o'], lp['w_att_o'], lp['w_gm_o'],
                   lp['w_out'], min(L, 512))
    if not ctx_out:
        return x_new, xc
    hyc, attc, gmc = _mixer(projc, k_all, v1_all, C, ident, lp, C, min(C, 256))
    xc_new = _merge(hyc, attc, gmc, projc, xc, mod_c, lp['g_post'], lp['w_hy_o'], lp['w_att_o'], lp['w_gm_o'],
                    lp['w_out'], C)
    return x_new, xc_new


def kernel(x, c, ctx, c_ctx, w_mod, b_mod, g_pre, g_post, w_in, hy_conv_w, hy_conv_b, hy_w1, hy_b1, hy_w2, hy_b2, hy_w3, hy_freq, hy_decay, hy_bias, q_gain, k_gain, gm_ln_g, gm_ln_b, gm_ws, gm_bs, w_hy_o, w_att_o, w_gm_o, w_out):
    B, L, D = x.shape
    rope = _rope_tables(L)

    R = -(-(B + 1) // (2 * SUBLANES)) * (2 * SUBLANES)
    cc = jnp.zeros((R, D), f32).at[:B].set(c).at[B].set(c_ctx)
    mod = _modulation(cc, w_mod, b_mod)

    w_in_p = jnp.concatenate([w_in[:, :, a:b].astype(bf16) for a, b in _COL_RANGES], axis=2)
    gm_bs_full = jnp.repeat(jnp.swapaxes(gm_bs, 1, 2), GM_GROUP_DIM, axis=2)

    xc = ctx
    for i in range(DEPTH):
        lp = {
            'g_pre': g_pre[i], 'g_post': g_post[i], 'w_in': w_in_p[i],
            'hy_conv_w': hy_conv_w[i], 'hy_conv_b': hy_conv_b[i],
            'hy_w1': hy_w1[i], 'hy_b1': hy_b1[i], 'hy_w2': hy_w2[i], 'hy_b2': hy_b2[i],
            'hy_w3': hy_w3[i], 'hy_freq': hy_freq[i], 'hy_decay': hy_decay[i], 'hy_bias': hy_bias[i],
            'q_gain': q_gain[i], 'k_gain': k_gain[i], 'gm_ln_g': gm_ln_g[i], 'gm_ln_b': gm_ln_b[i],
            'gm_ws': gm_ws[i].astype(bf16), 'gm_bs_full': gm_bs_full[i],
            'w_hy_o': w_hy_o[i].astype(bf16), 'w_att_o': w_att_o[i].astype(bf16),
            'w_gm_o': w_gm_o[i].astype(bf16), 'w_out': w_out[i].astype(bf16),
        }
        mod_x = mod[i, :B].reshape(B, 1, 3 * D)
        mod_c = jnp.broadcast_to(mod[i, B].reshape(1, 1, 3 * D), (B, 1, 3 * D))
        x, xc = _layer(x, xc, mod_x, mod_c, rope, lp, i < DEPTH - 1)
    return x
```

```python
import functools
import math

import numpy as np
import jax
import jax.numpy as jnp
from jax import lax
from jax.experimental import pallas as pl
from jax.experimental.pallas import tpu as pltpu

f32 = jnp.float32
bf16 = jnp.bfloat16

D_MODEL = 1024
DEPTH = 2
GRID_W = 64
RMS_EPS = 1e-6
LN_EPS = 1e-5

HY_WIDTH = D_MODEL
HY_ORDER = 2
HY_BANDS = 16
HY_EMB = 1 + 2 * HY_BANDS
HY_FILTER_HIDDEN = 64

HEAD_DIM = 128
ATT_HEADS = D_MODEL // HEAD_DIM
ATT_KV_HEADS = 2
ATT_GROUP = ATT_HEADS // ATT_KV_HEADS
ATT_WIDTH = ATT_HEADS * HEAD_DIM
KV_WIDTH = ATT_KV_HEADS * HEAD_DIM
AXIS_DIM = HEAD_DIM // 2
ROPE_THETA = 10000.0

GM_WIDTH = D_MODEL
GM_GROUPS = 8
GM_GROUP_DIM = GM_WIDTH // GM_GROUPS
CHUNK = 128

OFF_HY = 0
OFF_HY_GATE = OFF_HY + (HY_ORDER + 1) * HY_WIDTH
OFF_Q = OFF_HY_GATE + HY_WIDTH
OFF_K = OFF_Q + ATT_WIDTH
OFF_V = OFF_K + KV_WIDTH
OFF_ATT_GATE = OFF_V + KV_WIDTH
OFF_GM = OFF_ATT_GATE + ATT_WIDTH
OFF_GM_GATE = OFF_GM + 2 * GM_WIDTH
OFF_MERGE = OFF_GM_GATE + GM_WIDTH
IN_WIDTH = OFF_MERGE + 3 * D_MODEL

P_HY = 0
P_HY_GATE = 3 * D_MODEL
P_Q = 4 * D_MODEL
P_ATT_GATE = 5 * D_MODEL
P_GM = 6 * D_MODEL
P_GM_GATE = 8 * D_MODEL
P_MERGE = 9 * D_MODEL
P_K = 12 * D_MODEL
P_V = P_K + KV_WIDTH
_COL_RANGES = ((OFF_HY, OFF_K), (OFF_ATT_GATE, IN_WIDTH), (OFF_K, OFF_ATT_GATE))

LANES = 128
SUBLANES = 8
FFT_COLS = 256
FFT_GROUP = 32
INPROJ_COLS = 2560
FFT_CHUNK = 1024
VMEM_LIMIT = 52 * 1024 * 1024
FFT_VMEM_LIMIT = 58 * 1024 * 1024

_SM_SCALE_LOG2E = (HEAD_DIM ** -0.5) * math.log2(math.e)


def _params(sem, vmem=VMEM_LIMIT):
    return pltpu.CompilerParams(dimension_semantics=sem, vmem_limit_bytes=vmem)


def _silu(x):
    return x * jax.nn.sigmoid(x)


def _mod_kernel(cc_ref, w_ref, b_ref, o_ref):
    s = _silu(cc_ref[...]).astype(bf16)
    o_ref[0] = jnp.dot(s, w_ref[0].astype(bf16), preferred_element_type=f32) + b_ref[0]


def _modulation(cc, w_mod, b_mod):
    R, D = cc.shape
    tn = D
    return pl.pallas_call(
        _mod_kernel,
        grid=(DEPTH, 3 * D // tn),
        in_specs=[pl.BlockSpec((R, D), lambda i, n: (0, 0)),
                  pl.BlockSpec((1, D, tn), lambda i, n: (i, 0, n)),
                  pl.BlockSpec((1, 1, tn), lambda i, n: (i, 0, n))],
        out_specs=pl.BlockSpec((1, R, tn), lambda i, n: (i, 0, n)),
        out_shape=jax.ShapeDtypeStruct((DEPTH, R, 3 * D), f32),
        compiler_params=_params(("parallel", "parallel")),
    )(cc, w_mod, b_mod.reshape(DEPTH, 1, 3 * D))


def _inproj_kernel(x_ref, mod_ref, g_ref, w_ref, o_ref, h_scr):
    D = x_ref.shape[2]

    @pl.when(pl.program_id(2) == 0)
    def _():
        x = x_ref[0]
        r = lax.rsqrt(jnp.mean(x * x, axis=-1, keepdims=True) + RMS_EPS)
        shift = mod_ref[0, :, 0:D]
        scale = mod_ref[0, :, D:2 * D]
        h_scr[...] = ((x * r) * g_ref[...] * (1.0 + scale) + shift).astype(bf16)

    o_ref[0] = jnp.dot(h_scr[...], w_ref[...], preferred_element_type=f32).astype(o_ref.dtype)


def _inproj(x, mod, g_pre, w, tm, tn):
    B, L, D = x.shape
    N = w.shape[1]
    return pl.pallas_call(
        _inproj_kernel,
        grid=(B, L // tm, N // tn),
        in_specs=[pl.BlockSpec((1, tm, D), lambda b, m, n: (b, m, 0)),
                  pl.BlockSpec((1, 1, 3 * D), lambda b, m, n: (b, 0, 0)),
                  pl.BlockSpec((1, D), lambda b, m, n: (0, 0)),
                  pl.BlockSpec((D, tn), lambda b, m, n: (0, n))],
        out_specs=pl.BlockSpec((1, tm, tn), lambda b, m, n: (b, m, n)),
        out_shape=jax.ShapeDtypeStruct((B, L, N), bf16),
        scratch_shapes=[pltpu.VMEM((tm, D), bf16)],
        compiler_params=_params(("parallel", "parallel", "arbitrary")),
    )(x, mod, g_pre.reshape(1, D), w)


def _rope(x, cos, sin):
    lane = lax.broadcasted_iota(jnp.int32, x.shape, 1)
    first = (lane & (AXIS_DIM // 2)) == 0
    rot = jnp.where(first, pltpu.roll(x, HEAD_DIM - AXIS_DIM // 2, 1), pltpu.roll(x, AXIS_DIM // 2, 1))
    return x * cos + rot * sin


def _head_norm(x, gain):
    r = lax.rsqrt(jnp.mean(x * x, axis=-1, keepdims=True) + RMS_EPS)
    return x * r * gain


def _kvprep_kernel(c_ref, p_ref, cos_ref, sin_ref, gain_ref, k_ref, v1_ref):
    kv = jnp.where(pl.program_id(1) == 0, c_ref[0], p_ref[0])
    for h in range(ATT_KV_HEADS):
        sl = slice(h * HEAD_DIM, (h + 1) * HEAD_DIM)
        kn = _head_norm(kv[:, sl].astype(f32), gain_ref[...])
        k_ref[0, :, sl] = _rope(kn, cos_ref[...], sin_ref[...]).astype(bf16)
        v = kv[:, KV_WIDTH + h * HEAD_DIM:KV_WIDTH + (h + 1) * HEAD_DIM]
        v1_ref[0, :, 2 * h * HEAD_DIM:(2 * h + 1) * HEAD_DIM] = v
        v1_ref[0, :, (2 * h + 1) * HEAD_DIM:(2 * h + 2) * HEAD_DIM] = jnp.ones_like(v)


def _kvprep(projc, c_off, proj, cos, sin, gain):
    B, C, _ = projc.shape
    L = proj.shape[1]
    w = 2 * KV_WIDTH
    cb, pb = c_off // w, P_K // w
    return pl.pallas_call(
        _kvprep_kernel,
        grid=(B, (C + L) // C),
        in_specs=[pl.BlockSpec((1, C, w), lambda b, m: (b, 0, cb)),
                  pl.BlockSpec((1, C, w), lambda b, m: (b, jnp.maximum(m - 1, 0), pb)),
                  pl.BlockSpec((C, HEAD_DIM), lambda b, m: (m, 0)),
                  pl.BlockSpec((C, HEAD_DIM), lambda b, m: (m, 0)),
                  pl.BlockSpec((1, HEAD_DIM), lambda b, m: (0, 0))],
        out_specs=[pl.BlockSpec((1, C, KV_WIDTH), lambda b, m: (b, m, 0)),
                   pl.BlockSpec((1, C, w), lambda b, m: (b, m, 0))],
        out_shape=[jax.ShapeDtypeStruct((B, C + L, KV_WIDTH), bf16),
                   jax.ShapeDtypeStruct((B, C + L, w), bf16)],
        compiler_params=_params(("parallel", "parallel")),
    )(projc, proj, cos, sin, gain.reshape(1, HEAD_DIM))


def _attn_kernel(q_ref, k_ref, v1_ref, cos_ref, sin_ref, gain_ref, gate_ref, o_ref):
    k = k_ref[0]
    v1 = v1_ref[0]
    def scores(h):
        sl = slice(h * HEAD_DIM, (h + 1) * HEAD_DIM)
        qn = _head_norm(q_ref[0, :, sl].astype(f32), gain_ref[...])
        q = (_rope(qn, cos_ref[...], sin_ref[...]) * _SM_SCALE_LOG2E).astype(bf16)
        return lax.dot_general(q, k, (((1,), (1,)), ((), ())), preferred_element_type=f32)

    s_next = scores(0)
    for h in range(ATT_GROUP):
        sl = slice(h * HEAD_DIM, (h + 1) * HEAD_DIM)
        s = s_next
        if h + 1 < ATT_GROUP:
            s_next = scores(h + 1)
        p = jnp.exp2(s - jnp.max(s, axis=-1, keepdims=True))
        ol = jnp.dot(p.astype(bf16), v1, preferred_element_type=f32)
        o = ol[:, :HEAD_DIM] / ol[:, HEAD_DIM:HEAD_DIM + 1]
        o_ref[0, :, sl] = (o * _silu(gate_ref[0, :, sl].astype(f32))).astype(bf16)


def _attention(proj, k_all, vt_all, Lk, cos, sin, gain, tq):
    B, L, _ = proj.shape
    gw = ATT_GROUP * HEAD_DIM
    qb, gb = P_Q // gw, P_ATT_GATE // gw
    return pl.pallas_call(
        _attn_kernel,
        grid=(B, ATT_KV_HEADS, L // tq),
        in_specs=[pl.BlockSpec((1, tq, gw), lambda b, h, i: (b, i, qb + h)),
                  pl.BlockSpec((1, Lk, HEAD_DIM), lambda b, h, i: (b, 0, h)),
                  pl.BlockSpec((1, Lk, 2 * HEAD_DIM), lambda b, h, i: (b, 0, h)),
                  pl.BlockSpec((tq, HEAD_DIM), lambda b, h, i: (i, 0)),
                  pl.BlockSpec((tq, HEAD_DIM), lambda b, h, i: (i, 0)),
                  pl.BlockSpec((1, HEAD_DIM), lambda b, h, i: (0, 0)),
                  pl.BlockSpec((1, tq, gw), lambda b, h, i: (b, i, gb + h))],
        out_specs=pl.BlockSpec((1, tq, gw), lambda b, h, i: (b, i, h)),
        out_shape=jax.ShapeDtypeStruct((B, L, ATT_WIDTH), bf16),
        compiler_params=_params(("parallel", "parallel", "parallel")),
    )(proj, k_all, vt_all, cos, sin, gain.reshape(1, HEAD_DIM), proj)


def _gmlp_kernel(p_ref, gate_ref, lng_ref, lnb_ref, ws_ref, bs_ref, o_ref):
    tm = p_ref.shape[1]
    p = p_ref[0].astype(f32)
    uv = 0.5 * p * (1.0 + lax.erf(p * (2.0 ** -0.5)))
    u = uv[:, :GM_WIDTH]
    v = uv[:, GM_WIDTH:]
    vc = v - jnp.mean(v, axis=-1, keepdims=True)
    var = jnp.mean(vc * vc, axis=-1, keepdims=True)
    vn = (vc * lax.rsqrt(var + LN_EPS) * lng_ref[...] + lnb_ref[...]).astype(bf16)
    ug = u * _silu(gate_ref[0].astype(f32))
    for n in range(tm // CHUNK):
        rows = slice(n * CHUNK, (n + 1) * CHUNK)
        for g in range(GM_GROUPS):
            cols = slice(g * GM_GROUP_DIM, (g + 1) * GM_GROUP_DIM)
            y = jnp.dot(ws_ref[g], vn[rows, cols], preferred_element_type=f32) + bs_ref[:, cols]
            o_ref[0, rows, cols] = (ug[rows, cols] * y).astype(bf16)


def _gmlp(proj, ln_g, ln_b, ws, bs_full, tm):
    B, L, _ = proj.shape
    pb, gb = P_GM // (2 * GM_WIDTH), P_GM_GATE // GM_WIDTH
    return pl.pallas_call(
        _gmlp_kernel,
        grid=(B, L // tm),
        in_specs=[pl.BlockSpec((1, tm, 2 * GM_WIDTH), lambda b, m: (b, m, pb)),
                  pl.BlockSpec((1, tm, GM_WIDTH), lambda b, m: (b, m, gb)),
                  pl.BlockSpec((1, GM_WIDTH), lambda b, m: (0, 0)),
                  pl.BlockSpec((1, GM_WIDTH), lambda b, m: (0, 0)),
                  pl.BlockSpec((GM_GROUPS, CHUNK, CHUNK), lambda b, m: (0, 0, 0)),
                  pl.BlockSpec((CHUNK, GM_WIDTH), lambda b, m: (0, 0))],
        out_specs=pl.BlockSpec((1, tm, GM_WIDTH), lambda b, m: (b, m, 0)),
        out_shape=jax.ShapeDtypeStruct((B, L, GM_WIDTH), bf16),
        compiler_params=_params(("parallel", "parallel")),
    )(proj, proj, ln_g.reshape(1, GM_WIDTH), ln_b.reshape(1, GM_WIDTH), ws, bs_full)


def _shortconv_kernel(p_ref, w_ref, b_ref, o_ref):
    L = p_ref.shape[1]
    x = p_ref[0].astype(f32)
    row = lax.broadcasted_iota(jnp.int32, x.shape, 0)
    xm = jnp.where(row == 0, 0.0, pltpu.roll(x, 1, 0))
    xp = jnp.where(row == L - 1, 0.0, pltpu.roll(x, L - 1, 0))
    o_ref[0] = (xm * w_ref[0:1, :] + x * w_ref[1:2, :] + xp * w_ref[2:3, :] + b_ref[...]).astype(bf16)


def _shortconv(proj, w, b, tc):
    B, L, _ = proj.shape
    W = w.shape[1]
    return pl.pallas_call(
        _shortconv_kernel,
        grid=(B, W // tc),
        in_specs=[pl.BlockSpec((1, L, tc), lambda b, c: (b, 0, c)),
                  pl.BlockSpec((3, tc), lambda b, c: (0, c)),
                  pl.BlockSpec((1, tc), lambda b, c: (0, c))],
        out_specs=pl.BlockSpec((1, L, tc), lambda b, c: (b, 0, c)),
        out_shape=jax.ShapeDtypeStruct((B, L, W), bf16),
        compiler_params=_params(("parallel", "parallel")),
    )(proj, w, b.reshape(1, W))


def _filter_embedding(L):
    n = np.arange(2 * L)
    pos = np.where(n < L, n, 2 * L - n)
    pos = np.where(n == L, 0, pos)
    t = np.linspace(0.0, 1.0, L)[pos]
    bands = np.linspace(1e-4, HY_BANDS - 1, HY_BANDS)
    ang = (2.0 * math.pi / L) * pos[:, None] * bands[None, :]
    z = np.zeros((2 * L, LANES), np.float64)
    z[:, 0] = t
    z[:, 1:1 + HY_BANDS] = np.cos(ang)
    z[:, 1 + HY_BANDS:HY_EMB] = -np.sin(ang)
    z[:, HY_EMB] = (n < L)
    z[:, HY_EMB + 1] = (n > L) | (n == 0)
    return jnp.asarray(z, f32)


def _filter_kernel(z_ref, w1_ref, b1_ref, w2_ref, b2_ref, freq_ref, w3_ref, dec_ref, o_ref):
    hp = lax.Precision.HIGHEST
    C = o_ref.shape[2]
    z = z_ref[...]
    t = z[:, 0:1]
    mf = z[:, HY_EMB:HY_EMB + 1]
    mb = z[:, HY_EMB + 1:HY_EMB + 2]
    h = jnp.sin(freq_ref[0:1, :] * (jnp.dot(z, w1_ref[...], precision=hp, preferred_element_type=f32) + b1_ref[...]))
    h = jnp.sin(freq_ref[1:2, :] * (jnp.dot(h, w2_ref[...], precision=hp, preferred_element_type=f32) + b2_ref[...]))
    hi = h.astype(bf16)
    lo = (h - hi.astype(f32)).astype(bf16)
    h3 = jnp.concatenate([hi, lo, hi], axis=1)
    for blk in range(2 * HY_ORDER):
        cols = slice(blk * C, (blk + 1) * C)
        taps = jnp.dot(h3, w3_ref[:, cols], preferred_element_type=f32)
        taps = (mb if blk % 2 else mf) * taps * jnp.exp(-t * jnp.abs(dec_ref[:, cols]))
        if blk % 2:
            o_ref[blk // 2] = o_ref[blk // 2] + taps
        else:
            o_ref[blk // 2] = taps


def _filters(L, w1, b1, w2, b2, w3, freq, decay, tr):
    H = HY_FILTER_HIDDEN
    C = HY_WIDTH
    W = 2 * HY_ORDER * C
    z = _filter_embedding(L)
    w1p = jnp.zeros((LANES, H), f32).at[:HY_EMB].set(w1)
    w3_hi = w3.astype(bf16)
    w3_lo = (w3 - w3_hi.astype(f32)).astype(bf16)
    w3 = jnp.concatenate([w3_hi, w3_hi, w3_lo], axis=0)
    full = lambda shape: pl.BlockSpec(shape, lambda r: (0,) * len(shape))
    return pl.pallas_call(
        _filter_kernel,
        grid=(2 * L // tr,),
        in_specs=[pl.BlockSpec((tr, LANES), lambda r: (r, 0)),
                  full((LANES, H)), full((1, H)), full((H, H)), full((1, H)), full((2, H)),
                  full((3 * H, W)), full((1, W))],
        out_specs=pl.BlockSpec((HY_ORDER, tr, C), lambda r: (0, r, 0)),
        out_shape=jax.ShapeDtypeStruct((HY_ORDER, 2 * L, C), f32),
        compiler_params=_params(("parallel",)),
    )(z, w1p, b1.reshape(1, H), w2, b2.reshape(1, H), freq, w3, decay.reshape(1, W))


def _fft_dims(L):
    n2 = SUBLANES
    while (2 * n2) * (2 * n2) <= L:
        n2 *= 2
    return n2, L // n2, 2 * (L // n2), n2 + SUBLANES


def _a_pitch(L):
    return _fft_dims(L)[2] + SUBLANES


def _stack(fr, fi):
    return np.block([[fr, -fi], [fi, fr]])


@functools.lru_cache(maxsize=None)
def _dft_tables(L):
    N2, N1h, N1, _ = _fft_dims(L)
    N = 2 * L
    k1 = np.arange(N1)
    a1 = -2.0 * np.pi * np.outer(k1, np.arange(N1)) / N1
    f1 = _stack(np.cos(a1[:, :N1h]), np.sin(a1[:, :N1h]))
    f1k = np.concatenate([np.cos(a1), np.sin(a1)], axis=0)
    f3 = _stack(np.cos(-a1[:, :N1h].T), np.sin(-a1[:, :N1h].T))
    n2 = np.arange(N2)
    a2 = -2.0 * np.pi * (np.outer(n2, n2)[None] / N2 + (k1[:, None, None] * n2[None, None, :]) / N)
    f2 = np.stack([_stack(np.cos(a), np.sin(a)) for a in a2])
    a2t = -np.transpose(a2, (0, 2, 1))
    g2 = np.stack([_stack(np.cos(a), np.sin(a)) for a in a2t])
    return tuple(jnp.asarray(m, bf16) for m in (f1, f1k, f2, g2, f3))


def _load_cols(ref, lead, rows):
    return jnp.concatenate([ref[lead + (t, rows, slice(None))] for t in range(ref.shape[-3])], axis=1)


def _store_cols(ref, lead, rows, val):
    for t in range(ref.shape[-3]):
        ref[lead + (t, rows, slice(None))] = val[:, t * LANES:(t + 1) * LANES]


def _grouped_loop(n, group, load, compute, store):
    group = min(group, n)
    assert n % group == 0

    def body(i, c):
        idx = [i * group + u for u in range(group)]
        vals = [load(ix) for ix in idx]
        outs = [compute(ix, v) for ix, v in zip(idx, vals)]
        for ix, o in zip(idx, outs):
            store(ix, o)
        return c

    lax.fori_loop(0, n // group, body, 0)


def _load_ri(ref, rows):
    return jnp.concatenate([_load_cols(ref, (ri,), rows) for ri in range(2)], axis=0).astype(bf16)


def _store_ri(ref, rows, val):
    h = val.shape[0] // 2
    for ri in range(2):
        _store_cols(ref, (ri,), rows, val[ri * h:(ri + 1) * h])


def _spectrum_kernel(k_ref, f1k_ref, f2_ref, o_ref, a_scr, *, L):
    N2, N1h, N1, P = _fft_dims(L)
    PA = _a_pitch(L)
    T = a_scr.shape[1]
    inv_n = 1.0 / (2 * L)

    def load1(n2):
        return jnp.concatenate([k_ref[0, pl.ds(n2, N1, stride=N2), t * LANES:(t + 1) * LANES]
                                for t in range(T)], axis=1).astype(bf16)

    _grouped_loop(N2, FFT_GROUP, load1,
                  lambda n2, w: jnp.dot(f1k_ref[...], w, preferred_element_type=f32),
                  lambda n2, r: _store_ri(a_scr, pl.ds(pl.multiple_of(n2 * PA, SUBLANES), N1), r))

    def store2(k1, x):
        out_rows = pl.ds(pl.multiple_of(k1 * N2, N2), N2)
        o_ref[0, 0, out_rows, :] = x[:N2].astype(bf16)
        o_ref[0, 1, out_rows, :] = x[N2:].astype(bf16)

    _grouped_loop(N1, FFT_GROUP,
                  lambda k1: _load_ri(a_scr, pl.ds(k1, N2, stride=PA)),
                  lambda k1, a: jnp.dot(f2_ref[k1], a, preferred_element_type=f32) * inv_n,
                  store2)


def _spectrum(k, tc):
    _, N, C = k.shape
    L = N // 2
    N2, N1h, N1, P = _fft_dims(L)
    _, f1k, f2, _, _ = _dft_tables(L)
    return pl.pallas_call(
        functools.partial(_spectrum_kernel, L=L),
        grid=(HY_ORDER, C // tc),
        in_specs=[pl.BlockSpec((1, N, tc), lambda o, c: (o, 0, c)),
                  pl.BlockSpec(f1k.shape, lambda o, c: (0, 0)),
                  pl.BlockSpec(f2.shape, lambda o, c: (0, 0, 0))],
        out_specs=pl.BlockSpec((1, 2, N, tc), lambda o, c: (o, 0, 0, c)),
        out_shape=jax.ShapeDtypeStruct((HY_ORDER, 2, N, C), bf16),
        scratch_shapes=[pltpu.VMEM((2, tc // LANES, N2 * _a_pitch(L), LANES), f32)],
        compiler_params=_params(("parallel", "parallel")),
    )(k, f1k, f2)


def _fftconv_kernel(*refs, L, chunk, has_gate):
    if has_gate:
        u_in, u_ep, g_ref, gate_ref, bias_ref, kf_ref, f1_ref, f2_ref, g2_ref, f3_ref, o_ref, z_scr, a_scr = refs
    else:
        u_in, u_ep, g_ref, bias_ref, kf_ref, f1_ref, f2_ref, g2_ref, f3_ref, o_ref, z_scr, a_scr = refs
    N2, N1h, N1, P = _fft_dims(L)
    J = L // chunk
    nb = chunk // N2
    j = pl.program_id(2)

    @pl.when(j < J)
    def _load():
        for ri in range(2):
            for blk in range(nb):
                rows = pl.ds(pl.multiple_of((j * nb + blk) * P, SUBLANES), N2)
                _store_cols(z_scr, (ri,), rows, u_in[ri, blk * N2:(blk + 1) * N2, :].astype(f32))

    @pl.when(j == J - 1)
    def _transform():
        PA = _a_pitch(L)

        def a_block(n2):
            return pl.ds(pl.multiple_of(n2 * PA, SUBLANES), N1)

        _grouped_loop(N2, FFT_GROUP,
                      lambda n2: _load_ri(z_scr, pl.ds(n2, N1h, stride=P)),
                      lambda n2, w: jnp.dot(f1_ref[...], w, preferred_element_type=f32),
                      lambda n2, r: _store_ri(a_scr, a_block(n2), r))

        def block_rows(k1):
            return pl.ds(k1, N2, stride=PA)

        def middle(k1, a):
            x = jnp.dot(f2_ref[k1], a, preferred_element_type=f32)
            krows = pl.ds(pl.multiple_of(k1 * N2, N2), N2)
            kr = kf_ref[0, 0, krows, :].astype(f32)
            ki = kf_ref[0, 1, krows, :].astype(f32)
            xr, xi = x[:N2], x[N2:]
            y = jnp.concatenate([xr * kr - xi * ki, xr * ki + xi * kr], axis=0).astype(bf16)
            return jnp.dot(g2_ref[k1], y, preferred_element_type=f32)

        _grouped_loop(N1, FFT_GROUP,
                      lambda k1: _load_ri(a_scr, block_rows(k1)),
                      middle,
                      lambda k1, b: _store_ri(a_scr, block_rows(k1), b))

        _grouped_loop(N2, FFT_GROUP,
                      lambda n2: _load_ri(a_scr, a_block(n2)),
                      lambda n2, b: jnp.dot(f3_ref[...], b, preferred_element_type=f32),
                      lambda n2, y: _store_ri(z_scr, pl.ds(n2, N1h, stride=P), y))

    @pl.when(j >= J)
    def _epilogue():
        for ri in range(2):
            for blk in range(nb):
                rows = pl.ds(pl.multiple_of(((j - J) * nb + blk) * P, SUBLANES), N2)
                y = _load_cols(z_scr, (ri,), rows)
                sl = slice(blk * N2, (blk + 1) * N2)
                u = u_ep[ri, sl, :].astype(f32)
                o = g_ref[ri, sl, :].astype(f32) * (y + bias_ref[...] * u)
                if has_gate:
                    o = o * _silu(gate_ref[ri, sl, :].astype(f32))
                o_ref[ri, sl, :] = o.astype(bf16)


def _fftconv(u, u_cb, g, g_cb, gate, gate_cb, bias, kf, order, tc):
    B, L, _ = u.shape
    C = HY_WIDTH
    N2, N1h, N1, P = _fft_dims(L)
    chunk = min(L, FFT_CHUNK)
    J = L // chunk
    f1, _, f2, g2, f3 = _dft_tables(L)
    T = tc // LANES
    in_map = lambda cb: (lambda c, p, j: (p, jnp.minimum(j, J - 1), cb + c))
    ep_map = lambda cb: (lambda c, p, j: (p, jnp.maximum(j - J, 0), cb + c))
    const = lambda nd: (lambda c, p, j: (0,) * nd)
    once = pl.Buffered(1)
    ops = [u, u, g]
    specs = [pl.BlockSpec((2, chunk, tc), in_map(u_cb)),
             pl.BlockSpec((2, chunk, tc), ep_map(u_cb)),
             pl.BlockSpec((2, chunk, tc), ep_map(g_cb))]
    if gate is not None:
        ops.append(gate)
        specs.append(pl.BlockSpec((2, chunk, tc), ep_map(gate_cb)))
    ops += [bias.reshape(HY_ORDER, 1, C), kf, f1, f2, g2, f3]
    specs += [pl.BlockSpec((None, 1, tc), lambda c, p, j: (order, 0, c)),
              pl.BlockSpec((1, 2, 2 * L, tc), lambda c, p, j: (order, 0, 0, c), pipeline_mode=once),
              pl.BlockSpec(f1.shape, const(2), pipeline_mode=once),
              pl.BlockSpec(f2.shape, const(3), pipeline_mode=once),
              pl.BlockSpec(g2.shape, const(3), pipeline_mode=once),
              pl.BlockSpec(f3.shape, const(2), pipeline_mode=once)]
    return pl.pallas_call(
        functools.partial(_fftconv_kernel, L=L, chunk=chunk, has_gate=gate is not None),
        grid=(C // tc, B // 2, 2 * J),
        in_specs=specs,
        out_specs=pl.BlockSpec((2, chunk, tc), ep_map(0)),
        out_shape=jax.ShapeDtypeStruct((B, L, C), bf16),
        scratch_shapes=[pltpu.VMEM((2, T, N1h * P, LANES), f32),
                        pltpu.VMEM((2, T, N2 * _a_pitch(L), LANES), f32)],
        compiler_params=_params(("parallel", "parallel", "arbitrary"), FFT_VMEM_LIMIT),
    )(*ops)


def _merge_kernel(hy_ref, at_ref, gm_ref, mg_ref, x_ref, gate_ref, gpost_ref,
                  why_ref, wat_ref, wgm_ref, wout_ref, o_ref):
    D = x_ref.shape[2]
    acc = None
    for i, (br, w) in enumerate(((hy_ref, why_ref), (at_ref, wat_ref), (gm_ref, wgm_ref))):
        y = jnp.dot(br[0], w[...], preferred_element_type=f32)
        y = jax.nn.sigmoid(mg_ref[0, :, i * D:(i + 1) * D].astype(f32)) * y
        acc = y if acc is None else acc + y
    o = jnp.dot(acc.astype(bf16), wout_ref[...], preferred_element_type=f32)
    r = o * lax.rsqrt(jnp.mean(o * o, axis=-1, keepdims=True) + RMS_EPS) * gpost_ref[...]
    o_ref[0] = x_ref[0] + gate_ref[0] * r


def _merge(hy, at, gm, proj, x, mod, g_post, w_hy, w_at, w_gm, w_out, tm):
    B, L, D = x.shape
    mb = P_MERGE // (3 * D)
    row = lambda b, m: (b, m, 0)
    wspec = pl.BlockSpec((D, D), lambda b, m: (0, 0))
    return pl.pallas_call(
        _merge_kernel,
        grid=(B, L // tm),
        in_specs=[pl.BlockSpec((1, tm, D), row), pl.BlockSpec((1, tm, D), row), pl.BlockSpec((1, tm, D), row),
                  pl.BlockSpec((1, tm, 3 * D), lambda b, m: (b, m, mb)),
                  pl.BlockSpec((1, tm, D), row),
                  pl.BlockSpec((1, 1, D), lambda b, m: (b, 0, 2)),
                  pl.BlockSpec((1, D), lambda b, m: (0, 0)),
                  wspec, wspec, wspec, wspec],
        out_specs=pl.BlockSpec((1, tm, D), row),
        out_shape=jax.ShapeDtypeStruct((B, L, D), f32),
        compiler_params=_params(("parallel", "parallel")),
    )(hy, at, gm, proj, x, mod, g_post.reshape(1, D), w_hy, w_at, w_gm, w_out)


def _rope_tables(L):
    pos = np.arange(L)
    inv = np.power(ROPE_THETA, -np.arange(0, AXIS_DIM, 2) / AXIS_DIM)
    ar = (pos // GRID_W)[:, None] * inv
    ac = (pos % GRID_W)[:, None] * inv
    cos = np.concatenate([np.cos(ar), np.cos(ar), np.cos(ac), np.cos(ac)], axis=1)
    sin = np.concatenate([-np.sin(ar), np.sin(ar), -np.sin(ac), np.sin(ac)], axis=1)
    return jnp.asarray(cos, f32), jnp.asarray(sin, f32)


def _identity_rope(L):
    return jnp.ones((L, HEAD_DIM), f32), jnp.zeros((L, HEAD_DIM), f32)


def _hyena(proj, lp):
    B, L, _ = proj.shape
    tc = FFT_COLS
    nct = HY_WIDTH // tc
    hyc = _shortconv(proj, lp['hy_conv_w'], lp['hy_conv_b'], tc)
    k = _filters(L, lp['hy_w1'], lp['hy_b1'], lp['hy_w2'], lp['hy_b2'], lp['hy_w3'], lp['hy_freq'],
                 lp['hy_decay'], min(2 * L, 256))
    kf = _spectrum(k, LANES)
    z1 = _fftconv(hyc, 2 * nct, hyc, 0, None, 0, lp['hy_bias'], kf, 0, tc)
    return _fftconv(z1, 0, hyc, nct, proj, P_HY_GATE // tc, lp['hy_bias'], kf, 1, tc)


def _mixer(proj, k_all, v1_all, Lk, rope_q, lp, tq, tm_gm):
    att = _attention(proj, k_all, v1_all, Lk, rope_q[0], rope_q[1], lp['q_gain'], tq)
    hy = _hyena(proj, lp)
    gm = _gmlp(proj, lp['gm_ln_g'], lp['gm_ln_b'], lp['gm_ws'], lp['gm_bs_full'], tm_gm)
    return hy, att, gm


def _layer(x, xc, mod_x, mod_c, rope, lp, ctx_out):
    B, L, D = x.shape
    C = xc.shape[1]
    w_in = lp['w_in']
    proj = _inproj(x, mod_x, lp['g_pre'], w_in, min(L, 1024), INPROJ_COLS)
    wc = w_in if ctx_out else w_in[:, P_K:]
    projc = _inproj(xc, mod_c, lp['g_pre'], wc, C, min(wc.shape[1], INPROJ_COLS))
    ident = _identity_rope(C)
    k_all, v1_all = _kvprep(projc, P_K if ctx_out else 0, proj,
                            jnp.concatenate([ident[0], rope[0]]), jnp.concatenate([ident[1], rope[1]]),
                            lp['k_gain'])

    hy, att, gm = _mixer(proj, k_all, v1_all, C + L, rope, lp, 256, 256)
    x_new = _merge(hy, att, gm, proj, x, mod_x, lp['g_post'], lp['w_hy_o'], lp['w_att_o'], lp['w_gm_o'],
                   lp['w_out'], min(L, 512))
    if not ctx_out:
        return x_new, xc
    hyc, attc, gmc = _mixer(projc, k_all, v1_all, C, ident, lp, C, min(C, 256))
    xc_new = _merge(hyc, attc, gmc, projc, xc, mod_c, lp['g_post'], lp['w_hy_o'], lp['w_att_o'], lp['w_gm_o'],
                    lp['w_out'], C)
    return x_new, xc_new


def kernel(x, c, ctx, c_ctx, w_mod, b_mod, g_pre, g_post, w_in, hy_conv_w, hy_conv_b, hy_w1, hy_b1, hy_w2, hy_b2, hy_w3, hy_freq, hy_decay, hy_bias, q_gain, k_gain, gm_ln_g, gm_ln_b, gm_ws, gm_bs, w_hy_o, w_att_o, w_gm_o, w_out):
    B, L, D = x.shape
    rope = _rope_tables(L)

    R = -(-(B + 1) // (2 * SUBLANES)) * (2 * SUBLANES)
    cc = jnp.zeros((R, D), f32).at[:B].set(c).at[B].set(c_ctx)
    mod = _modulation(cc, w_mod, b_mod)

    w_in_p = jnp.concatenate([w_in[:, :, a:b].astype(bf16) for a, b in _COL_RANGES], axis=2)
    gm_bs_full = jnp.repeat(jnp.swapaxes(gm_bs, 1, 2), GM_GROUP_DIM, axis=2)

    xc = ctx
    for i in range(DEPTH):
        lp = {
            'g_pre': g_pre[i], 'g_post': g_post[i], 'w_in': w_in_p[i],
            'hy_conv_w': hy_conv_w[i], 'hy_conv_b': hy_conv_b[i],
            'hy_w1': hy_w1[i], 'hy_b1': hy_b1[i], 'hy_w2': hy_w2[i], 'hy_b2': hy_b2[i],
            'hy_w3': hy_w3[i], 'hy_freq': hy_freq[i], 'hy_decay': hy_decay[i], 'hy_bias': hy_bias[i],
            'q_gain': q_gain[i], 'k_gain': k_gain[i], 'gm_ln_g': gm_ln_g[i], 'gm_ln_b': gm_ln_b[i],
            'gm_ws': gm_ws[i].astype(bf16), 'gm_bs_full': gm_bs_full[i],
            'w_hy_o': w_hy_o[i].astype(bf16), 'w_att_o': w_att_o[i].astype(bf16),
            'w_gm_o': w_gm_o[i].astype(bf16), 'w_out': w_out[i].astype(bf16),
        }
        mod_x = mod[i, :B].reshape(B, 1, 3 * D)
        mod_c = jnp.broadcast_to(mod[i, B].reshape(1, 1, 3 * D), (B, 1, 3 * D))
        x, xc = _layer(x, xc, mod_x, mod_c, rope, lp, i < DEPTH - 1)
    return x
```

```python
import functools
import math

import numpy as np
import jax
import jax.numpy as jnp
from jax import lax
from jax.experimental import pallas as pl
from jax.experimental.pallas import tpu as pltpu

f32 = jnp.float32
bf16 = jnp.bfloat16

D_MODEL = 1024
DEPTH = 2
GRID_W = 64
RMS_EPS = 1e-6
LN_EPS = 1e-5

HY_WIDTH = D_MODEL
HY_ORDER = 2
HY_BANDS = 16
HY_EMB = 1 + 2 * HY_BANDS
HY_FILTER_HIDDEN = 64

HEAD_DIM = 128
ATT_HEADS = D_MODEL // HEAD_DIM
ATT_KV_HEADS = 2
ATT_GROUP = ATT_HEADS // ATT_KV_HEADS
ATT_WIDTH = ATT_HEADS * HEAD_DIM
KV_WIDTH = ATT_KV_HEADS * HEAD_DIM
AXIS_DIM = HEAD_DIM // 2
ROPE_THETA = 10000.0

GM_WIDTH = D_MODEL
GM_GROUPS = 8
GM_GROUP_DIM = GM_WIDTH // GM_GROUPS
CHUNK = 128

OFF_HY = 0
OFF_HY_GATE = OFF_HY + (HY_ORDER + 1) * HY_WIDTH
OFF_Q = OFF_HY_GATE + HY_WIDTH
OFF_K = OFF_Q + ATT_WIDTH
OFF_V = OFF_K + KV_WIDTH
OFF_ATT_GATE = OFF_V + KV_WIDTH
OFF_GM = OFF_ATT_GATE + ATT_WIDTH
OFF_GM_GATE = OFF_GM + 2 * GM_WIDTH
OFF_MERGE = OFF_GM_GATE + GM_WIDTH
IN_WIDTH = OFF_MERGE + 3 * D_MODEL

P_HY = 0
P_HY_GATE = 3 * D_MODEL
P_Q = 4 * D_MODEL
P_ATT_GATE = 5 * D_MODEL
P_GM = 6 * D_MODEL
P_GM_GATE = 8 * D_MODEL
P_MERGE = 9 * D_MODEL
P_K = 12 * D_MODEL
P_V = P_K + KV_WIDTH
_COL_RANGES = ((OFF_HY, OFF_K), (OFF_ATT_GATE, IN_WIDTH), (OFF_K, OFF_ATT_GATE))

LANES = 128
SUBLANES = 8
FFT_COLS = 256
FFT_GROUP = 32
INPROJ_COLS = 2560
ATT_AHEAD = 1
ATT_ROWS = 256
ATT_TQ = 512
FFT_CHUNK = 1024
VMEM_LIMIT = 52 * 1024 * 1024
FFT_VMEM_LIMIT = 58 * 1024 * 1024

_SM_SCALE_LOG2E = (HEAD_DIM ** -0.5) * math.log2(math.e)


def _params(sem, vmem=VMEM_LIMIT):
    return pltpu.CompilerParams(dimension_semantics=sem, vmem_limit_bytes=vmem)


def _silu(x):
    return x * jax.nn.sigmoid(x)


def _mod_kernel(cc_ref, w_ref, b_ref, o_ref):
    s = _silu(cc_ref[...]).astype(bf16)
    o_ref[0] = jnp.dot(s, w_ref[0].astype(bf16), preferred_element_type=f32) + b_ref[0]


def _modulation(cc, w_mod, b_mod):
    R, D = cc.shape
    tn = D
    return pl.pallas_call(
        _mod_kernel,
        grid=(DEPTH, 3 * D // tn),
        in_specs=[pl.BlockSpec((R, D), lambda i, n: (0, 0)),
                  pl.BlockSpec((1, D, tn), lambda i, n: (i, 0, n)),
                  pl.BlockSpec((1, 1, tn), lambda i, n: (i, 0, n))],
        out_specs=pl.BlockSpec((1, R, tn), lambda i, n: (i, 0, n)),
        out_shape=jax.ShapeDtypeStruct((DEPTH, R, 3 * D), f32),
        compiler_params=_params(("parallel", "parallel")),
    )(cc, w_mod, b_mod.reshape(DEPTH, 1, 3 * D))


def _inproj_kernel(x_ref, mod_ref, g_ref, w_ref, o_ref, h_scr):
    D = x_ref.shape[2]

    @pl.when(pl.program_id(2) == 0)
    def _():
        x = x_ref[0]
        r = lax.rsqrt(jnp.mean(x * x, axis=-1, keepdims=True) + RMS_EPS)
        shift = mod_ref[0, :, 0:D]
        scale = mod_ref[0, :, D:2 * D]
        h_scr[...] = ((x * r) * g_ref[...] * (1.0 + scale) + shift).astype(bf16)

    o_ref[0] = jnp.dot(h_scr[...], w_ref[...], preferred_element_type=f32).astype(o_ref.dtype)


def _inproj(x, mod, g_pre, w, tm, tn):
    B, L, D = x.shape
    N = w.shape[1]
    return pl.pallas_call(
        _inproj_kernel,
        grid=(B, L // tm, N // tn),
        in_specs=[pl.BlockSpec((1, tm, D), lambda b, m, n: (b, m, 0)),
                  pl.BlockSpec((1, 1, 3 * D), lambda b, m, n: (b, 0, 0)),
                  pl.BlockSpec((1, D), lambda b, m, n: (0, 0)),
                  pl.BlockSpec((D, tn), lambda b, m, n: (0, n))],
        out_specs=pl.BlockSpec((1, tm, tn), lambda b, m, n: (b, m, n)),
        out_shape=jax.ShapeDtypeStruct((B, L, N), bf16),
        scratch_shapes=[pltpu.VMEM((tm, D), bf16)],
        compiler_params=_params(("parallel", "parallel", "arbitrary")),
    )(x, mod, g_pre.reshape(1, D), w)


def _rope(x, cos, sin):
    lane = lax.broadcasted_iota(jnp.int32, x.shape, 1)
    first = (lane & (AXIS_DIM // 2)) == 0
    rot = jnp.where(first, pltpu.roll(x, HEAD_DIM - AXIS_DIM // 2, 1), pltpu.roll(x, AXIS_DIM // 2, 1))
    return x * cos + rot * sin


def _head_norm(x, gain):
    r = lax.rsqrt(jnp.mean(x * x, axis=-1, keepdims=True) + RMS_EPS)
    return x * r * gain


def _kvprep_kernel(kv_ref, cos_ref, sin_ref, gain_ref, *rest):
    k_ref, v1_ref = rest[-2:]
    kv = kv_ref[0]
    for h in range(ATT_KV_HEADS):
        sl = slice(h * HEAD_DIM, (h + 1) * HEAD_DIM)
        kn = _head_norm(kv[:, sl].astype(f32), gain_ref[...])
        k_ref[0, :, sl] = _rope(kn, cos_ref[...], sin_ref[...]).astype(bf16)
        v = kv[:, KV_WIDTH + h * HEAD_DIM:KV_WIDTH + (h + 1) * HEAD_DIM]
        v1_ref[0, :, 2 * h * HEAD_DIM:(2 * h + 1) * HEAD_DIM] = v
        v1_ref[0, :, (2 * h + 1) * HEAD_DIM:(2 * h + 2) * HEAD_DIM] = jnp.ones_like(v)


def _kvprep(src, col_off, cos, sin, gain, Lk, row_off, tm, into=None):
    B, n, _ = src.shape
    w = 2 * KV_WIDTH
    cb, rb = col_off // w, row_off // tm
    in_specs = [pl.BlockSpec((1, tm, w), lambda b, m: (b, m, cb)),
                pl.BlockSpec((tm, HEAD_DIM), lambda b, m: (m, 0)),
                pl.BlockSpec((tm, HEAD_DIM), lambda b, m: (m, 0)),
                pl.BlockSpec((1, HEAD_DIM), lambda b, m: (0, 0))]
    ops = [src, cos, sin, gain.reshape(1, HEAD_DIM)]
    aliases = {}
    if into is not None:
        in_specs += [pl.BlockSpec(memory_space=pl.ANY)] * 2
        ops += list(into)
        aliases = {4: 0, 5: 1}
    return pl.pallas_call(
        _kvprep_kernel,
        grid=(B, n // tm),
        in_specs=in_specs,
        out_specs=[pl.BlockSpec((1, tm, KV_WIDTH), lambda b, m: (b, rb + m, 0)),
                   pl.BlockSpec((1, tm, w), lambda b, m: (b, rb + m, 0))],
        out_shape=[jax.ShapeDtypeStruct((B, Lk, KV_WIDTH), bf16),
                   jax.ShapeDtypeStruct((B, Lk, w), bf16)],
        input_output_aliases=aliases,
        compiler_params=_params(("parallel", "parallel")),
    )(*ops)


def _attn_kernel(q_ref, k_ref, v1_ref, cos_ref, sin_ref, gain_ref, gate_ref, o_ref):
    k = k_ref[0]
    v1 = v1_ref[0]
    tq = q_ref.shape[1]
    sub = min(tq, ATT_ROWS)
    units = [(slice(r * sub, (r + 1) * sub), slice(h * HEAD_DIM, (h + 1) * HEAD_DIM))
             for r in range(tq // sub) for h in range(ATT_GROUP)]

    def scores(unit):
        rows, sl = unit
        qn = _head_norm(q_ref[0, rows, sl].astype(f32), gain_ref[...])
        q = (_rope(qn, cos_ref[rows, :], sin_ref[rows, :]) * _SM_SCALE_LOG2E).astype(bf16)
        return lax.dot_general(q, k, (((1,), (1,)), ((), ())), preferred_element_type=f32)

    pending = [scores(u) for u in units[:ATT_AHEAD]]
    for i, (rows, sl) in enumerate(units):
        s = pending.pop(0)
        if i + ATT_AHEAD < len(units):
            pending.append(scores(units[i + ATT_AHEAD]))
        p = jnp.exp2(s - jnp.max(s, axis=-1, keepdims=True))
        ol = jnp.dot(p.astype(bf16), v1, preferred_element_type=f32)
        o = ol[:, :HEAD_DIM] / ol[:, HEAD_DIM:HEAD_DIM + 1]
        o_ref[0, rows, sl] = (o * _silu(gate_ref[0, rows, sl].astype(f32))).astype(bf16)


def _attention(proj, k_all, vt_all, Lk, kblk, cos, sin, gain, tq):
    B, L, _ = proj.shape
    gw = ATT_GROUP * HEAD_DIM
    qb, gb = P_Q // gw, P_ATT_GATE // gw
    return pl.pallas_call(
        _attn_kernel,
        grid=(B, ATT_KV_HEADS, L // tq),
        in_specs=[pl.BlockSpec((1, tq, gw), lambda b, h, i: (b, i, qb + h)),
                  pl.BlockSpec((1, Lk, HEAD_DIM), lambda b, h, i: (b, kblk, h)),
                  pl.BlockSpec((1, Lk, 2 * HEAD_DIM), lambda b, h, i: (b, kblk, h)),
                  pl.BlockSpec((tq, HEAD_DIM), lambda b, h, i: (i, 0)),
                  pl.BlockSpec((tq, HEAD_DIM), lambda b, h, i: (i, 0)),
                  pl.BlockSpec((1, HEAD_DIM), lambda b, h, i: (0, 0)),
                  pl.BlockSpec((1, tq, gw), lambda b, h, i: (b, i, gb + h))],
        out_specs=pl.BlockSpec((1, tq, gw), lambda b, h, i: (b, i, h)),
        out_shape=jax.ShapeDtypeStruct((B, L, ATT_WIDTH), bf16),
        compiler_params=_params(("parallel", "parallel", "parallel")),
    )(proj, k_all, vt_all, cos, sin, gain.reshape(1, HEAD_DIM), proj)


def _gmlp_kernel(p_ref, gate_ref, lng_ref, lnb_ref, ws_ref, bs_ref, o_ref):
    tm = p_ref.shape[1]
    p = p_ref[0].astype(f32)
    uv = 0.5 * p * (1.0 + lax.erf(p * (2.0 ** -0.5)))
    u = uv[:, :GM_WIDTH]
    v = uv[:, GM_WIDTH:]
    vc = v - jnp.mean(v, axis=-1, keepdims=True)
    var = jnp.mean(vc * vc, axis=-1, keepdims=True)
    vn = (vc * lax.rsqrt(var + LN_EPS) * lng_ref[...] + lnb_ref[...]).astype(bf16)
    ug = u * _silu(gate_ref[0].astype(f32))
    for n in range(tm // CHUNK):
        rows = slice(n * CHUNK, (n + 1) * CHUNK)
        for g in range(GM_GROUPS):
            cols = slice(g * GM_GROUP_DIM, (g + 1) * GM_GROUP_DIM)
            y = jnp.dot(ws_ref[g], vn[rows, cols], preferred_element_type=f32) + bs_ref[:, cols]
            o_ref[0, rows, cols] = (ug[rows, cols] * y).astype(bf16)


def _gmlp(proj, ln_g, ln_b, ws, bs_full, tm):
    B, L, _ = proj.shape
    pb, gb = P_GM // (2 * GM_WIDTH), P_GM_GATE // GM_WIDTH
    return pl.pallas_call(
        _gmlp_kernel,
        grid=(B, L // tm),
        in_specs=[pl.BlockSpec((1, tm, 2 * GM_WIDTH), lambda b, m: (b, m, pb)),
                  pl.BlockSpec((1, tm, GM_WIDTH), lambda b, m: (b, m, gb)),
                  pl.BlockSpec((1, GM_WIDTH), lambda b, m: (0, 0)),
                  pl.BlockSpec((1, GM_WIDTH), lambda b, m: (0, 0)),
                  pl.BlockSpec((GM_GROUPS, CHUNK, CHUNK), lambda b, m: (0, 0, 0)),
                  pl.BlockSpec((CHUNK, GM_WIDTH), lambda b, m: (0, 0))],
        out_specs=pl.BlockSpec((1, tm, GM_WIDTH), lambda b, m: (b, m, 0)),
        out_shape=jax.ShapeDtypeStruct((B, L, GM_WIDTH), bf16),
        compiler_params=_params(("parallel", "parallel")),
    )(proj, proj, ln_g.reshape(1, GM_WIDTH), ln_b.reshape(1, GM_WIDTH), ws, bs_full)


def _shortconv_kernel(p_ref, w_ref, b_ref, o_ref):
    L = p_ref.shape[1]
    x = p_ref[0].astype(f32)
    row = lax.broadcasted_iota(jnp.int32, x.shape, 0)
    xm = jnp.where(row == 0, 0.0, pltpu.roll(x, 1, 0))
    xp = jnp.where(row == L - 1, 0.0, pltpu.roll(x, L - 1, 0))
    o_ref[0] = (xm * w_ref[0:1, :] + x * w_ref[1:2, :] + xp * w_ref[2:3, :] + b_ref[...]).astype(bf16)


def _shortconv(proj, w, b, tc):
    B, L, _ = proj.shape
    W = w.shape[1]
    return pl.pallas_call(
        _shortconv_kernel,
        grid=(B, W // tc),
        in_specs=[pl.BlockSpec((1, L, tc), lambda b, c: (b, 0, c)),
                  pl.BlockSpec((3, tc), lambda b, c: (0, c)),
                  pl.BlockSpec((1, tc), lambda b, c: (0, c))],
        out_specs=pl.BlockSpec((1, L, tc), lambda b, c: (b, 0, c)),
        out_shape=jax.ShapeDtypeStruct((B, L, W), bf16),
        compiler_params=_params(("parallel", "parallel")),
    )(proj, w, b.reshape(1, W))


def _filter_embedding(L):
    n = np.arange(2 * L)
    pos = np.where(n < L, n, 2 * L - n)
    pos = np.where(n == L, 0, pos)
    t = np.linspace(0.0, 1.0, L)[pos]
    bands = np.linspace(1e-4, HY_BANDS - 1, HY_BANDS)
    ang = (2.0 * math.pi / L) * pos[:, None] * bands[None, :]
    z = np.zeros((2 * L, LANES), np.float64)
    z[:, 0] = t
    z[:, 1:1 + HY_BANDS] = np.cos(ang)
    z[:, 1 + HY_BANDS:HY_EMB] = -np.sin(ang)
    z[:, HY_EMB] = (n < L)
    z[:, HY_EMB + 1] = (n > L) | (n == 0)
    return jnp.asarray(z, f32)


def _filter_kernel(z_ref, w1_ref, b1_ref, w2_ref, b2_ref, freq_ref, w3_ref, dec_ref, o_ref):
    hp = lax.Precision.HIGHEST
    C = o_ref.shape[2]
    z = z_ref[...]
    t = z[:, 0:1]
    mf = z[:, HY_EMB:HY_EMB + 1]
    mb = z[:, HY_EMB + 1:HY_EMB + 2]
    h = jnp.sin(freq_ref[0:1, :] * (jnp.dot(z, w1_ref[...], precision=hp, preferred_element_type=f32) + b1_ref[...]))
    h = jnp.sin(freq_ref[1:2, :] * (jnp.dot(h, w2_ref[...], precision=hp, preferred_element_type=f32) + b2_ref[...]))
    hi = h.astype(bf16)
    lo = (h - hi.astype(f32)).astype(bf16)
    h3 = jnp.concatenate([hi, lo, hi], axis=1)
    for blk in range(2 * HY_ORDER):
        cols = slice(blk * C, (blk + 1) * C)
        taps = jnp.dot(h3, w3_ref[:, cols], preferred_element_type=f32)
        taps = (mb if blk % 2 else mf) * taps * jnp.exp(-t * jnp.abs(dec_ref[:, cols]))
        if blk % 2:
            o_ref[blk // 2] = o_ref[blk // 2] + taps
        else:
            o_ref[blk // 2] = taps


def _filters(L, w1, b1, w2, b2, w3, freq, decay, tr):
    H = HY_FILTER_HIDDEN
    C = HY_WIDTH
    W = 2 * HY_ORDER * C
    z = _filter_embedding(L)
    w1p = jnp.zeros((LANES, H), f32).at[:HY_EMB].set(w1)
    w3_hi = w3.astype(bf16)
    w3_lo = (w3 - w3_hi.astype(f32)).astype(bf16)
    w3 = jnp.concatenate([w3_hi, w3_hi, w3_lo], axis=0)
    full = lambda shape: pl.BlockSpec(shape, lambda r: (0,) * len(shape))
    return pl.pallas_call(
        _filter_kernel,
        grid=(2 * L // tr,),
        in_specs=[pl.BlockSpec((tr, LANES), lambda r: (r, 0)),
                  full((LANES, H)), full((1, H)), full((H, H)), full((1, H)), full((2, H)),
                  full((3 * H, W)), full((1, W))],
        out_specs=pl.BlockSpec((HY_ORDER, tr, C), lambda r: (0, r, 0)),
        out_shape=jax.ShapeDtypeStruct((HY_ORDER, 2 * L, C), f32),
        compiler_params=_params(("parallel",)),
    )(z, w1p, b1.reshape(1, H), w2, b2.reshape(1, H), freq, w3, decay.reshape(1, W))


def _fft_dims(L):
    n2 = SUBLANES
    while (2 * n2) * (2 * n2) <= L:
        n2 *= 2
    return n2, L // n2, 2 * (L // n2), n2 + SUBLANES


def _a_pitch(L):
    return _fft_dims(L)[2] + SUBLANES


def _stack(fr, fi):
    return np.block([[fr, -fi], [fi, fr]])


@functools.lru_cache(maxsize=None)
def _dft_tables(L):
    N2, N1h, N1, _ = _fft_dims(L)
    N = 2 * L
    k1 = np.arange(N1)
    a1 = -2.0 * np.pi * np.outer(k1, np.arange(N1)) / N1
    f1 = _stack(np.cos(a1[:, :N1h]), np.sin(a1[:, :N1h]))
    f1k = np.concatenate([np.cos(a1), np.sin(a1)], axis=0)
    f3 = _stack(np.cos(-a1[:, :N1h].T), np.sin(-a1[:, :N1h].T))
    n2 = np.arange(N2)
    a2 = -2.0 * np.pi * (np.outer(n2, n2)[None] / N2 + (k1[:, None, None] * n2[None, None, :]) / N)
    f2 = np.stack([_stack(np.cos(a), np.sin(a)) for a in a2])
    a2t = -np.transpose(a2, (0, 2, 1))
    g2 = np.stack([_stack(np.cos(a), np.sin(a)) for a in a2t])
    return tuple(jnp.asarray(m, bf16) for m in (f1, f1k, f2, g2, f3))


def _load_cols(ref, lead, rows):
    return jnp.concatenate([ref[lead + (t, rows, slice(None))] for t in range(ref.shape[-3])], axis=1)


def _store_cols(ref, lead, rows, val):
    for t in range(ref.shape[-3]):
        ref[lead + (t, rows, slice(None))] = val[:, t * LANES:(t + 1) * LANES]


def _grouped_loop(n, group, load, compute, store):
    group = min(group, n)
    assert n % group == 0

    def body(i, c):
        idx = [i * group + u for u in range(group)]
        vals = [load(ix) for ix in idx]
        outs = [compute(ix, v) for ix, v in zip(idx, vals)]
        for ix, o in zip(idx, outs):
            store(ix, o)
        return c

    lax.fori_loop(0, n // group, body, 0)


def _load_ri(ref, rows):
    return jnp.concatenate([_load_cols(ref, (ri,), rows) for ri in range(2)], axis=0).astype(bf16)


def _store_ri(ref, rows, val):
    h = val.shape[0] // 2
    for ri in range(2):
        _store_cols(ref, (ri,), rows, val[ri * h:(ri + 1) * h])


def _spectrum_kernel(k_ref, f1k_ref, f2_ref, o_ref, a_scr, *, L):
    N2, N1h, N1, P = _fft_dims(L)
    PA = _a_pitch(L)
    T = a_scr.shape[1]
    inv_n = 1.0 / (2 * L)

    def load1(n2):
        return jnp.concatenate([k_ref[0, pl.ds(n2, N1, stride=N2), t * LANES:(t + 1) * LANES]
                                for t in range(T)], axis=1).astype(bf16)

    _grouped_loop(N2, FFT_GROUP, load1,
                  lambda n2, w: jnp.dot(f1k_ref[...], w, preferred_element_type=f32),
                  lambda n2, r: _store_ri(a_scr, pl.ds(pl.multiple_of(n2 * PA, SUBLANES), N1), r))

    def store2(k1, x):
        out_rows = pl.ds(pl.multiple_of(k1 * N2, N2), N2)
        o_ref[0, 0, out_rows, :] = x[:N2].astype(bf16)
        o_ref[0, 1, out_rows, :] = x[N2:].astype(bf16)

    _grouped_loop(N1, FFT_GROUP,
                  lambda k1: _load_ri(a_scr, pl.ds(k1, N2, stride=PA)),
                  lambda k1, a: jnp.dot(f2_ref[k1], a, preferred_element_type=f32) * inv_n,
                  store2)


def _spectrum(k, tc):
    _, N, C = k.shape
    L = N // 2
    N2, N1h, N1, P = _fft_dims(L)
    _, f1k, f2, _, _ = _dft_tables(L)
    return pl.pallas_call(
        functools.partial(_spectrum_kernel, L=L),
        grid=(HY_ORDER, C // tc),
        in_specs=[pl.BlockSpec((1, N, tc), lambda o, c: (o, 0, c)),
                  pl.BlockSpec(f1k.shape, lambda o, c: (0, 0)),
                  pl.BlockSpec(f2.shape, lambda o, c: (0, 0, 0))],
        out_specs=pl.BlockSpec((1, 2, N, tc), lambda o, c: (o, 0, 0, c)),
        out_shape=jax.ShapeDtypeStruct((HY_ORDER, 2, N, C), bf16),
        scratch_shapes=[pltpu.VMEM((2, tc // LANES, N2 * _a_pitch(L), LANES), f32)],
        compiler_params=_params(("parallel", "parallel")),
    )(k, f1k, f2)


def _fftconv_kernel(*refs, L, chunk, has_gate):
    if has_gate:
        u_in, u_ep, g_ref, gate_ref, bias_ref, kf_ref, f1_ref, f2_ref, g2_ref, f3_ref, o_ref, z_scr, a_scr = refs
    else:
        u_in, u_ep, g_ref, bias_ref, kf_ref, f1_ref, f2_ref, g2_ref, f3_ref, o_ref, z_scr, a_scr = refs
    N2, N1h, N1, P = _fft_dims(L)
    J = L // chunk
    nb = chunk // N2
    j = pl.program_id(2)

    @pl.when(j < J)
    def _load():
        for ri in range(2):
            for blk in range(nb):
                rows = pl.ds(pl.multiple_of((j * nb + blk) * P, SUBLANES), N2)
                _store_cols(z_scr, (ri,), rows, u_in[ri, blk * N2:(blk + 1) * N2, :].astype(f32))

    @pl.when(j == J - 1)
    def _transform():
        PA = _a_pitch(L)

        def a_block(n2):
            return pl.ds(pl.multiple_of(n2 * PA, SUBLANES), N1)

        _grouped_loop(N2, FFT_GROUP,
                      lambda n2: _load_ri(z_scr, pl.ds(n2, N1h, stride=P)),
                      lambda n2, w: jnp.dot(f1_ref[...], w, preferred_element_type=f32),
                      lambda n2, r: _store_ri(a_scr, a_block(n2), r))

        def block_rows(k1):
            return pl.ds(k1, N2, stride=PA)

        def middle(k1, a):
            x = jnp.dot(f2_ref[k1], a, preferred_element_type=f32)
            krows = pl.ds(pl.multiple_of(k1 * N2, N2), N2)
            kr = kf_ref[0, 0, krows, :].astype(f32)
            ki = kf_ref[0, 1, krows, :].astype(f32)
            xr, xi = x[:N2], x[N2:]
            y = jnp.concatenate([xr * kr - xi * ki, xr * ki + xi * kr], axis=0).astype(bf16)
            return jnp.dot(g2_ref[k1], y, preferred_element_type=f32)

        _grouped_loop(N1, FFT_GROUP,
                      lambda k1: _load_ri(a_scr, block_rows(k1)),
                      middle,
                      lambda k1, b: _store_ri(a_scr, block_rows(k1), b))

        _grouped_loop(N2, FFT_GROUP,
                      lambda n2: _load_ri(a_scr, a_block(n2)),
                      lambda n2, b: jnp.dot(f3_ref[...], b, preferred_element_type=f32),
                      lambda n2, y: _store_ri(z_scr, pl.ds(n2, N1h, stride=P), y))

    @pl.when(j >= J)
    def _epilogue():
        for ri in range(2):
            for blk in range(nb):
                rows = pl.ds(pl.multiple_of(((j - J) * nb + blk) * P, SUBLANES), N2)
                y = _load_cols(z_scr, (ri,), rows)
                sl = slice(blk * N2, (blk + 1) * N2)
                u = u_ep[ri, sl, :].astype(f32)
                o = g_ref[ri, sl, :].astype(f32) * (y + bias_ref[...] * u)
                if has_gate:
                    o = o * _silu(gate_ref[ri, sl, :].astype(f32))
                o_ref[ri, sl, :] = o.astype(bf16)


def _fftconv(u, u_cb, g, g_cb, gate, gate_cb, bias, kf, order, tc):
    B, L, _ = u.shape
    C = HY_WIDTH
    N2, N1h, N1, P = _fft_dims(L)
    chunk = min(L, FFT_CHUNK)
    J = L // chunk
    f1, _, f2, g2, f3 = _dft_tables(L)
    T = tc // LANES
    in_map = lambda cb: (lambda c, p, j: (p, jnp.minimum(j, J - 1), cb + c))
    ep_map = lambda cb: (lambda c, p, j: (p, jnp.maximum(j - J, 0), cb + c))
    const = lambda nd: (lambda c, p, j: (0,) * nd)
    once = pl.Buffered(1)
    ops = [u, u, g]
    specs = [pl.BlockSpec((2, chunk, tc), in_map(u_cb)),
             pl.BlockSpec((2, chunk, tc), ep_map(u_cb)),
             pl.BlockSpec((2, chunk, tc), ep_map(g_cb))]
    if gate is not None:
        ops.append(gate)
        specs.append(pl.BlockSpec((2, chunk, tc), ep_map(gate_cb)))
    ops += [bias.reshape(HY_ORDER, 1, C), kf, f1, f2, g2, f3]
    specs += [pl.BlockSpec((None, 1, tc), lambda c, p, j: (order, 0, c)),
              pl.BlockSpec((1, 2, 2 * L, tc), lambda c, p, j: (order, 0, 0, c), pipeline_mode=once),
              pl.BlockSpec(f1.shape, const(2), pipeline_mode=once),
              pl.BlockSpec(f2.shape, const(3), pipeline_mode=once),
              pl.BlockSpec(g2.shape, const(3), pipeline_mode=once),
              pl.BlockSpec(f3.shape, const(2), pipeline_mode=once)]
    return pl.pallas_call(
        functools.partial(_fftconv_kernel, L=L, chunk=chunk, has_gate=gate is not None),
        grid=(C // tc, B // 2, 2 * J),
        in_specs=specs,
        out_specs=pl.BlockSpec((2, chunk, tc), ep_map(0)),
        out_shape=jax.ShapeDtypeStruct((B, L, C), bf16),
        scratch_shapes=[pltpu.VMEM((2, T, N1h * P, LANES), f32),
                        pltpu.VMEM((2, T, N2 * _a_pitch(L), LANES), f32)],
        compiler_params=_params(("parallel", "parallel", "arbitrary"), FFT_VMEM_LIMIT),
    )(*ops)


def _merge_kernel(hy_ref, at_ref, gm_ref, mg_ref, x_ref, gate_ref, gpost_ref,
                  why_ref, wat_ref, wgm_ref, wout_ref, o_ref):
    D = x_ref.shape[2]
    acc = None
    for i, (br, w) in enumerate(((hy_ref, why_ref), (at_ref, wat_ref), (gm_ref, wgm_ref))):
        y = jnp.dot(br[0], w[...], preferred_element_type=f32)
        y = jax.nn.sigmoid(mg_ref[0, :, i * D:(i + 1) * D].astype(f32)) * y
        acc = y if acc is None else acc + y
    o = jnp.dot(acc.astype(bf16), wout_ref[...], preferred_element_type=f32)
    r = o * lax.rsqrt(jnp.mean(o * o, axis=-1, keepdims=True) + RMS_EPS) * gpost_ref[...]
    o_ref[0] = x_ref[0] + gate_ref[0] * r


def _merge(hy, at, gm, proj, x, mod, g_post, w_hy, w_at, w_gm, w_out, tm):
    B, L, D = x.shape
    mb = P_MERGE // (3 * D)
    row = lambda b, m: (b, m, 0)
    wspec = pl.BlockSpec((D, D), lambda b, m: (0, 0))
    return pl.pallas_call(
        _merge_kernel,
        grid=(B, L // tm),
        in_specs=[pl.BlockSpec((1, tm, D), row), pl.BlockSpec((1, tm, D), row), pl.BlockSpec((1, tm, D), row),
                  pl.BlockSpec((1, tm, 3 * D), lambda b, m: (b, m, mb)),
                  pl.BlockSpec((1, tm, D), row),
                  pl.BlockSpec((1, 1, D), lambda b, m: (b, 0, 2)),
                  pl.BlockSpec((1, D), lambda b, m: (0, 0)),
                  wspec, wspec, wspec, wspec],
        out_specs=pl.BlockSpec((1, tm, D), row),
        out_shape=jax.ShapeDtypeStruct((B, L, D), f32),
        compiler_params=_params(("parallel", "parallel")),
    )(hy, at, gm, proj, x, mod, g_post.reshape(1, D), w_hy, w_at, w_gm, w_out)


def _rope_tables(L):
    pos = np.arange(L)
    inv = np.power(ROPE_THETA, -np.arange(0, AXIS_DIM, 2) / AXIS_DIM)
    ar = (pos // GRID_W)[:, None] * inv
    ac = (pos % GRID_W)[:, None] * inv
    cos = np.concatenate([np.cos(ar), np.cos(ar), np.cos(ac), np.cos(ac)], axis=1)
    sin = np.concatenate([-np.sin(ar), np.sin(ar), -np.sin(ac), np.sin(ac)], axis=1)
    return jnp.asarray(cos, f32), jnp.asarray(sin, f32)


def _identity_rope(L):
    return jnp.ones((L, HEAD_DIM), f32), jnp.zeros((L, HEAD_DIM), f32)


def _hyena(proj, lp):
    B, L, _ = proj.shape
    tc = FFT_COLS
    nct = HY_WIDTH // tc
    hyc = _shortconv(proj, lp['hy_conv_w'], lp['hy_conv_b'], tc)
    k = _filters(L, lp['hy_w1'], lp['hy_b1'], lp['hy_w2'], lp['hy_b2'], lp['hy_w3'], lp['hy_freq'],
                 lp['hy_decay'], min(2 * L, 256))
    kf = _spectrum(k, LANES)
    z1 = _fftconv(hyc, 2 * nct, hyc, 0, None, 0, lp['hy_bias'], kf, 0, tc)
    return _fftconv(z1, 0, hyc, nct, proj, P_HY_GATE // tc, lp['hy_bias'], kf, 1, tc)


def _mixer(proj, k_all, v1_all, Lk, kblk, rope_q, lp, tq, tm_gm):
    att = _attention(proj, k_all, v1_all, Lk, kblk, rope_q[0], rope_q[1], lp['q_gain'], tq)
    hy = _hyena(proj, lp)
    gm = _gmlp(proj, lp['gm_ln_g'], lp['gm_ln_b'], lp['gm_ws'], lp['gm_bs_full'], tm_gm)
    return hy, att, gm


def _layer(x, xc, mod_x, mod_c, rope, lp, ctx_out):
    B, L, D = x.shape
    C = xc.shape[1]
    w_in = lp['w_in']
    proj = _inproj(x, mod_x, lp['g_pre'], w_in, min(L, 1024), INPROJ_COLS)
    wc = w_in if ctx_out else w_in[:, P_K:]
    projc = _inproj(xc, mod_c, lp['g_pre'], wc, C, min(wc.shape[1], INPROJ_COLS))
    ident = _identity_rope(C)
    kv = _kvprep(proj, P_K, rope[0], rope[1], lp['k_gain'], L + C, 0, min(L, 1024))
    k_all, v1_all = _kvprep(projc, P_K if ctx_out else 0, ident[0], ident[1], lp['k_gain'], L + C, L, C, into=kv)

    hy, att, gm = _mixer(proj, k_all, v1_all, C + L, 0, rope, lp, min(L, ATT_TQ), 256)
    x_new = _merge(hy, att, gm, proj, x, mod_x, lp['g_post'], lp['w_hy_o'], lp['w_att_o'], lp['w_gm_o'],
                   lp['w_out'], min(L, 512))
    if not ctx_out:
        return x_new, xc
    hyc, attc, gmc = _mixer(projc, k_all, v1_all, C, L // C, ident, lp, C, min(C, 256))
    xc_new = _merge(hyc, attc, gmc, projc, xc, mod_c, lp['g_post'], lp['w_hy_o'], lp['w_att_o'], lp['w_gm_o'],
                    lp['w_out'], C)
    return x_new, xc_new


def kernel(x, c, ctx, c_ctx, w_mod, b_mod, g_pre, g_post, w_in, hy_conv_w, hy_conv_b, hy_w1, hy_b1, hy_w2, hy_b2, hy_w3, hy_freq, hy_decay, hy_bias, q_gain, k_gain, gm_ln_g, gm_ln_b, gm_ws, gm_bs, w_hy_o, w_att_o, w_gm_o, w_out):
    B, L, D = x.shape
    rope = _rope_tables(L)

    R = -(-(B + 1) // (2 * SUBLANES)) * (2 * SUBLANES)
    cc = jnp.zeros((R, D), f32).at[:B].set(c).at[B].set(c_ctx)
    mod = _modulation(cc, w_mod, b_mod)

    w_in_p = jnp.concatenate([w_in[:, :, a:b].astype(bf16) for a, b in _COL_RANGES], axis=2)
    gm_bs_full = jnp.repeat(jnp.swapaxes(gm_bs, 1, 2), GM_GROUP_DIM, axis=2)

    xc = ctx
    for i in range(DEPTH):
        lp = {
            'g_pre': g_pre[i], 'g_post': g_post[i], 'w_in': w_in_p[i],
            'hy_conv_w': hy_conv_w[i], 'hy_conv_b': hy_conv_b[i],
            'hy_w1': hy_w1[i], 'hy_b1': hy_b1[i], 'hy_w2': hy_w2[i], 'hy_b2': hy_b2[i],
            'hy_w3': hy_w3[i], 'hy_freq': hy_freq[i], 'hy_decay': hy_decay[i], 'hy_bias': hy_bias[i],
            'q_gain': q_gain[i], 'k_gain': k_gain[i], 'gm_ln_g': gm_ln_g[i], 'gm_ln_b': gm_ln_b[i],
            'gm_ws': gm_ws[i].astype(bf16), 'gm_bs_full': gm_bs_full[i],
            'w_hy_o': w_hy_o[i].astype(bf16), 'w_att_o': w_att_o[i].astype(bf16),
            'w_gm_o': w_gm_o[i].astype(bf16), 'w_out': w_out[i].astype(bf16),
        }
        mod_x = mod[i, :B].reshape(B, 1, 3 * D)
        mod_c = jnp.broadcast_to(mod[i, B].reshape(1, 1, 3 * D), (B, 1, 3 * D))
        x, xc = _layer(x, xc, mod_x, mod_c, rope, lp, i < DEPTH - 1)
    return x
```

```python
import functools
import math

import numpy as np
import jax
import jax.numpy as jnp
from jax import lax
from jax.experimental import pallas as pl
from jax.experimental.pallas import tpu as pltpu

f32 = jnp.float32
bf16 = jnp.bfloat16

D_MODEL = 1024
DEPTH = 2
GRID_W = 64
RMS_EPS = 1e-6
LN_EPS = 1e-5

HY_WIDTH = D_MODEL
HY_ORDER = 2
HY_BANDS = 16
HY_EMB = 1 + 2 * HY_BANDS
HY_FILTER_HIDDEN = 64

HEAD_DIM = 128
ATT_HEADS = D_MODEL // HEAD_DIM
ATT_KV_HEADS = 2
ATT_GROUP = ATT_HEADS // ATT_KV_HEADS
ATT_WIDTH = ATT_HEADS * HEAD_DIM
KV_WIDTH = ATT_KV_HEADS * HEAD_DIM
AXIS_DIM = HEAD_DIM // 2
ROPE_THETA = 10000.0

GM_WIDTH = D_MODEL
GM_GROUPS = 8
GM_GROUP_DIM = GM_WIDTH // GM_GROUPS
CHUNK = 128

OFF_HY = 0
OFF_HY_GATE = OFF_HY + (HY_ORDER + 1) * HY_WIDTH
OFF_Q = OFF_HY_GATE + HY_WIDTH
OFF_K = OFF_Q + ATT_WIDTH
OFF_V = OFF_K + KV_WIDTH
OFF_ATT_GATE = OFF_V + KV_WIDTH
OFF_GM = OFF_ATT_GATE + ATT_WIDTH
OFF_GM_GATE = OFF_GM + 2 * GM_WIDTH
OFF_MERGE = OFF_GM_GATE + GM_WIDTH
IN_WIDTH = OFF_MERGE + 3 * D_MODEL

P_HY = 0
P_HY_GATE = 3 * D_MODEL
P_Q = 4 * D_MODEL
P_ATT_GATE = 5 * D_MODEL
P_GM = 6 * D_MODEL
P_GM_GATE = 8 * D_MODEL
P_MERGE = 9 * D_MODEL
P_K = 12 * D_MODEL
P_V = P_K + KV_WIDTH
_COL_RANGES = ((OFF_HY, OFF_K), (OFF_ATT_GATE, IN_WIDTH), (OFF_K, OFF_ATT_GATE))

LANES = 128
SUBLANES = 8
FFT_COLS = 256
FFT_GROUP = 32
INPROJ_COLS = 2560
ATT_AHEAD = 1
ATT_ROWS = 256
ATT_TQ = 512
FFT_CHUNK = 1024
VMEM_LIMIT = 52 * 1024 * 1024
FFT_VMEM_LIMIT = 58 * 1024 * 1024

_SM_SCALE_LOG2E = (HEAD_DIM ** -0.5) * math.log2(math.e)


def _params(sem, vmem=VMEM_LIMIT):
    return pltpu.CompilerParams(dimension_semantics=sem, vmem_limit_bytes=vmem)


def _silu(x):
    return x * jax.nn.sigmoid(x)


def _mod_kernel(cc_ref, w_ref, b_ref, o_ref):
    s = _silu(cc_ref[...]).astype(bf16)
    o_ref[0] = jnp.dot(s, w_ref[0].astype(bf16), preferred_element_type=f32) + b_ref[0]


def _modulation(cc, w_mod, b_mod):
    R, D = cc.shape
    tn = D
    return pl.pallas_call(
        _mod_kernel,
        grid=(DEPTH, 3 * D // tn),
        in_specs=[pl.BlockSpec((R, D), lambda i, n: (0, 0)),
                  pl.BlockSpec((1, D, tn), lambda i, n: (i, 0, n)),
                  pl.BlockSpec((1, 1, tn), lambda i, n: (i, 0, n))],
        out_specs=pl.BlockSpec((1, R, tn), lambda i, n: (i, 0, n)),
        out_shape=jax.ShapeDtypeStruct((DEPTH, R, 3 * D), f32),
        compiler_params=_params(("parallel", "parallel")),
    )(cc, w_mod, b_mod.reshape(DEPTH, 1, 3 * D))


def _inproj_kernel(x_ref, mod_ref, g_ref, w_ref, o_ref, h_scr):
    D = x_ref.shape[2]

    @pl.when(pl.program_id(2) == 0)
    def _():
        x = x_ref[0]
        r = lax.rsqrt(jnp.mean(x * x, axis=-1, keepdims=True) + RMS_EPS)
        shift = mod_ref[0, :, 0:D]
        scale = mod_ref[0, :, D:2 * D]
        h_scr[...] = ((x * r) * g_ref[...] * (1.0 + scale) + shift).astype(bf16)

    o_ref[0] = jnp.dot(h_scr[...], w_ref[...], preferred_element_type=f32).astype(o_ref.dtype)


def _inproj(x, mod, g_pre, w, tm, tn):
    B, L, D = x.shape
    N = w.shape[1]
    return pl.pallas_call(
        _inproj_kernel,
        grid=(B, L // tm, N // tn),
        in_specs=[pl.BlockSpec((1, tm, D), lambda b, m, n: (b, m, 0)),
                  pl.BlockSpec((1, 1, 3 * D), lambda b, m, n: (b, 0, 0)),
                  pl.BlockSpec((1, D), lambda b, m, n: (0, 0)),
                  pl.BlockSpec((D, tn), lambda b, m, n: (0, n))],
        out_specs=pl.BlockSpec((1, tm, tn), lambda b, m, n: (b, m, n)),
        out_shape=jax.ShapeDtypeStruct((B, L, N), bf16),
        scratch_shapes=[pltpu.VMEM((tm, D), bf16)],
        compiler_params=_params(("parallel", "parallel", "arbitrary")),
    )(x, mod, g_pre.reshape(1, D), w)


def _rope(x, cos, sin):
    lane = lax.broadcasted_iota(jnp.int32, x.shape, 1)
    first = (lane & (AXIS_DIM // 2)) == 0
    rot = jnp.where(first, pltpu.roll(x, HEAD_DIM - AXIS_DIM // 2, 1), pltpu.roll(x, AXIS_DIM // 2, 1))
    return x * cos + rot * sin


def _head_norm(x, gain):
    r = lax.rsqrt(jnp.mean(x * x, axis=-1, keepdims=True) + RMS_EPS)
    return x * r * gain


def _kvprep_kernel(kv_ref, cos_ref, sin_ref, gain_ref, *rest):
    k_ref, v1_ref = rest[-2:]
    kv = kv_ref[0]
    for h in range(ATT_KV_HEADS):
        sl = slice(h * HEAD_DIM, (h + 1) * HEAD_DIM)
        kn = _head_norm(kv[:, sl].astype(f32), gain_ref[...])
        k_ref[0, :, sl] = _rope(kn, cos_ref[...], sin_ref[...]).astype(bf16)
        v = kv[:, KV_WIDTH + h * HEAD_DIM:KV_WIDTH + (h + 1) * HEAD_DIM]
        v1_ref[0, :, 2 * h * HEAD_DIM:(2 * h + 1) * HEAD_DIM] = v
        v1_ref[0, :, (2 * h + 1) * HEAD_DIM:(2 * h + 2) * HEAD_DIM] = jnp.ones_like(v)


def _kvprep(src, col_off, cos, sin, gain, Lk, row_off, tm, into=None):
    B, n, _ = src.shape
    w = 2 * KV_WIDTH
    cb, rb = col_off // w, row_off // tm
    in_specs = [pl.BlockSpec((1, tm, w), lambda b, m: (b, m, cb)),
                pl.BlockSpec((tm, HEAD_DIM), lambda b, m: (m, 0)),
                pl.BlockSpec((tm, HEAD_DIM), lambda b, m: (m, 0)),
                pl.BlockSpec((1, HEAD_DIM), lambda b, m: (0, 0))]
    ops = [src, cos, sin, gain.reshape(1, HEAD_DIM)]
    aliases = {}
    if into is not None:
        in_specs += [pl.BlockSpec(memory_space=pl.ANY)] * 2
        ops += list(into)
        aliases = {4: 0, 5: 1}
    return pl.pallas_call(
        _kvprep_kernel,
        grid=(B, n // tm),
        in_specs=in_specs,
        out_specs=[pl.BlockSpec((1, tm, KV_WIDTH), lambda b, m: (b, rb + m, 0)),
                   pl.BlockSpec((1, tm, w), lambda b, m: (b, rb + m, 0))],
        out_shape=[jax.ShapeDtypeStruct((B, Lk, KV_WIDTH), bf16),
                   jax.ShapeDtypeStruct((B, Lk, w), bf16)],
        input_output_aliases=aliases,
        compiler_params=_params(("parallel", "parallel")),
    )(*ops)


def _attn_kernel(q_ref, k_ref, v1_ref, cos_ref, sin_ref, gain_ref, gate_ref, o_ref):
    k = k_ref[0]
    v1 = v1_ref[0]
    tq = q_ref.shape[1]
    sub = min(tq, ATT_ROWS)
    units = [(slice(r * sub, (r + 1) * sub), slice(h * HEAD_DIM, (h + 1) * HEAD_DIM))
             for r in range(tq // sub) for h in range(ATT_GROUP)]

    def scores(unit):
        rows, sl = unit
        qn = _head_norm(q_ref[0, rows, sl].astype(f32), gain_ref[...])
        q = (_rope(qn, cos_ref[rows, :], sin_ref[rows, :]) * _SM_SCALE_LOG2E).astype(bf16)
        return lax.dot_general(q, k, (((1,), (1,)), ((), ())), preferred_element_type=f32)

    pending = [scores(u) for u in units[:ATT_AHEAD]]
    for i, (rows, sl) in enumerate(units):
        s = pending.pop(0)
        if i + ATT_AHEAD < len(units):
            pending.append(scores(units[i + ATT_AHEAD]))
        p = jnp.exp2(s - jnp.max(s, axis=-1, keepdims=True))
        ol = jnp.dot(p.astype(bf16), v1, preferred_element_type=f32)
        o = ol[:, :HEAD_DIM] / ol[:, HEAD_DIM:HEAD_DIM + 1]
        o_ref[0, rows, sl] = (o * _silu(gate_ref[0, rows, sl].astype(f32))).astype(bf16)


def _attention(proj, k_all, vt_all, Lk, kblk, cos, sin, gain, tq):
    B, L, _ = proj.shape
    gw = ATT_GROUP * HEAD_DIM
    qb, gb = P_Q // gw, P_ATT_GATE // gw
    return pl.pallas_call(
        _attn_kernel,
        grid=(B, ATT_KV_HEADS, L // tq),
        in_specs=[pl.BlockSpec((1, tq, gw), lambda b, h, i: (b, i, qb + h)),
                  pl.BlockSpec((1, Lk, HEAD_DIM), lambda b, h, i: (b, kblk, h)),
                  pl.BlockSpec((1, Lk, 2 * HEAD_DIM), lambda b, h, i: (b, kblk, h)),
                  pl.BlockSpec((tq, HEAD_DIM), lambda b, h, i: (i, 0)),
                  pl.BlockSpec((tq, HEAD_DIM), lambda b, h, i: (i, 0)),
                  pl.BlockSpec((1, HEAD_DIM), lambda b, h, i: (0, 0)),
                  pl.BlockSpec((1, tq, gw), lambda b, h, i: (b, i, gb + h))],
        out_specs=pl.BlockSpec((1, tq, gw), lambda b, h, i: (b, i, h)),
        out_shape=jax.ShapeDtypeStruct((B, L, ATT_WIDTH), bf16),
        compiler_params=_params(("parallel", "parallel", "parallel")),
    )(proj, k_all, vt_all, cos, sin, gain.reshape(1, HEAD_DIM), proj)


def _gmlp_kernel(p_ref, gate_ref, lng_ref, lnb_ref, ws_ref, bs_ref, o_ref):
    tm = p_ref.shape[1]
    p = p_ref[0].astype(f32)
    uv = 0.5 * p * (1.0 + lax.erf(p * (2.0 ** -0.5)))
    u = uv[:, :GM_WIDTH]
    v = uv[:, GM_WIDTH:]
    vc = v - jnp.mean(v, axis=-1, keepdims=True)
    var = jnp.mean(vc * vc, axis=-1, keepdims=True)
    vn = (vc * lax.rsqrt(var + LN_EPS) * lng_ref[...] + lnb_ref[...]).astype(bf16)
    ug = u * _silu(gate_ref[0].astype(f32))
    for n in range(tm // CHUNK):
        rows = slice(n * CHUNK, (n + 1) * CHUNK)
        for g in range(GM_GROUPS):
            cols = slice(g * GM_GROUP_DIM, (g + 1) * GM_GROUP_DIM)
            y = jnp.dot(ws_ref[g], vn[rows, cols], preferred_element_type=f32) + bs_ref[:, cols]
            o_ref[0, rows, cols] = (ug[rows, cols] * y).astype(bf16)


def _gmlp(proj, ln_g, ln_b, ws, bs_full, tm):
    B, L, _ = proj.shape
    pb, gb = P_GM // (2 * GM_WIDTH), P_GM_GATE // GM_WIDTH
    return pl.pallas_call(
        _gmlp_kernel,
        grid=(B, L // tm),
        in_specs=[pl.BlockSpec((1, tm, 2 * GM_WIDTH), lambda b, m: (b, m, pb)),
                  pl.BlockSpec((1, tm, GM_WIDTH), lambda b, m: (b, m, gb)),
                  pl.BlockSpec((1, GM_WIDTH), lambda b, m: (0, 0)),
                  pl.BlockSpec((1, GM_WIDTH), lambda b, m: (0, 0)),
                  pl.BlockSpec((GM_GROUPS, CHUNK, CHUNK), lambda b, m: (0, 0, 0)),
                  pl.BlockSpec((CHUNK, GM_WIDTH), lambda b, m: (0, 0))],
        out_specs=pl.BlockSpec((1, tm, GM_WIDTH), lambda b, m: (b, m, 0)),
        out_shape=jax.ShapeDtypeStruct((B, L, GM_WIDTH), bf16),
        compiler_params=_params(("parallel", "parallel")),
    )(proj, proj, ln_g.reshape(1, GM_WIDTH), ln_b.reshape(1, GM_WIDTH), ws, bs_full)


def _shortconv_kernel(p_ref, w_ref, b_ref, o_ref):
    L = p_ref.shape[1]
    x = p_ref[0].astype(f32)
    row = lax.broadcasted_iota(jnp.int32, x.shape, 0)
    xm = jnp.where(row == 0, 0.0, pltpu.roll(x, 1, 0))
    xp = jnp.where(row == L - 1, 0.0, pltpu.roll(x, L - 1, 0))
    o_ref[0] = (xm * w_ref[0:1, :] + x * w_ref[1:2, :] + xp * w_ref[2:3, :] + b_ref[...]).astype(bf16)


def _shortconv(proj, w, b, tc):
    B, L, _ = proj.shape
    W = w.shape[1]
    return pl.pallas_call(
        _shortconv_kernel,
        grid=(B, W // tc),
        in_specs=[pl.BlockSpec((1, L, tc), lambda b, c: (b, 0, c)),
                  pl.BlockSpec((3, tc), lambda b, c: (0, c)),
                  pl.BlockSpec((1, tc), lambda b, c: (0, c))],
        out_specs=pl.BlockSpec((1, L, tc), lambda b, c: (b, 0, c)),
        out_shape=jax.ShapeDtypeStruct((B, L, W), bf16),
        compiler_params=_params(("parallel", "parallel")),
    )(proj, w, b.reshape(1, W))


def _filter_embedding(L):
    n = np.arange(2 * L)
    pos = np.where(n < L, n, 2 * L - n)
    pos = np.where(n == L, 0, pos)
    t = np.linspace(0.0, 1.0, L)[pos]
    bands = np.linspace(1e-4, HY_BANDS - 1, HY_BANDS)
    ang = (2.0 * math.pi / L) * pos[:, None] * bands[None, :]
    z = np.zeros((2 * L, LANES), np.float64)
    z[:, 0] = t
    z[:, 1:1 + HY_BANDS] = np.cos(ang)
    z[:, 1 + HY_BANDS:HY_EMB] = -np.sin(ang)
    z[:, HY_EMB] = (n < L)
    z[:, HY_EMB + 1] = (n > L) | (n == 0)
    return jnp.asarray(z, f32)


def _filter_kernel(z_ref, w1_ref, b1_ref, w2_ref, b2_ref, freq_ref, w3_ref, dec_ref, bias_ref, o_ref):
    hp = lax.Precision.HIGHEST
    C = o_ref.shape[2]
    z = z_ref[...]
    t = z[:, 0:1]
    mf = z[:, HY_EMB:HY_EMB + 1]
    mb = z[:, HY_EMB + 1:HY_EMB + 2]
    m0 = mf * mb
    h = jnp.sin(freq_ref[0:1, :] * (jnp.dot(z, w1_ref[...], precision=hp, preferred_element_type=f32) + b1_ref[...]))
    h = jnp.sin(freq_ref[1:2, :] * (jnp.dot(h, w2_ref[...], precision=hp, preferred_element_type=f32) + b2_ref[...]))
    hi = h.astype(bf16)
    lo = (h - hi.astype(f32)).astype(bf16)
    h3 = jnp.concatenate([hi, lo, hi], axis=1)
    for blk in range(2 * HY_ORDER):
        cols = slice(blk * C, (blk + 1) * C)
        taps = jnp.dot(h3, w3_ref[:, cols], preferred_element_type=f32)
        taps = (mb if blk % 2 else mf) * taps * jnp.exp(-t * jnp.abs(dec_ref[:, cols]))
        if blk % 2:
            o_ref[blk // 2] = o_ref[blk // 2] + taps
        else:
            o_ref[blk // 2] = taps + m0 * bias_ref[blk // 2:blk // 2 + 1, :]


def _filters(L, w1, b1, w2, b2, w3, freq, decay, bias, tr):
    H = HY_FILTER_HIDDEN
    C = HY_WIDTH
    W = 2 * HY_ORDER * C
    z = _filter_embedding(L)
    w1p = jnp.zeros((LANES, H), f32).at[:HY_EMB].set(w1)
    w3_hi = w3.astype(bf16)
    w3_lo = (w3 - w3_hi.astype(f32)).astype(bf16)
    w3 = jnp.concatenate([w3_hi, w3_hi, w3_lo], axis=0)
    full = lambda shape: pl.BlockSpec(shape, lambda r: (0,) * len(shape))
    return pl.pallas_call(
        _filter_kernel,
        grid=(2 * L // tr,),
        in_specs=[pl.BlockSpec((tr, LANES), lambda r: (r, 0)),
                  full((LANES, H)), full((1, H)), full((H, H)), full((1, H)), full((2, H)),
                  full((3 * H, W)), full((1, W)), full((HY_ORDER, C))],
        out_specs=pl.BlockSpec((HY_ORDER, tr, C), lambda r: (0, r, 0)),
        out_shape=jax.ShapeDtypeStruct((HY_ORDER, 2 * L, C), f32),
        compiler_params=_params(("parallel",)),
    )(z, w1p, b1.reshape(1, H), w2, b2.reshape(1, H), freq, w3, decay.reshape(1, W), bias)


def _fft_dims(L):
    n2 = SUBLANES
    while (2 * n2) * (2 * n2) <= L:
        n2 *= 2
    return n2, L // n2, 2 * (L // n2), n2 + SUBLANES


def _a_pitch(L):
    return _fft_dims(L)[2] + SUBLANES


def _stack(fr, fi):
    return np.block([[fr, -fi], [fi, fr]])


@functools.lru_cache(maxsize=None)
def _dft_tables(L):
    N2, N1h, N1, _ = _fft_dims(L)
    N = 2 * L
    k1 = np.arange(N1)
    a1 = -2.0 * np.pi * np.outer(k1, np.arange(N1)) / N1
    f1 = _stack(np.cos(a1[:, :N1h]), np.sin(a1[:, :N1h]))
    f1k = np.concatenate([np.cos(a1), np.sin(a1)], axis=0)
    f3 = _stack(np.cos(-a1[:, :N1h].T), np.sin(-a1[:, :N1h].T))
    n2 = np.arange(N2)
    a2 = -2.0 * np.pi * (np.outer(n2, n2)[None] / N2 + (k1[:, None, None] * n2[None, None, :]) / N)
    f2 = np.stack([_stack(np.cos(a), np.sin(a)) for a in a2])
    a2t = -np.transpose(a2, (0, 2, 1))
    g2 = np.stack([_stack(np.cos(a), np.sin(a)) for a in a2t])
    return tuple(jnp.asarray(m, bf16) for m in (f1, f1k, f2, g2, f3))


def _load_cols(ref, lead, rows):
    return jnp.concatenate([ref[lead + (t, rows, slice(None))] for t in range(ref.shape[-3])], axis=1)


def _store_cols(ref, lead, rows, val):
    for t in range(ref.shape[-3]):
        ref[lead + (t, rows, slice(None))] = val[:, t * LANES:(t + 1) * LANES]


def _grouped_loop(n, group, load, compute, store):
    group = min(group, n)
    assert n % group == 0

    def body(i, c):
        idx = [i * group + u for u in range(group)]
        vals = [load(ix) for ix in idx]
        outs = [compute(ix, v) for ix, v in zip(idx, vals)]
        for ix, o in zip(idx, outs):
            store(ix, o)
        return c

    lax.fori_loop(0, n // group, body, 0)


def _load_ri(ref, rows):
    return jnp.concatenate([_load_cols(ref, (ri,), rows) for ri in range(2)], axis=0).astype(bf16)


def _store_ri(ref, rows, val):
    h = val.shape[0] // 2
    for ri in range(2):
        _store_cols(ref, (ri,), rows, val[ri * h:(ri + 1) * h])


def _spectrum_kernel(k_ref, f1k_ref, f2_ref, o_ref, a_scr, *, L):
    N2, N1h, N1, P = _fft_dims(L)
    PA = _a_pitch(L)
    T = a_scr.shape[1]
    inv_n = 1.0 / (2 * L)

    def load1(n2):
        return jnp.concatenate([k_ref[0, pl.ds(n2, N1, stride=N2), t * LANES:(t + 1) * LANES]
                                for t in range(T)], axis=1).astype(bf16)

    _grouped_loop(N2, FFT_GROUP, load1,
                  lambda n2, w: jnp.dot(f1k_ref[...], w, preferred_element_type=f32),
                  lambda n2, r: _store_ri(a_scr, pl.ds(pl.multiple_of(n2 * PA, SUBLANES), N1), r))

    def store2(k1, x):
        out_rows = pl.ds(pl.multiple_of(k1 * N2, N2), N2)
        o_ref[0, 0, out_rows, :] = x[:N2].astype(bf16)
        o_ref[0, 1, out_rows, :] = x[N2:].astype(bf16)

    _grouped_loop(N1, FFT_GROUP,
                  lambda k1: _load_ri(a_scr, pl.ds(k1, N2, stride=PA)),
                  lambda k1, a: jnp.dot(f2_ref[k1], a, preferred_element_type=f32) * inv_n,
                  store2)


def _spectrum(k, tc):
    _, N, C = k.shape
    L = N // 2
    N2, N1h, N1, P = _fft_dims(L)
    _, f1k, f2, _, _ = _dft_tables(L)
    return pl.pallas_call(
        functools.partial(_spectrum_kernel, L=L),
        grid=(HY_ORDER, C // tc),
        in_specs=[pl.BlockSpec((1, N, tc), lambda o, c: (o, 0, c)),
                  pl.BlockSpec(f1k.shape, lambda o, c: (0, 0)),
                  pl.BlockSpec(f2.shape, lambda o, c: (0, 0, 0))],
        out_specs=pl.BlockSpec((1, 2, N, tc), lambda o, c: (o, 0, 0, c)),
        out_shape=jax.ShapeDtypeStruct((HY_ORDER, 2, N, C), bf16),
        scratch_shapes=[pltpu.VMEM((2, tc // LANES, N2 * _a_pitch(L), LANES), f32)],
        compiler_params=_params(("parallel", "parallel")),
    )(k, f1k, f2)


def _fftconv_kernel(*refs, L, chunk, has_gate):
    if has_gate:
        u_in, g_ref, gate_ref, kf_ref, f1_ref, f2_ref, g2_ref, f3_ref, o_ref, z_scr, a_scr = refs
    else:
        u_in, g_ref, kf_ref, f1_ref, f2_ref, g2_ref, f3_ref, o_ref, z_scr, a_scr = refs
    N2, N1h, N1, P = _fft_dims(L)
    J = L // chunk
    nb = chunk // N2
    j = pl.program_id(2)

    @pl.when(j < J)
    def _load():
        for ri in range(2):
            for blk in range(nb):
                rows = pl.ds(pl.multiple_of((j * nb + blk) * P, SUBLANES), N2)
                _store_cols(z_scr, (ri,), rows, u_in[ri, blk * N2:(blk + 1) * N2, :].astype(f32))

    @pl.when(j == J - 1)
    def _transform():
        PA = _a_pitch(L)

        def a_block(n2):
            return pl.ds(pl.multiple_of(n2 * PA, SUBLANES), N1)

        _grouped_loop(N2, FFT_GROUP,
                      lambda n2: _load_ri(z_scr, pl.ds(n2, N1h, stride=P)),
                      lambda n2, w: jnp.dot(f1_ref[...], w, preferred_element_type=f32),
                      lambda n2, r: _store_ri(a_scr, a_block(n2), r))

        def block_rows(k1):
            return pl.ds(k1, N2, stride=PA)

        def middle(k1, a):
            x = jnp.dot(f2_ref[k1], a, preferred_element_type=f32)
            krows = pl.ds(pl.multiple_of(k1 * N2, N2), N2)
            kr = kf_ref[0, 0, krows, :].astype(f32)
            ki = kf_ref[0, 1, krows, :].astype(f32)
            xr, xi = x[:N2], x[N2:]
            y = jnp.concatenate([xr * kr - xi * ki, xr * ki + xi * kr], axis=0).astype(bf16)
            return jnp.dot(g2_ref[k1], y, preferred_element_type=f32)

        _grouped_loop(N1, FFT_GROUP,
                      lambda k1: _load_ri(a_scr, block_rows(k1)),
                      middle,
                      lambda k1, b: _store_ri(a_scr, block_rows(k1), b))

        _grouped_loop(N2, FFT_GROUP,
                      lambda n2: _load_ri(a_scr, a_block(n2)),
                      lambda n2, b: jnp.dot(f3_ref[...], b, preferred_element_type=f32),
                      lambda n2, y: _store_ri(z_scr, pl.ds(n2, N1h, stride=P), y))

    @pl.when(j >= J)
    def _epilogue():
        for ri in range(2):
            for blk in range(nb):
                rows = pl.ds(pl.multiple_of(((j - J) * nb + blk) * P, SUBLANES), N2)
                y = _load_cols(z_scr, (ri,), rows)
                sl = slice(blk * N2, (blk + 1) * N2)
                o = g_ref[ri, sl, :].astype(f32) * y
                if has_gate:
                    o = o * _silu(gate_ref[ri, sl, :].astype(f32))
                o_ref[ri, sl, :] = o.astype(bf16)


def _fftconv(u, u_cb, g, g_cb, gate, gate_cb, kf, order, tc):
    B, L, _ = u.shape
    C = HY_WIDTH
    N2, N1h, N1, P = _fft_dims(L)
    chunk = min(L, FFT_CHUNK)
    J = L // chunk
    f1, _, f2, g2, f3 = _dft_tables(L)
    T = tc // LANES
    in_map = lambda cb: (lambda c, p, j: (p, jnp.minimum(j, J - 1), cb + c))
    ep_map = lambda cb: (lambda c, p, j: (p, jnp.maximum(j - J, 0), cb + c))
    const = lambda nd: (lambda c, p, j: (0,) * nd)
    once = pl.Buffered(1)
    ops = [u, g]
    specs = [pl.BlockSpec((2, chunk, tc), in_map(u_cb)),
             pl.BlockSpec((2, chunk, tc), ep_map(g_cb))]
    if gate is not None:
        ops.append(gate)
        specs.append(pl.BlockSpec((2, chunk, tc), ep_map(gate_cb)))
    ops += [kf, f1, f2, g2, f3]
    specs += [pl.BlockSpec((1, 2, 2 * L, tc), lambda c, p, j: (order, 0, 0, c), pipeline_mode=once),
              pl.BlockSpec(f1.shape, const(2), pipeline_mode=once),
              pl.BlockSpec(f2.shape, const(3), pipeline_mode=once),
              pl.BlockSpec(g2.shape, const(3), pipeline_mode=once),
              pl.BlockSpec(f3.shape, const(2), pipeline_mode=once)]
    return pl.pallas_call(
        functools.partial(_fftconv_kernel, L=L, chunk=chunk, has_gate=gate is not None),
        grid=(C // tc, B // 2, 2 * J),
        in_specs=specs,
        out_specs=pl.BlockSpec((2, chunk, tc), ep_map(0)),
        out_shape=jax.ShapeDtypeStruct((B, L, C), bf16),
        scratch_shapes=[pltpu.VMEM((2, T, N1h * P, LANES), f32),
                        pltpu.VMEM((2, T, N2 * _a_pitch(L), LANES), f32)],
        compiler_params=_params(("parallel", "parallel", "arbitrary"), FFT_VMEM_LIMIT),
    )(*ops)


def _merge_kernel(hy_ref, at_ref, gm_ref, mg_ref, x_ref, gate_ref, gpost_ref,
                  why_ref, wat_ref, wgm_ref, wout_ref, o_ref):
    D = x_ref.shape[2]
    acc = None
    for i, (br, w) in enumerate(((hy_ref, why_ref), (at_ref, wat_ref), (gm_ref, wgm_ref))):
        y = jnp.dot(br[0], w[...], preferred_element_type=f32)
        y = jax.nn.sigmoid(mg_ref[0, :, i * D:(i + 1) * D].astype(f32)) * y
        acc = y if acc is None else acc + y
    o = jnp.dot(acc.astype(bf16), wout_ref[...], preferred_element_type=f32)
    r = o * lax.rsqrt(jnp.mean(o * o, axis=-1, keepdims=True) + RMS_EPS) * gpost_ref[...]
    o_ref[0] = x_ref[0] + gate_ref[0] * r


def _merge(hy, at, gm, proj, x, mod, g_post, w_hy, w_at, w_gm, w_out, tm):
    B, L, D = x.shape
    mb = P_MERGE // (3 * D)
    row = lambda b, m: (b, m, 0)
    wspec = pl.BlockSpec((D, D), lambda b, m: (0, 0))
    return pl.pallas_call(
        _merge_kernel,
        grid=(B, L // tm),
        in_specs=[pl.BlockSpec((1, tm, D), row), pl.BlockSpec((1, tm, D), row), pl.BlockSpec((1, tm, D), row),
                  pl.BlockSpec((1, tm, 3 * D), lambda b, m: (b, m, mb)),
                  pl.BlockSpec((1, tm, D), row),
                  pl.BlockSpec((1, 1, D), lambda b, m: (b, 0, 2)),
                  pl.BlockSpec((1, D), lambda b, m: (0, 0)),
                  wspec, wspec, wspec, wspec],
        out_specs=pl.BlockSpec((1, tm, D), row),
        out_shape=jax.ShapeDtypeStruct((B, L, D), f32),
        compiler_params=_params(("parallel", "parallel")),
    )(hy, at, gm, proj, x, mod, g_post.reshape(1, D), w_hy, w_at, w_gm, w_out)


def _rope_tables(L):
    pos = np.arange(L)
    inv = np.power(ROPE_THETA, -np.arange(0, AXIS_DIM, 2) / AXIS_DIM)
    ar = (pos // GRID_W)[:, None] * inv
    ac = (pos % GRID_W)[:, None] * inv
    cos = np.concatenate([np.cos(ar), np.cos(ar), np.cos(ac), np.cos(ac)], axis=1)
    sin = np.concatenate([-np.sin(ar), np.sin(ar), -np.sin(ac), np.sin(ac)], axis=1)
    return jnp.asarray(cos, f32), jnp.asarray(sin, f32)


def _identity_rope(L):
    return jnp.ones((L, HEAD_DIM), f32), jnp.zeros((L, HEAD_DIM), f32)


def _hyena(proj, lp):
    B, L, _ = proj.shape
    tc = FFT_COLS
    nct = HY_WIDTH // tc
    hyc = _shortconv(proj, lp['hy_conv_w'], lp['hy_conv_b'], tc)
    k = _filters(L, lp['hy_w1'], lp['hy_b1'], lp['hy_w2'], lp['hy_b2'], lp['hy_w3'], lp['hy_freq'],
                 lp['hy_decay'], lp['hy_bias'], min(2 * L, 256))
    kf = _spectrum(k, LANES)
    z1 = _fftconv(hyc, 2 * nct, hyc, 0, None, 0, kf, 0, tc)
    return _fftconv(z1, 0, hyc, nct, proj, P_HY_GATE // tc, kf, 1, tc)


def _mixer(proj, k_all, v1_all, Lk, kblk, rope_q, lp, tq, tm_gm):
    att = _attention(proj, k_all, v1_all, Lk, kblk, rope_q[0], rope_q[1], lp['q_gain'], tq)
    hy = _hyena(proj, lp)
    gm = _gmlp(proj, lp['gm_ln_g'], lp['gm_ln_b'], lp['gm_ws'], lp['gm_bs_full'], tm_gm)
    return hy, att, gm


def _layer(x, xc, mod_x, mod_c, rope, lp, ctx_out):
    B, L, D = x.shape
    C = xc.shape[1]
    w_in = lp['w_in']
    proj = _inproj(x, mod_x, lp['g_pre'], w_in, min(L, 1024), INPROJ_COLS)
    wc = w_in if ctx_out else w_in[:, P_K:]
    projc = _inproj(xc.reshape(1, B * C, D), mod_c[:1], lp['g_pre'], wc, min(B * C, 1024),
                    min(wc.shape[1], INPROJ_COLS)).reshape(B, C, wc.shape[1])
    ident = _identity_rope(C)
    kv = _kvprep(proj, P_K, rope[0], rope[1], lp['k_gain'], L + C, 0, min(L, 1024))
    k_all, v1_all = _kvprep(projc, P_K if ctx_out else 0, ident[0], ident[1], lp['k_gain'], L + C, L, C, into=kv)

    hy, att, gm = _mixer(proj, k_all, v1_all, C + L, 0, rope, lp, min(L, ATT_TQ), 256)
    x_new = _merge(hy, att, gm, proj, x, mod_x, lp['g_post'], lp['w_hy_o'], lp['w_att_o'], lp['w_gm_o'],
                   lp['w_out'], min(L, 512))
    if not ctx_out:
        return x_new, xc
    hyc, attc, gmc = _mixer(projc, k_all, v1_all, C, L // C, ident, lp, C, min(C, 256))
    xc_new = _merge(hyc, attc, gmc, projc, xc, mod_c, lp['g_post'], lp['w_hy_o'], lp['w_att_o'], lp['w_gm_o'],
                    lp['w_out'], C)
    return x_new, xc_new


def kernel(x, c, ctx, c_ctx, w_mod, b_mod, g_pre, g_post, w_in, hy_conv_w, hy_conv_b, hy_w1, hy_b1, hy_w2, hy_b2, hy_w3, hy_freq, hy_decay, hy_bias, q_gain, k_gain, gm_ln_g, gm_ln_b, gm_ws, gm_bs, w_hy_o, w_att_o, w_gm_o, w_out):
    B, L, D = x.shape
    rope = _rope_tables(L)

    R = -(-(B + 1) // (2 * SUBLANES)) * (2 * SUBLANES)
    cc = jnp.zeros((R, D), f32).at[:B].set(c).at[B].set(c_ctx)
    mod = _modulation(cc, w_mod, b_mod)

    w_in_p = jnp.concatenate([w_in[:, :, a:b].astype(bf16) for a, b in _COL_RANGES], axis=2)
    gm_bs_full = jnp.repeat(jnp.swapaxes(gm_bs, 1, 2), GM_GROUP_DIM, axis=2)

    xc = ctx
    for i in range(DEPTH):
        lp = {
            'g_pre': g_pre[i], 'g_post': g_post[i], 'w_in': w_in_p[i],
            'hy_conv_w': hy_conv_w[i], 'hy_conv_b': hy_conv_b[i],
            'hy_w1': hy_w1[i], 'hy_b1': hy_b1[i], 'hy_w2': hy_w2[i], 'hy_b2': hy_b2[i],
            'hy_w3': hy_w3[i], 'hy_freq': hy_freq[i], 'hy_decay': hy_decay[i], 'hy_bias': hy_bias[i],
            'q_gain': q_gain[i], 'k_gain': k_gain[i], 'gm_ln_g': gm_ln_g[i], 'gm_ln_b': gm_ln_b[i],
            'gm_ws': gm_ws[i].astype(bf16), 'gm_bs_full': gm_bs_full[i],
            'w_hy_o': w_hy_o[i].astype(bf16), 'w_att_o': w_att_o[i].astype(bf16),
            'w_gm_o': w_gm_o[i].astype(bf16), 'w_out': w_out[i].astype(bf16),
        }
        mod_x = mod[i, :B].reshape(B, 1, 3 * D)
        mod_c = jnp.broadcast_to(mod[i, B].reshape(1, 1, 3 * D), (B, 1, 3 * D))
        x, xc = _layer(x, xc, mod_x, mod_c, rope, lp, i < DEPTH - 1)
    return x
```

```python
import functools
import math

import numpy as np
import jax
import jax.numpy as jnp
from jax import lax
from jax.experimental import pallas as pl
from jax.experimental.pallas import tpu as pltpu

f32 = jnp.float32
bf16 = jnp.bfloat16

D_MODEL = 1024
DEPTH = 2
GRID_W = 64
RMS_EPS = 1e-6
LN_EPS = 1e-5

HY_WIDTH = D_MODEL
HY_ORDER = 2
HY_BANDS = 16
HY_EMB = 1 + 2 * HY_BANDS
HY_FILTER_HIDDEN = 64

HEAD_DIM = 128
ATT_HEADS = D_MODEL // HEAD_DIM
ATT_KV_HEADS = 2
ATT_GROUP = ATT_HEADS // ATT_KV_HEADS
ATT_WIDTH = ATT_HEADS * HEAD_DIM
KV_WIDTH = ATT_KV_HEADS * HEAD_DIM
AXIS_DIM = HEAD_DIM // 2
ROPE_THETA = 10000.0

GM_WIDTH = D_MODEL
GM_GROUPS = 8
GM_GROUP_DIM = GM_WIDTH // GM_GROUPS
CHUNK = 128

OFF_HY = 0
OFF_HY_GATE = OFF_HY + (HY_ORDER + 1) * HY_WIDTH
OFF_Q = OFF_HY_GATE + HY_WIDTH
OFF_K = OFF_Q + ATT_WIDTH
OFF_V = OFF_K + KV_WIDTH
OFF_ATT_GATE = OFF_V + KV_WIDTH
OFF_GM = OFF_ATT_GATE + ATT_WIDTH
OFF_GM_GATE = OFF_GM + 2 * GM_WIDTH
OFF_MERGE = OFF_GM_GATE + GM_WIDTH
IN_WIDTH = OFF_MERGE + 3 * D_MODEL

P_HY = 0
P_HY_GATE = 3 * D_MODEL
P_Q = 4 * D_MODEL
P_ATT_GATE = 5 * D_MODEL
P_GM = 6 * D_MODEL
P_GM_GATE = 8 * D_MODEL
P_MERGE = 9 * D_MODEL
P_K = 12 * D_MODEL
P_V = P_K + KV_WIDTH
_COL_RANGES = ((OFF_HY, OFF_K), (OFF_ATT_GATE, IN_WIDTH), (OFF_K, OFF_ATT_GATE))

LANES = 128
SUBLANES = 8
FFT_COLS = 256
FFT_GROUP = 32
INPROJ_COLS = 2560
ATT_AHEAD = 1
ATT_ROWS = 256
ATT_TQ = 512
MERGE_ROWS = 256
FFT_CHUNK = 1024
VMEM_LIMIT = 52 * 1024 * 1024
FFT_VMEM_LIMIT = 58 * 1024 * 1024

_SM_SCALE_LOG2E = (HEAD_DIM ** -0.5) * math.log2(math.e)


def _params(sem, vmem=VMEM_LIMIT):
    return pltpu.CompilerParams(dimension_semantics=sem, vmem_limit_bytes=vmem)


def _silu(x):
    return x * jax.nn.sigmoid(x)


def _mod_kernel(cc_ref, w_ref, b_ref, o_ref):
    s = _silu(cc_ref[...]).astype(bf16)
    o_ref[0] = jnp.dot(s, w_ref[0].astype(bf16), preferred_element_type=f32) + b_ref[0]


def _modulation(cc, w_mod, b_mod):
    R, D = cc.shape
    tn = D
    return pl.pallas_call(
        _mod_kernel,
        grid=(DEPTH, 3 * D // tn),
        in_specs=[pl.BlockSpec((R, D), lambda i, n: (0, 0)),
                  pl.BlockSpec((1, D, tn), lambda i, n: (i, 0, n)),
                  pl.BlockSpec((1, 1, tn), lambda i, n: (i, 0, n))],
        out_specs=pl.BlockSpec((1, R, tn), lambda i, n: (i, 0, n)),
        out_shape=jax.ShapeDtypeStruct((DEPTH, R, 3 * D), f32),
        compiler_params=_params(("parallel", "parallel")),
    )(cc, w_mod, b_mod.reshape(DEPTH, 1, 3 * D))


def _inproj_kernel(x_ref, mod_ref, g_ref, w_ref, o_ref, h_scr):
    D = x_ref.shape[2]

    @pl.when(pl.program_id(2) == 0)
    def _():
        x = x_ref[0]
        r = lax.rsqrt(jnp.mean(x * x, axis=-1, keepdims=True) + RMS_EPS)
        shift = mod_ref[0, :, 0:D]
        scale = mod_ref[0, :, D:2 * D]
        h_scr[...] = ((x * r) * g_ref[...] * (1.0 + scale) + shift).astype(bf16)

    o_ref[0] = jnp.dot(h_scr[...], w_ref[...], preferred_element_type=f32).astype(o_ref.dtype)


def _inproj(x, mod, g_pre, w, tm, tn):
    B, L, D = x.shape
    N = w.shape[1]
    return pl.pallas_call(
        _inproj_kernel,
        grid=(B, L // tm, N // tn),
        in_specs=[pl.BlockSpec((1, tm, D), lambda b, m, n: (b, m, 0)),
                  pl.BlockSpec((1, 1, 3 * D), lambda b, m, n: (b, 0, 0)),
                  pl.BlockSpec((1, D), lambda b, m, n: (0, 0)),
                  pl.BlockSpec((D, tn), lambda b, m, n: (0, n))],
        out_specs=pl.BlockSpec((1, tm, tn), lambda b, m, n: (b, m, n)),
        out_shape=jax.ShapeDtypeStruct((B, L, N), bf16),
        scratch_shapes=[pltpu.VMEM((tm, D), bf16)],
        compiler_params=_params(("parallel", "parallel", "arbitrary")),
    )(x, mod, g_pre.reshape(1, D), w)


def _rope(x, cos, sin):
    lane = lax.broadcasted_iota(jnp.int32, x.shape, 1)
    first = (lane & (AXIS_DIM // 2)) == 0
    rot = jnp.where(first, pltpu.roll(x, HEAD_DIM - AXIS_DIM // 2, 1), pltpu.roll(x, AXIS_DIM // 2, 1))
    return x * cos + rot * sin


def _head_norm(x, gain):
    r = lax.rsqrt(jnp.mean(x * x, axis=-1, keepdims=True) + RMS_EPS)
    return x * r * gain


def _kvprep_kernel(kv_ref, cos_ref, sin_ref, gain_ref, *rest):
    k_ref, v1_ref = rest[-2:]
    kv = kv_ref[0]
    for h in range(ATT_KV_HEADS):
        sl = slice(h * HEAD_DIM, (h + 1) * HEAD_DIM)
        kn = _head_norm(kv[:, sl].astype(f32), gain_ref[...])
        k_ref[0, :, sl] = _rope(kn, cos_ref[...], sin_ref[...]).astype(bf16)
        v = kv[:, KV_WIDTH + h * HEAD_DIM:KV_WIDTH + (h + 1) * HEAD_DIM]
        v1_ref[0, :, 2 * h * HEAD_DIM:(2 * h + 1) * HEAD_DIM] = v
        v1_ref[0, :, (2 * h + 1) * HEAD_DIM:(2 * h + 2) * HEAD_DIM] = jnp.ones_like(v)


def _kvprep(src, col_off, cos, sin, gain, Lk, row_off, tm, into=None):
    B, n, _ = src.shape
    w = 2 * KV_WIDTH
    cb, rb = col_off // w, row_off // tm
    in_specs = [pl.BlockSpec((1, tm, w), lambda b, m: (b, m, cb)),
                pl.BlockSpec((tm, HEAD_DIM), lambda b, m: (m, 0)),
                pl.BlockSpec((tm, HEAD_DIM), lambda b, m: (m, 0)),
                pl.BlockSpec((1, HEAD_DIM), lambda b, m: (0, 0))]
    ops = [src, cos, sin, gain.reshape(1, HEAD_DIM)]
    aliases = {}
    if into is not None:
        in_specs += [pl.BlockSpec(memory_space=pl.ANY)] * 2
        ops += list(into)
        aliases = {4: 0, 5: 1}
    return pl.pallas_call(
        _kvprep_kernel,
        grid=(B, n // tm),
        in_specs=in_specs,
        out_specs=[pl.BlockSpec((1, tm, KV_WIDTH), lambda b, m: (b, rb + m, 0)),
                   pl.BlockSpec((1, tm, w), lambda b, m: (b, rb + m, 0))],
        out_shape=[jax.ShapeDtypeStruct((B, Lk, KV_WIDTH), bf16),
                   jax.ShapeDtypeStruct((B, Lk, w), bf16)],
        input_output_aliases=aliases,
        compiler_params=_params(("parallel", "parallel")),
    )(*ops)


def _attn_kernel(q_ref, k_ref, v1_ref, cos_ref, sin_ref, gain_ref, gate_ref, o_ref):
    k = k_ref[0]
    v1 = v1_ref[0]
    tq = q_ref.shape[1]
    sub = min(tq, ATT_ROWS)
    units = [(slice(r * sub, (r + 1) * sub), slice(h * HEAD_DIM, (h + 1) * HEAD_DIM))
             for r in range(tq // sub) for h in range(ATT_GROUP)]

    def scores(unit):
        rows, sl = unit
        qn = _head_norm(q_ref[0, rows, sl].astype(f32), gain_ref[...])
        q = (_rope(qn, cos_ref[rows, :], sin_ref[rows, :]) * _SM_SCALE_LOG2E).astype(bf16)
        return lax.dot_general(q, k, (((1,), (1,)), ((), ())), preferred_element_type=f32)

    pending = [scores(u) for u in units[:ATT_AHEAD]]
    for i, (rows, sl) in enumerate(units):
        s = pending.pop(0)
        if i + ATT_AHEAD < len(units):
            pending.append(scores(units[i + ATT_AHEAD]))
        p = jnp.exp2(s - jnp.max(s, axis=-1, keepdims=True))
        ol = jnp.dot(p.astype(bf16), v1, preferred_element_type=f32)
        o = ol[:, :HEAD_DIM] / ol[:, HEAD_DIM:HEAD_DIM + 1]
        o_ref[0, rows, sl] = (o * _silu(gate_ref[0, rows, sl].astype(f32))).astype(bf16)


def _attention(proj, k_all, vt_all, Lk, kblk, cos, sin, gain, tq):
    B, L, _ = proj.shape
    gw = ATT_GROUP * HEAD_DIM
    qb, gb = P_Q // gw, P_ATT_GATE // gw
    return pl.pallas_call(
        _attn_kernel,
        grid=(B, ATT_KV_HEADS, L // tq),
        in_specs=[pl.BlockSpec((1, tq, gw), lambda b, h, i: (b, i, qb + h)),
                  pl.BlockSpec((1, Lk, HEAD_DIM), lambda b, h, i: (b, kblk, h)),
                  pl.BlockSpec((1, Lk, 2 * HEAD_DIM), lambda b, h, i: (b, kblk, h)),
                  pl.BlockSpec((tq, HEAD_DIM), lambda b, h, i: (i, 0)),
                  pl.BlockSpec((tq, HEAD_DIM), lambda b, h, i: (i, 0)),
                  pl.BlockSpec((1, HEAD_DIM), lambda b, h, i: (0, 0)),
                  pl.BlockSpec((1, tq, gw), lambda b, h, i: (b, i, gb + h))],
        out_specs=pl.BlockSpec((1, tq, gw), lambda b, h, i: (b, i, h)),
        out_shape=jax.ShapeDtypeStruct((B, L, ATT_WIDTH), bf16),
        compiler_params=_params(("parallel", "parallel", "parallel")),
    )(proj, k_all, vt_all, cos, sin, gain.reshape(1, HEAD_DIM), proj)


def _gmlp_tile(p, gate, lng, lnb, ws_ref, bs_ref):
    uv = 0.5 * p * (1.0 + lax.erf(p * (2.0 ** -0.5)))
    u = uv[:, :GM_WIDTH]
    v = uv[:, GM_WIDTH:]
    vc = v - jnp.mean(v, axis=-1, keepdims=True)
    var = jnp.mean(vc * vc, axis=-1, keepdims=True)
    vn = (vc * lax.rsqrt(var + LN_EPS) * lng + lnb).astype(bf16)
    ug = u * _silu(gate)
    chunks = []
    for n in range(p.shape[0] // CHUNK):
        rows = slice(n * CHUNK, (n + 1) * CHUNK)
        groups = []
        for g in range(GM_GROUPS):
            cols = slice(g * GM_GROUP_DIM, (g + 1) * GM_GROUP_DIM)
            y = jnp.dot(ws_ref[g], vn[rows, cols], preferred_element_type=f32) + bs_ref[:, cols]
            groups.append((ug[rows, cols] * y).astype(bf16))
        chunks.append(jnp.concatenate(groups, axis=1))
    return jnp.concatenate(chunks, axis=0)


def _shortconv_kernel(p_ref, w_ref, b_ref, o_ref, s_scr):
    L = p_ref.shape[1]
    lo, hi = SUBLANES, SUBLANES + L
    zero = jnp.zeros((SUBLANES, LANES), f32)
    for t in range(s_scr.shape[0]):
        cols = slice(t * LANES, (t + 1) * LANES)
        s_scr[t, 0:lo, :] = zero
        s_scr[t, hi:hi + SUBLANES, :] = zero
        s_scr[t, lo:hi, :] = p_ref[0, :, cols].astype(f32)
        y = (s_scr[t, lo - 1:hi - 1, :] * w_ref[0:1, cols] + s_scr[t, lo:hi, :] * w_ref[1:2, cols]
             + s_scr[t, lo + 1:hi + 1, :] * w_ref[2:3, cols] + b_ref[:, cols])
        o_ref[0, :, cols] = y.astype(bf16)


def _shortconv(proj, w, b, tc):
    B, L, _ = proj.shape
    W = w.shape[1]
    return pl.pallas_call(
        _shortconv_kernel,
        grid=(B, W // tc),
        in_specs=[pl.BlockSpec((1, L, tc), lambda b, c: (b, 0, c)),
                  pl.BlockSpec((3, tc), lambda b, c: (0, c)),
                  pl.BlockSpec((1, tc), lambda b, c: (0, c))],
        out_specs=pl.BlockSpec((1, L, tc), lambda b, c: (b, 0, c)),
        out_shape=jax.ShapeDtypeStruct((B, L, W), bf16),
        scratch_shapes=[pltpu.VMEM((tc // LANES, L + 2 * SUBLANES, LANES), f32)],
        compiler_params=_params(("parallel", "parallel")),
    )(proj, w, b.reshape(1, W))


def _filter_embedding(L):
    n = np.arange(2 * L)
    pos = np.where(n < L, n, 2 * L - n)
    pos = np.where(n == L, 0, pos)
    t = np.linspace(0.0, 1.0, L)[pos]
    bands = np.linspace(1e-4, HY_BANDS - 1, HY_BANDS)
    ang = (2.0 * math.pi / L) * pos[:, None] * bands[None, :]
    z = np.zeros((2 * L, LANES), np.float64)
    z[:, 0] = t
    z[:, 1:1 + HY_BANDS] = np.cos(ang)
    z[:, 1 + HY_BANDS:HY_EMB] = -np.sin(ang)
    z[:, HY_EMB] = (n < L)
    z[:, HY_EMB + 1] = (n > L) | (n == 0)
    return jnp.asarray(z, f32)


def _filter_kernel(z_ref, w1_ref, b1_ref, w2_ref, b2_ref, freq_ref, w3_ref, dec_ref, bias_ref, o_ref):
    hp = lax.Precision.HIGHEST
    C = o_ref.shape[2]
    z = z_ref[...]
    t = z[:, 0:1]
    mf = z[:, HY_EMB:HY_EMB + 1]
    mb = z[:, HY_EMB + 1:HY_EMB + 2]
    m0 = mf * mb
    h = jnp.sin(freq_ref[0:1, :] * (jnp.dot(z, w1_ref[...], precision=hp, preferred_element_type=f32) + b1_ref[...]))
    h = jnp.sin(freq_ref[1:2, :] * (jnp.dot(h, w2_ref[...], precision=hp, preferred_element_type=f32) + b2_ref[...]))
    hi = h.astype(bf16)
    lo = (h - hi.astype(f32)).astype(bf16)
    h3 = jnp.concatenate([hi, lo, hi], axis=1)
    for blk in range(2 * HY_ORDER):
        cols = slice(blk * C, (blk + 1) * C)
        taps = jnp.dot(h3, w3_ref[:, cols], preferred_element_type=f32)
        taps = (mb if blk % 2 else mf) * taps * jnp.exp(-t * jnp.abs(dec_ref[:, cols]))
        if blk % 2:
            o_ref[blk // 2] = o_ref[blk // 2] + taps
        else:
            o_ref[blk // 2] = taps + m0 * bias_ref[blk // 2:blk // 2 + 1, :]


def _filters(L, w1, b1, w2, b2, w3, freq, decay, bias, tr):
    H = HY_FILTER_HIDDEN
    C = HY_WIDTH
    W = 2 * HY_ORDER * C
    z = _filter_embedding(L)
    w1p = jnp.zeros((LANES, H), f32).at[:HY_EMB].set(w1)
    w3_hi = w3.astype(bf16)
    w3_lo = (w3 - w3_hi.astype(f32)).astype(bf16)
    w3 = jnp.concatenate([w3_hi, w3_hi, w3_lo], axis=0)
    full = lambda shape: pl.BlockSpec(shape, lambda r: (0,) * len(shape))
    return pl.pallas_call(
        _filter_kernel,
        grid=(2 * L // tr,),
        in_specs=[pl.BlockSpec((tr, LANES), lambda r: (r, 0)),
                  full((LANES, H)), full((1, H)), full((H, H)), full((1, H)), full((2, H)),
                  full((3 * H, W)), full((1, W)), full((HY_ORDER, C))],
        out_specs=pl.BlockSpec((HY_ORDER, tr, C), lambda r: (0, r, 0)),
        out_shape=jax.ShapeDtypeStruct((HY_ORDER, 2 * L, C), f32),
        compiler_params=_params(("parallel",)),
    )(z, w1p, b1.reshape(1, H), w2, b2.reshape(1, H), freq, w3, decay.reshape(1, W), bias)


def _fft_dims(L):
    n2 = SUBLANES
    while (2 * n2) * (2 * n2) <= L:
        n2 *= 2
    return n2, L // n2, 2 * (L // n2), n2 + SUBLANES


def _a_pitch(L):
    return _fft_dims(L)[2] + SUBLANES


def _stack(fr, fi):
    return np.block([[fr, -fi], [fi, fr]])


@functools.lru_cache(maxsize=None)
def _dft_tables(L):
    N2, N1h, N1, _ = _fft_dims(L)
    N = 2 * L
    k1 = np.arange(N1)
    a1 = -2.0 * np.pi * np.outer(k1, np.arange(N1)) / N1
    f1 = _stack(np.cos(a1[:, :N1h]), np.sin(a1[:, :N1h]))
    f1k = np.concatenate([np.cos(a1), np.sin(a1)], axis=0)
    f3 = _stack(np.cos(-a1[:, :N1h].T), np.sin(-a1[:, :N1h].T))
    n2 = np.arange(N2)
    a2 = -2.0 * np.pi * (np.outer(n2, n2)[None] / N2 + (k1[:, None, None] * n2[None, None, :]) / N)
    f2 = np.stack([_stack(np.cos(a), np.sin(a)) for a in a2])
    a2t = -np.transpose(a2, (0, 2, 1))
    g2 = np.stack([_stack(np.cos(a), np.sin(a)) for a in a2t])
    return tuple(jnp.asarray(m, bf16) for m in (f1, f1k, f2, g2, f3))


def _load_cols(ref, lead, rows):
    return jnp.concatenate([ref[lead + (t, rows, slice(None))] for t in range(ref.shape[-3])], axis=1)


def _store_cols(ref, lead, rows, val):
    for t in range(ref.shape[-3]):
        ref[lead + (t, rows, slice(None))] = val[:, t * LANES:(t + 1) * LANES]


def _grouped_loop(n, group, load, compute, store):
    group = min(group, n)
    assert n % group == 0

    def body(i, c):
        idx = [i * group + u for u in range(group)]
        vals = [load(ix) for ix in idx]
        outs = [compute(ix, v) for ix, v in zip(idx, vals)]
        for ix, o in zip(idx, outs):
            store(ix, o)
        return c

    lax.fori_loop(0, n // group, body, 0)


def _load_ri(ref, rows):
    return jnp.concatenate([_load_cols(ref, (ri,), rows) for ri in range(2)], axis=0).astype(bf16)


def _store_ri(ref, rows, val):
    h = val.shape[0] // 2
    for ri in range(2):
        _store_cols(ref, (ri,), rows, val[ri * h:(ri + 1) * h])


def _spectrum_kernel(k_ref, f1k_ref, f2_ref, o_ref, a_scr, *, L):
    N2, N1h, N1, P = _fft_dims(L)
    PA = _a_pitch(L)
    T = a_scr.shape[1]
    inv_n = 1.0 / (2 * L)

    def load1(n2):
        return jnp.concatenate([k_ref[0, pl.ds(n2, N1, stride=N2), t * LANES:(t + 1) * LANES]
                                for t in range(T)], axis=1).astype(bf16)

    _grouped_loop(N2, FFT_GROUP, load1,
                  lambda n2, w: jnp.dot(f1k_ref[...], w, preferred_element_type=f32),
                  lambda n2, r: _store_ri(a_scr, pl.ds(pl.multiple_of(n2 * PA, SUBLANES), N1), r))

    def store2(k1, x):
        out_rows = pl.ds(pl.multiple_of(k1 * N2, N2), N2)
        o_ref[0, 0, out_rows, :] = x[:N2].astype(bf16)
        o_ref[0, 1, out_rows, :] = x[N2:].astype(bf16)

    _grouped_loop(N1, FFT_GROUP,
                  lambda k1: _load_ri(a_scr, pl.ds(k1, N2, stride=PA)),
                  lambda k1, a: jnp.dot(f2_ref[k1], a, preferred_element_type=f32) * inv_n,
                  store2)


def _spectrum(k, tc):
    _, N, C = k.shape
    L = N // 2
    N2, N1h, N1, P = _fft_dims(L)
    _, f1k, f2, _, _ = _dft_tables(L)
    return pl.pallas_call(
        functools.partial(_spectrum_kernel, L=L),
        grid=(HY_ORDER, C // tc),
        in_specs=[pl.BlockSpec((1, N, tc), lambda o, c: (o, 0, c)),
                  pl.BlockSpec(f1k.shape, lambda o, c: (0, 0)),
                  pl.BlockSpec(f2.shape, lambda o, c: (0, 0, 0))],
        out_specs=pl.BlockSpec((1, 2, N, tc), lambda o, c: (o, 0, 0, c)),
        out_shape=jax.ShapeDtypeStruct((HY_ORDER, 2, N, C), bf16),
        scratch_shapes=[pltpu.VMEM((2, tc // LANES, N2 * _a_pitch(L), LANES), f32)],
        compiler_params=_params(("parallel", "parallel")),
    )(k, f1k, f2)


def _fftconv_kernel(*refs, L, chunk, has_gate):
    if has_gate:
        u_in, g_ref, gate_ref, kf_ref, f1_ref, f2_ref, g2_ref, f3_ref, o_ref, z_scr, a_scr = refs
    else:
        u_in, g_ref, kf_ref, f1_ref, f2_ref, g2_ref, f3_ref, o_ref, z_scr, a_scr = refs
    N2, N1h, N1, P = _fft_dims(L)
    J = L // chunk
    nb = chunk // N2
    j = pl.program_id(2)

    @pl.when(j < J)
    def _load():
        for ri in range(2):
            for blk in range(nb):
                rows = pl.ds(pl.multiple_of((j * nb + blk) * P, SUBLANES), N2)
                _store_cols(z_scr, (ri,), rows, u_in[ri, blk * N2:(blk + 1) * N2, :].astype(f32))

    @pl.when(j == J - 1)
    def _transform():
        PA = _a_pitch(L)

        def a_block(n2):
            return pl.ds(pl.multiple_of(n2 * PA, SUBLANES), N1)

        _grouped_loop(N2, FFT_GROUP,
                      lambda n2: _load_ri(z_scr, pl.ds(n2, N1h, stride=P)),
                      lambda n2, w: jnp.dot(f1_ref[...], w, preferred_element_type=f32),
                      lambda n2, r: _store_ri(a_scr, a_block(n2), r))

        def block_rows(k1):
            return pl.ds(k1, N2, stride=PA)

        def middle(k1, a):
            x = jnp.dot(f2_ref[k1], a, preferred_element_type=f32)
            krows = pl.ds(pl.multiple_of(k1 * N2, N2), N2)
            kr = kf_ref[0, 0, krows, :].astype(f32)
            ki = kf_ref[0, 1, krows, :].astype(f32)
            xr, xi = x[:N2], x[N2:]
            y = jnp.concatenate([xr * kr - xi * ki, xr * ki + xi * kr], axis=0).astype(bf16)
            return jnp.dot(g2_ref[k1], y, preferred_element_type=f32)

        _grouped_loop(N1, FFT_GROUP,
                      lambda k1: _load_ri(a_scr, block_rows(k1)),
                      middle,
                      lambda k1, b: _store_ri(a_scr, block_rows(k1), b))

        _grouped_loop(N2, FFT_GROUP,
                      lambda n2: _load_ri(a_scr, a_block(n2)),
                      lambda n2, b: jnp.dot(f3_ref[...], b, preferred_element_type=f32),
                      lambda n2, y: _store_ri(z_scr, pl.ds(n2, N1h, stride=P), y))

    @pl.when(j >= J)
    def _epilogue():
        for ri in range(2):
            for blk in range(nb):
                rows = pl.ds(pl.multiple_of(((j - J) * nb + blk) * P, SUBLANES), N2)
                y = _load_cols(z_scr, (ri,), rows)
                sl = slice(blk * N2, (blk + 1) * N2)
                o = g_ref[ri, sl, :].astype(f32) * y
                if has_gate:
                    o = o * _silu(gate_ref[ri, sl, :].astype(f32))
                o_ref[ri, sl, :] = o.astype(bf16)


def _fftconv(u, u_cb, g, g_cb, gate, gate_cb, kf, order, tc):
    B, L, _ = u.shape
    C = HY_WIDTH
    N2, N1h, N1, P = _fft_dims(L)
    chunk = min(L, FFT_CHUNK)
    J = L // chunk
    f1, _, f2, g2, f3 = _dft_tables(L)
    T = tc // LANES
    in_map = lambda cb: (lambda c, p, j: (p, jnp.minimum(j, J - 1), cb + c))
    ep_map = lambda cb: (lambda c, p, j: (p, jnp.maximum(j - J, 0), cb + c))
    const = lambda nd: (lambda c, p, j: (0,) * nd)
    once = pl.Buffered(1)
    ops = [u, g]
    specs = [pl.BlockSpec((2, chunk, tc), in_map(u_cb)),
             pl.BlockSpec((2, chunk, tc), ep_map(g_cb))]
    if gate is not None:
        ops.append(gate)
        specs.append(pl.BlockSpec((2, chunk, tc), ep_map(gate_cb)))
    ops += [kf, f1, f2, g2, f3]
    specs += [pl.BlockSpec((1, 2, 2 * L, tc), lambda c, p, j: (order, 0, 0, c), pipeline_mode=once),
              pl.BlockSpec(f1.shape, const(2), pipeline_mode=once),
              pl.BlockSpec(f2.shape, const(3), pipeline_mode=once),
              pl.BlockSpec(g2.shape, const(3), pipeline_mode=once),
              pl.BlockSpec(f3.shape, const(2), pipeline_mode=once)]
    return pl.pallas_call(
        functools.partial(_fftconv_kernel, L=L, chunk=chunk, has_gate=gate is not None),
        grid=(C // tc, B // 2, 2 * J),
        in_specs=specs,
        out_specs=pl.BlockSpec((2, chunk, tc), ep_map(0)),
        out_shape=jax.ShapeDtypeStruct((B, L, C), bf16),
        scratch_shapes=[pltpu.VMEM((2, T, N1h * P, LANES), f32),
                        pltpu.VMEM((2, T, N2 * _a_pitch(L), LANES), f32)],
        compiler_params=_params(("parallel", "parallel", "arbitrary"), FFT_VMEM_LIMIT),
    )(*ops)


def _merge_kernel(hy_ref, at_ref, gmp_ref, gmg_ref, lng_ref, lnb_ref, ws_ref, bs_ref, mg_ref, x_ref, gate_ref,
                  gpost_ref, why_ref, wat_ref, wgm_ref, wout_ref, o_ref):
    D = x_ref.shape[2]
    tm = x_ref.shape[1]
    sub = min(tm, MERGE_ROWS)
    for r0 in range(0, tm, sub):
        rows = slice(r0, r0 + sub)

        def gated(i, br, w):
            y = jnp.dot(br, w[...], preferred_element_type=f32)
            return jax.nn.sigmoid(mg_ref[0, rows, i * D:(i + 1) * D].astype(f32)) * y

        gm = _gmlp_tile(gmp_ref[0, rows, :].astype(f32), gmg_ref[0, rows, :].astype(f32), lng_ref[...],
                        lnb_ref[...], ws_ref, bs_ref)
        acc = gated(0, hy_ref[0, rows, :], why_ref) + gated(1, at_ref[0, rows, :], wat_ref) + gated(2, gm, wgm_ref)
        o = jnp.dot(acc.astype(bf16), wout_ref[...], preferred_element_type=f32)
        r = o * lax.rsqrt(jnp.mean(o * o, axis=-1, keepdims=True) + RMS_EPS) * gpost_ref[...]
        o_ref[0, rows, :] = x_ref[0, rows, :] + gate_ref[0] * r


def _merge(hy, at, proj, x, mod, lp, tm):
    B, L, D = x.shape
    mb, pb, gb = P_MERGE // (3 * D), P_GM // (2 * GM_WIDTH), P_GM_GATE // GM_WIDTH
    row = lambda b, m: (b, m, 0)
    vec = pl.BlockSpec((1, D), lambda b, m: (0, 0))
    wspec = pl.BlockSpec((D, D), lambda b, m: (0, 0))
    return pl.pallas_call(
        _merge_kernel,
        grid=(B, L // tm),
        in_specs=[pl.BlockSpec((1, tm, D), row), pl.BlockSpec((1, tm, D), row),
                  pl.BlockSpec((1, tm, 2 * GM_WIDTH), lambda b, m: (b, m, pb)),
                  pl.BlockSpec((1, tm, GM_WIDTH), lambda b, m: (b, m, gb)),
                  vec, vec,
                  pl.BlockSpec((GM_GROUPS, CHUNK, CHUNK), lambda b, m: (0, 0, 0)),
                  pl.BlockSpec((CHUNK, GM_WIDTH), lambda b, m: (0, 0)),
                  pl.BlockSpec((1, tm, 3 * D), lambda b, m: (b, m, mb)),
                  pl.BlockSpec((1, tm, D), row),
                  pl.BlockSpec((1, 1, D), lambda b, m: (b, 0, 2)),
                  vec, wspec, wspec, wspec, wspec],
        out_specs=pl.BlockSpec((1, tm, D), row),
        out_shape=jax.ShapeDtypeStruct((B, L, D), f32),
        compiler_params=_params(("parallel", "parallel")),
    )(hy, at, proj, proj, lp['gm_ln_g'].reshape(1, D), lp['gm_ln_b'].reshape(1, D), lp['gm_ws'], lp['gm_bs_full'],
      proj, x, mod, lp['g_post'].reshape(1, D), lp['w_hy_o'], lp['w_att_o'], lp['w_gm_o'], lp['w_out'])


def _rope_tables(L):
    pos = np.arange(L)
    inv = np.power(ROPE_THETA, -np.arange(0, AXIS_DIM, 2) / AXIS_DIM)
    ar = (pos // GRID_W)[:, None] * inv
    ac = (pos % GRID_W)[:, None] * inv
    cos = np.concatenate([np.cos(ar), np.cos(ar), np.cos(ac), np.cos(ac)], axis=1)
    sin = np.concatenate([-np.sin(ar), np.sin(ar), -np.sin(ac), np.sin(ac)], axis=1)
    return jnp.asarray(cos, f32), jnp.asarray(sin, f32)


def _identity_rope(L):
    return jnp.ones((L, HEAD_DIM), f32), jnp.zeros((L, HEAD_DIM), f32)


def _hyena(proj, lp):
    B, L, _ = proj.shape
    tc = FFT_COLS
    nct = HY_WIDTH // tc
    hyc = _shortconv(proj, lp['hy_conv_w'], lp['hy_conv_b'], tc)
    k = _filters(L, lp['hy_w1'], lp['hy_b1'], lp['hy_w2'], lp['hy_b2'], lp['hy_w3'], lp['hy_freq'],
                 lp['hy_decay'], lp['hy_bias'], min(2 * L, 256))
    kf = _spectrum(k, LANES)
    z1 = _fftconv(hyc, 2 * nct, hyc, 0, None, 0, kf, 0, tc)
    return _fftconv(z1, 0, hyc, nct, proj, P_HY_GATE // tc, kf, 1, tc)


def _mixer(proj, k_all, v1_all, Lk, kblk, rope_q, lp, tq):
    att = _attention(proj, k_all, v1_all, Lk, kblk, rope_q[0], rope_q[1], lp['q_gain'], tq)
    return _hyena(proj, lp), att


def _layer(x, xc, mod_x, mod_c, rope, lp, ctx_out):
    B, L, D = x.shape
    C = xc.shape[1]
    w_in = lp['w_in']
    proj = _inproj(x, mod_x, lp['g_pre'], w_in, min(L, 1024), INPROJ_COLS)
    wc = w_in if ctx_out else w_in[:, P_K:]
    projc = _inproj(xc.reshape(1, B * C, D), mod_c[:1], lp['g_pre'], wc, min(B * C, 1024),
                    min(wc.shape[1], INPROJ_COLS)).reshape(B, C, wc.shape[1])
    ident = _identity_rope(C)
    kv = _kvprep(proj, P_K, rope[0], rope[1], lp['k_gain'], L + C, 0, min(L, 1024))
    k_all, v1_all = _kvprep(projc, P_K if ctx_out else 0, ident[0], ident[1], lp['k_gain'], L + C, L, C, into=kv)

    hy, att = _mixer(proj, k_all, v1_all, C + L, 0, rope, lp, min(L, ATT_TQ))
    x_new = _merge(hy, att, proj, x, mod_x, lp, min(L, 512))
    if not ctx_out:
        return x_new, xc
    hyc, attc = _mixer(projc, k_all, v1_all, C, L // C, ident, lp, C)
    return x_new, _merge(hyc, attc, projc, xc, mod_c, lp, C)


def kernel(x, c, ctx, c_ctx, w_mod, b_mod, g_pre, g_post, w_in, hy_conv_w, hy_conv_b, hy_w1, hy_b1, hy_w2, hy_b2, hy_w3, hy_freq, hy_decay, hy_bias, q_gain, k_gain, gm_ln_g, gm_ln_b, gm_ws, gm_bs, w_hy_o, w_att_o, w_gm_o, w_out):
    B, L, D = x.shape
    rope = _rope_tables(L)

    R = -(-(B + 1) // (2 * SUBLANES)) * (2 * SUBLANES)
    cc = jnp.zeros((R, D), f32).at[:B].set(c).at[B].set(c_ctx)
    mod = _modulation(cc, w_mod, b_mod)

    w_in_p = jnp.concatenate([w_in[:, :, a:b].astype(bf16) for a, b in _COL_RANGES], axis=2)
    gm_bs_full = jnp.repeat(jnp.swapaxes(gm_bs, 1, 2), GM_GROUP_DIM, axis=2)

    xc = ctx
    for i in range(DEPTH):
        lp = {
            'g_pre': g_pre[i], 'g_post': g_post[i], 'w_in': w_in_p[i],
            'hy_conv_w': hy_conv_w[i], 'hy_conv_b': hy_conv_b[i],
            'hy_w1': hy_w1[i], 'hy_b1': hy_b1[i], 'hy_w2': hy_w2[i], 'hy_b2': hy_b2[i],
            'hy_w3': hy_w3[i], 'hy_freq': hy_freq[i], 'hy_decay': hy_decay[i], 'hy_bias': hy_bias[i],
            'q_gain': q_gain[i], 'k_gain': k_gain[i], 'gm_ln_g': gm_ln_g[i], 'gm_ln_b': gm_ln_b[i],
            'gm_ws': gm_ws[i].astype(bf16), 'gm_bs_full': gm_bs_full[i],
            'w_hy_o': w_hy_o[i].astype(bf16), 'w_att_o': w_att_o[i].astype(bf16),
            'w_gm_o': w_gm_o[i].astype(bf16), 'w_out': w_out[i].astype(bf16),
        }
        mod_x = mod[i, :B].reshape(B, 1, 3 * D)
        mod_c = jnp.broadcast_to(mod[i, B].reshape(1, 1, 3 * D), (B, 1, 3 * D))
        x, xc = _layer(x, xc, mod_x, mod_c, rope, lp, i < DEPTH - 1)
    return x
```

```python
import functools
import math

import numpy as np
import jax
import jax.numpy as jnp
from jax import lax
from jax.experimental import pallas as pl
from jax.experimental.pallas import tpu as pltpu

f32 = jnp.float32
bf16 = jnp.bfloat16

D_MODEL = 1024
DEPTH = 2
GRID_W = 64
RMS_EPS = 1e-6
LN_EPS = 1e-5

HY_WIDTH = D_MODEL
HY_ORDER = 2
HY_BANDS = 16
HY_EMB = 1 + 2 * HY_BANDS
HY_FILTER_HIDDEN = 64

HEAD_DIM = 128
ATT_HEADS = D_MODEL // HEAD_DIM
ATT_KV_HEADS = 2
ATT_GROUP = ATT_HEADS // ATT_KV_HEADS
ATT_WIDTH = ATT_HEADS * HEAD_DIM
KV_WIDTH = ATT_KV_HEADS * HEAD_DIM
AXIS_DIM = HEAD_DIM // 2
ROPE_THETA = 10000.0

GM_WIDTH = D_MODEL
GM_GROUPS = 8
GM_GROUP_DIM = GM_WIDTH // GM_GROUPS
CHUNK = 128

OFF_HY = 0
OFF_HY_GATE = OFF_HY + (HY_ORDER + 1) * HY_WIDTH
OFF_Q = OFF_HY_GATE + HY_WIDTH
OFF_K = OFF_Q + ATT_WIDTH
OFF_V = OFF_K + KV_WIDTH
OFF_ATT_GATE = OFF_V + KV_WIDTH
OFF_GM = OFF_ATT_GATE + ATT_WIDTH
OFF_GM_GATE = OFF_GM + 2 * GM_WIDTH
OFF_MERGE = OFF_GM_GATE + GM_WIDTH
IN_WIDTH = OFF_MERGE + 3 * D_MODEL

P_HY = 0
P_HY_GATE = 3 * D_MODEL
P_Q = 4 * D_MODEL
P_ATT_GATE = 5 * D_MODEL
P_GM = 6 * D_MODEL
P_GM_GATE = 8 * D_MODEL
P_MERGE = 9 * D_MODEL
P_K = 12 * D_MODEL
P_V = P_K + KV_WIDTH
_COL_RANGES = ((OFF_HY, OFF_K), (OFF_ATT_GATE, IN_WIDTH), (OFF_K, OFF_ATT_GATE))

LANES = 128
SUBLANES = 8
FFT_COLS = 256
FFT_GROUP = 32
INPROJ_COLS = 2560
ATT_AHEAD = 1
ATT_ROWS = 256
ATT_TQ = 1024
MERGE_ROWS = 256
FFT_CHUNK = 1024
VMEM_LIMIT = 52 * 1024 * 1024
FFT_VMEM_LIMIT = 58 * 1024 * 1024

_SM_SCALE_LOG2E = (HEAD_DIM ** -0.5) * math.log2(math.e)


def _params(sem, vmem=VMEM_LIMIT):
    return pltpu.CompilerParams(dimension_semantics=sem, vmem_limit_bytes=vmem)


def _silu(x):
    return x * jax.nn.sigmoid(x)


def _mod_kernel(cc_ref, w_ref, b_ref, o_ref):
    s = _silu(cc_ref[...]).astype(bf16)
    o_ref[0] = jnp.dot(s, w_ref[0].astype(bf16), preferred_element_type=f32) + b_ref[0]


def _modulation(cc, w_mod, b_mod):
    R, D = cc.shape
    tn = D
    return pl.pallas_call(
        _mod_kernel,
        grid=(DEPTH, 3 * D // tn),
        in_specs=[pl.BlockSpec((R, D), lambda i, n: (0, 0)),
                  pl.BlockSpec((1, D, tn), lambda i, n: (i, 0, n)),
                  pl.BlockSpec((1, 1, tn), lambda i, n: (i, 0, n))],
        out_specs=pl.BlockSpec((1, R, tn), lambda i, n: (i, 0, n)),
        out_shape=jax.ShapeDtypeStruct((DEPTH, R, 3 * D), f32),
        compiler_params=_params(("parallel", "parallel")),
    )(cc, w_mod, b_mod.reshape(DEPTH, 1, 3 * D))


def _inproj_kernel(x_ref, mod_ref, g_ref, w_ref, o_ref, h_scr):
    D = x_ref.shape[2]

    @pl.when(pl.program_id(2) == 0)
    def _():
        x = x_ref[0]
        r = lax.rsqrt(jnp.mean(x * x, axis=-1, keepdims=True) + RMS_EPS)
        shift = mod_ref[0, :, 0:D]
        scale = mod_ref[0, :, D:2 * D]
        h_scr[...] = ((x * r) * g_ref[...] * (1.0 + scale) + shift).astype(bf16)

    o_ref[0] = jnp.dot(h_scr[...], w_ref[...], preferred_element_type=f32).astype(o_ref.dtype)


def _inproj(x, mod, g_pre, w, tm, tn):
    B, L, D = x.shape
    N = w.shape[1]
    return pl.pallas_call(
        _inproj_kernel,
        grid=(B, L // tm, N // tn),
        in_specs=[pl.BlockSpec((1, tm, D), lambda b, m, n: (b, m, 0)),
                  pl.BlockSpec((1, 1, 3 * D), lambda b, m, n: (b, 0, 0)),
                  pl.BlockSpec((1, D), lambda b, m, n: (0, 0)),
                  pl.BlockSpec((D, tn), lambda b, m, n: (0, n))],
        out_specs=pl.BlockSpec((1, tm, tn), lambda b, m, n: (b, m, n)),
        out_shape=jax.ShapeDtypeStruct((B, L, N), bf16),
        scratch_shapes=[pltpu.VMEM((tm, D), bf16)],
        compiler_params=_params(("parallel", "parallel", "arbitrary")),
    )(x, mod, g_pre.reshape(1, D), w)


def _rope(x, cos, sin):
    lane = lax.broadcasted_iota(jnp.int32, x.shape, 1)
    first = (lane & (AXIS_DIM // 2)) == 0
    rot = jnp.where(first, pltpu.roll(x, HEAD_DIM - AXIS_DIM // 2, 1), pltpu.roll(x, AXIS_DIM // 2, 1))
    return x * cos + rot * sin


def _head_norm(x, gain):
    r = lax.rsqrt(jnp.mean(x * x, axis=-1, keepdims=True) + RMS_EPS)
    return x * r * gain


def _kvprep_kernel(kv_ref, cos_ref, sin_ref, gain_ref, *rest):
    k_ref, v1_ref = rest[-2:]
    kv = kv_ref[0]
    for h in range(ATT_KV_HEADS):
        sl = slice(h * HEAD_DIM, (h + 1) * HEAD_DIM)
        kn = _head_norm(kv[:, sl].astype(f32), gain_ref[...])
        k_ref[0, :, sl] = _rope(kn, cos_ref[...], sin_ref[...]).astype(bf16)
        v = kv[:, KV_WIDTH + h * HEAD_DIM:KV_WIDTH + (h + 1) * HEAD_DIM]
        v1_ref[0, :, 2 * h * HEAD_DIM:(2 * h + 1) * HEAD_DIM] = v
        v1_ref[0, :, (2 * h + 1) * HEAD_DIM:(2 * h + 2) * HEAD_DIM] = jnp.ones_like(v)


def _kvprep(src, col_off, cos, sin, gain, Lk, row_off, tm, into=None):
    B, n, _ = src.shape
    w = 2 * KV_WIDTH
    cb, rb = col_off // w, row_off // tm
    in_specs = [pl.BlockSpec((1, tm, w), lambda b, m: (b, m, cb)),
                pl.BlockSpec((tm, HEAD_DIM), lambda b, m: (m, 0)),
                pl.BlockSpec((tm, HEAD_DIM), lambda b, m: (m, 0)),
                pl.BlockSpec((1, HEAD_DIM), lambda b, m: (0, 0))]
    ops = [src, cos, sin, gain.reshape(1, HEAD_DIM)]
    aliases = {}
    if into is not None:
        in_specs += [pl.BlockSpec(memory_space=pl.ANY)] * 2
        ops += list(into)
        aliases = {4: 0, 5: 1}
    return pl.pallas_call(
        _kvprep_kernel,
        grid=(B, n // tm),
        in_specs=in_specs,
        out_specs=[pl.BlockSpec((1, tm, KV_WIDTH), lambda b, m: (b, rb + m, 0)),
                   pl.BlockSpec((1, tm, w), lambda b, m: (b, rb + m, 0))],
        out_shape=[jax.ShapeDtypeStruct((B, Lk, KV_WIDTH), bf16),
                   jax.ShapeDtypeStruct((B, Lk, w), bf16)],
        input_output_aliases=aliases,
        compiler_params=_params(("parallel", "parallel")),
    )(*ops)


def _attn_kernel(q_ref, k_ref, v1_ref, cos_ref, sin_ref, gain_ref, gate_ref, o_ref):
    k = k_ref[0]
    v1 = v1_ref[0]
    tq = q_ref.shape[1]
    sub = min(tq, ATT_ROWS)
    units = [(slice(r * sub, (r + 1) * sub), slice(h * HEAD_DIM, (h + 1) * HEAD_DIM))
             for r in range(tq // sub) for h in range(ATT_GROUP)]

    def scores(unit):
        rows, sl = unit
        qn = _head_norm(q_ref[0, rows, sl].astype(f32), gain_ref[...])
        q = (_rope(qn, cos_ref[rows, :], sin_ref[rows, :]) * _SM_SCALE_LOG2E).astype(bf16)
        return lax.dot_general(q, k, (((1,), (1,)), ((), ())), preferred_element_type=f32)

    pending = [scores(u) for u in units[:ATT_AHEAD]]
    for i, (rows, sl) in enumerate(units):
        s = pending.pop(0)
        if i + ATT_AHEAD < len(units):
            pending.append(scores(units[i + ATT_AHEAD]))
        p = jnp.exp2(s - jnp.max(s, axis=-1, keepdims=True))
        ol = jnp.dot(p.astype(bf16), v1, preferred_element_type=f32)
        o = ol[:, :HEAD_DIM] / ol[:, HEAD_DIM:HEAD_DIM + 1]
        o_ref[0, rows, sl] = (o * _silu(gate_ref[0, rows, sl].astype(f32))).astype(bf16)


def _attention(proj, k_all, vt_all, Lk, kblk, cos, sin, gain, tq):
    B, L, _ = proj.shape
    gw = ATT_GROUP * HEAD_DIM
    qb, gb = P_Q // gw, P_ATT_GATE // gw
    return pl.pallas_call(
        _attn_kernel,
        grid=(B, ATT_KV_HEADS, L // tq),
        in_specs=[pl.BlockSpec((1, tq, gw), lambda b, h, i: (b, i, qb + h)),
                  pl.BlockSpec((1, Lk, HEAD_DIM), lambda b, h, i: (b, kblk, h)),
                  pl.BlockSpec((1, Lk, 2 * HEAD_DIM), lambda b, h, i: (b, kblk, h)),
                  pl.BlockSpec((tq, HEAD_DIM), lambda b, h, i: (i, 0)),
                  pl.BlockSpec((tq, HEAD_DIM), lambda b, h, i: (i, 0)),
                  pl.BlockSpec((1, HEAD_DIM), lambda b, h, i: (0, 0)),
                  pl.BlockSpec((1, tq, gw), lambda b, h, i: (b, i, gb + h))],
        out_specs=pl.BlockSpec((1, tq, gw), lambda b, h, i: (b, i, h)),
        out_shape=jax.ShapeDtypeStruct((B, L, ATT_WIDTH), bf16),
        compiler_params=_params(("parallel", "parallel", "parallel")),
    )(proj, k_all, vt_all, cos, sin, gain.reshape(1, HEAD_DIM), proj)


def _gmlp_tile(p, gate, lng, lnb, ws_ref, bs_ref):
    uv = 0.5 * p * (1.0 + lax.erf(p * (2.0 ** -0.5)))
    u = uv[:, :GM_WIDTH]
    v = uv[:, GM_WIDTH:]
    vc = v - jnp.mean(v, axis=-1, keepdims=True)
    var = jnp.mean(vc * vc, axis=-1, keepdims=True)
    vn = (vc * lax.rsqrt(var + LN_EPS) * lng + lnb).astype(bf16)
    ug = u * _silu(gate)
    chunks = []
    for n in range(p.shape[0] // CHUNK):
        rows = slice(n * CHUNK, (n + 1) * CHUNK)
        groups = []
        for g in range(GM_GROUPS):
            cols = slice(g * GM_GROUP_DIM, (g + 1) * GM_GROUP_DIM)
            y = jnp.dot(ws_ref[g], vn[rows, cols], preferred_element_type=f32) + bs_ref[:, cols]
            groups.append((ug[rows, cols] * y).astype(bf16))
        chunks.append(jnp.concatenate(groups, axis=1))
    return jnp.concatenate(chunks, axis=0)


def _shortconv_kernel(p_ref, w_ref, b_ref, o_ref, s_scr):
    L = p_ref.shape[1]
    lo, hi = SUBLANES, SUBLANES + L
    zero = jnp.zeros((SUBLANES, LANES), f32)
    for t in range(s_scr.shape[0]):
        cols = slice(t * LANES, (t + 1) * LANES)
        s_scr[t, 0:lo, :] = zero
        s_scr[t, hi:hi + SUBLANES, :] = zero
        s_scr[t, lo:hi, :] = p_ref[0, :, cols].astype(f32)
        y = (s_scr[t, lo - 1:hi - 1, :] * w_ref[0:1, cols] + s_scr[t, lo:hi, :] * w_ref[1:2, cols]
             + s_scr[t, lo + 1:hi + 1, :] * w_ref[2:3, cols] + b_ref[:, cols])
        o_ref[0, :, cols] = y.astype(bf16)


def _shortconv(proj, w, b, tc):
    B, L, _ = proj.shape
    W = w.shape[1]
    return pl.pallas_call(
        _shortconv_kernel,
        grid=(B, W // tc),
        in_specs=[pl.BlockSpec((1, L, tc), lambda b, c: (b, 0, c)),
                  pl.BlockSpec((3, tc), lambda b, c: (0, c)),
                  pl.BlockSpec((1, tc), lambda b, c: (0, c))],
        out_specs=pl.BlockSpec((1, L, tc), lambda b, c: (b, 0, c)),
        out_shape=jax.ShapeDtypeStruct((B, L, W), bf16),
        scratch_shapes=[pltpu.VMEM((tc // LANES, L + 2 * SUBLANES, LANES), f32)],
        compiler_params=_params(("parallel", "parallel")),
    )(proj, w, b.reshape(1, W))


def _filter_embedding(L):
    pos = np.arange(L)
    bands = np.linspace(1e-4, HY_BANDS - 1, HY_BANDS)
    ang = (2.0 * math.pi / L) * pos[:, None] * bands[None, :]
    z = np.zeros((L, LANES), np.float64)
    z[:, 0] = np.linspace(0.0, 1.0, L)
    z[:, 1:1 + HY_BANDS] = np.cos(ang)
    z[:, 1 + HY_BANDS:HY_EMB] = -np.sin(ang)
    z[:, HY_EMB] = (pos == 0)
    return jnp.asarray(z, f32)


def _filter_kernel(z_ref, w1_ref, b1_ref, w2_ref, b2_ref, freq_ref, w3_ref, dec_ref, bias_ref, o_ref):
    hp = lax.Precision.HIGHEST
    C = o_ref.shape[3]
    z = z_ref[...]
    t = z[:, 0:1]
    m0 = z[:, HY_EMB:HY_EMB + 1]
    h = jnp.sin(freq_ref[0:1, :] * (jnp.dot(z, w1_ref[...], precision=hp, preferred_element_type=f32) + b1_ref[...]))
    h = jnp.sin(freq_ref[1:2, :] * (jnp.dot(h, w2_ref[...], precision=hp, preferred_element_type=f32) + b2_ref[...]))
    hi = h.astype(bf16)
    lo = (h - hi.astype(f32)).astype(bf16)
    h3 = jnp.concatenate([hi, lo, hi], axis=1)
    for blk in range(2 * HY_ORDER):
        cols = slice(blk * C, (blk + 1) * C)
        taps = jnp.dot(h3, w3_ref[:, cols], preferred_element_type=f32) * jnp.exp(-t * jnp.abs(dec_ref[:, cols]))
        if blk % 2 == 0:
            taps = taps + m0 * bias_ref[blk // 2:blk // 2 + 1, :]
        o_ref[blk // 2, blk % 2] = taps


def _filters(L, w1, b1, w2, b2, w3, freq, decay, bias, tr):
    H = HY_FILTER_HIDDEN
    C = HY_WIDTH
    W = 2 * HY_ORDER * C
    z = _filter_embedding(L)
    w1p = jnp.zeros((LANES, H), f32).at[:HY_EMB].set(w1)
    w3_hi = w3.astype(bf16)
    w3_lo = (w3 - w3_hi.astype(f32)).astype(bf16)
    w3 = jnp.concatenate([w3_hi, w3_hi, w3_lo], axis=0)
    full = lambda shape: pl.BlockSpec(shape, lambda r: (0,) * len(shape))
    return pl.pallas_call(
        _filter_kernel,
        grid=(L // tr,),
        in_specs=[pl.BlockSpec((tr, LANES), lambda r: (r, 0)),
                  full((LANES, H)), full((1, H)), full((H, H)), full((1, H)), full((2, H)),
                  full((3 * H, W)), full((1, W)), full((HY_ORDER, C))],
        out_specs=pl.BlockSpec((HY_ORDER, 2, tr, C), lambda r: (0, 0, r, 0)),
        out_shape=jax.ShapeDtypeStruct((HY_ORDER, 2, L, C), f32),
        compiler_params=_params(("parallel",)),
    )(z, w1p, b1.reshape(1, H), w2, b2.reshape(1, H), freq, w3, decay.reshape(1, W), bias)


def _fft_dims(L):
    n2 = SUBLANES
    while (2 * n2) * (2 * n2) <= L:
        n2 *= 2
    return n2, L // n2, 2 * (L // n2), n2 + SUBLANES


def _a_pitch(L):
    return _fft_dims(L)[2] + SUBLANES


def _stack(fr, fi):
    return np.block([[fr, -fi], [fi, fr]])


@functools.lru_cache(maxsize=None)
def _dft_tables(L):
    N2, N1h, N1, _ = _fft_dims(L)
    N = 2 * L
    k1 = np.arange(N1)
    a1 = -2.0 * np.pi * np.outer(k1, np.arange(N1)) / N1
    f1 = _stack(np.cos(a1[:, :N1h]), np.sin(a1[:, :N1h]))
    f1k = np.concatenate([np.cos(a1[:, :N1h]), np.sin(a1[:, :N1h])], axis=0)
    f3 = _stack(np.cos(-a1[:, :N1h].T), np.sin(-a1[:, :N1h].T))
    n2 = np.arange(N2)
    a2 = -2.0 * np.pi * (np.outer(n2, n2)[None] / N2 + (k1[:, None, None] * n2[None, None, :]) / N)
    f2 = np.stack([_stack(np.cos(a), np.sin(a)) for a in a2])
    a2t = -np.transpose(a2, (0, 2, 1))
    g2 = np.stack([_stack(np.cos(a), np.sin(a)) for a in a2t])
    return tuple(jnp.asarray(m, bf16) for m in (f1, f1k, f2, g2, f3))


def _load_cols(ref, lead, rows):
    return jnp.concatenate([ref[lead + (t, rows, slice(None))] for t in range(ref.shape[-3])], axis=1)


def _store_cols(ref, lead, rows, val):
    for t in range(ref.shape[-3]):
        ref[lead + (t, rows, slice(None))] = val[:, t * LANES:(t + 1) * LANES]


def _grouped_loop(n, group, load, compute, store):
    group = min(group, n)
    assert n % group == 0

    def body(i, c):
        idx = [i * group + u for u in range(group)]
        vals = [load(ix) for ix in idx]
        outs = [compute(ix, v) for ix, v in zip(idx, vals)]
        for ix, o in zip(idx, outs):
            store(ix, o)
        return c

    lax.fori_loop(0, n // group, body, 0)


def _load_ri(ref, rows):
    return jnp.concatenate([_load_cols(ref, (ri,), rows) for ri in range(2)], axis=0).astype(bf16)


def _store_ri(ref, rows, val):
    h = val.shape[0] // 2
    for ri in range(2):
        _store_cols(ref, (ri,), rows, val[ri * h:(ri + 1) * h])


def _spectrum_kernel(k_ref, f1k_ref, f2_ref, o_ref, a_scr, *, L):
    N2, N1h, N1, P = _fft_dims(L)
    PA = _a_pitch(L)
    inv_n = 1.0 / (2 * L)

    def load1(n2):
        rows = pl.ds(n2, N1h, stride=N2)
        return jnp.concatenate([k_ref[0, 0, rows, :], k_ref[0, 1, rows, :]], axis=1).astype(bf16)

    _grouped_loop(N2, FFT_GROUP, load1,
                  lambda n2, w: jnp.dot(f1k_ref[...], w, preferred_element_type=f32),
                  lambda n2, r: _store_ri(a_scr, pl.ds(pl.multiple_of(n2 * PA, SUBLANES), N1), r))

    def store2(k1, x):
        out_rows = pl.ds(pl.multiple_of(k1 * N2, N2), N2)
        o_ref[0, 0, out_rows, :] = (x[:N2, :LANES] + x[:N2, LANES:]).astype(bf16)
        o_ref[0, 1, out_rows, :] = (x[N2:, :LANES] - x[N2:, LANES:]).astype(bf16)

    _grouped_loop(N1, FFT_GROUP,
                  lambda k1: _load_ri(a_scr, pl.ds(k1, N2, stride=PA)),
                  lambda k1, a: jnp.dot(f2_ref[k1], a, preferred_element_type=f32) * inv_n,
                  store2)


def _spectrum(taps):
    _, _, L, C = taps.shape
    N2, N1h, N1, P = _fft_dims(L)
    _, f1k, f2, _, _ = _dft_tables(L)
    return pl.pallas_call(
        functools.partial(_spectrum_kernel, L=L),
        grid=(HY_ORDER, C // LANES),
        in_specs=[pl.BlockSpec((1, 2, L, LANES), lambda o, c: (o, 0, 0, c)),
                  pl.BlockSpec(f1k.shape, lambda o, c: (0, 0)),
                  pl.BlockSpec(f2.shape, lambda o, c: (0, 0, 0))],
        out_specs=pl.BlockSpec((1, 2, 2 * L, LANES), lambda o, c: (o, 0, 0, c)),
        out_shape=jax.ShapeDtypeStruct((HY_ORDER, 2, 2 * L, C), bf16),
        scratch_shapes=[pltpu.VMEM((2, 2, N2 * _a_pitch(L), LANES), f32)],
        compiler_params=_params(("parallel", "parallel")),
    )(taps, f1k, f2)


def _fftconv_kernel(*refs, L, chunk, has_gate):
    if has_gate:
        u_in, g_ref, gate_ref, kf_ref, f1_ref, f2_ref, g2_ref, f3_ref, o_ref, z_scr, a_scr = refs
    else:
        u_in, g_ref, kf_ref, f1_ref, f2_ref, g2_ref, f3_ref, o_ref, z_scr, a_scr = refs
    N2, N1h, N1, P = _fft_dims(L)
    J = L // chunk
    nb = chunk // N2
    j = pl.program_id(2)

    @pl.when(j < J)
    def _load():
        for ri in range(2):
            for blk in range(nb):
                rows = pl.ds(pl.multiple_of((j * nb + blk) * P, SUBLANES), N2)
                _store_cols(z_scr, (ri,), rows, u_in[ri, blk * N2:(blk + 1) * N2, :].astype(f32))

    @pl.when(j == J - 1)
    def _transform():
        PA = _a_pitch(L)

        def a_block(n2):
            return pl.ds(pl.multiple_of(n2 * PA, SUBLANES), N1)

        _grouped_loop(N2, FFT_GROUP,
                      lambda n2: _load_ri(z_scr, pl.ds(n2, N1h, stride=P)),
                      lambda n2, w: jnp.dot(f1_ref[...], w, preferred_element_type=f32),
                      lambda n2, r: _store_ri(a_scr, a_block(n2), r))

        def block_rows(k1):
            return pl.ds(k1, N2, stride=PA)

        def middle(k1, a):
            x = jnp.dot(f2_ref[k1], a, preferred_element_type=f32)
            krows = pl.ds(pl.multiple_of(k1 * N2, N2), N2)
            kr = kf_ref[0, 0, krows, :].astype(f32)
            ki = kf_ref[0, 1, krows, :].astype(f32)
            xr, xi = x[:N2], x[N2:]
            y = jnp.concatenate([xr * kr - xi * ki, xr * ki + xi * kr], axis=0).astype(bf16)
            return jnp.dot(g2_ref[k1], y, preferred_element_type=f32)

        _grouped_loop(N1, FFT_GROUP,
                      lambda k1: _load_ri(a_scr, block_rows(k1)),
                      middle,
                      lambda k1, b: _store_ri(a_scr, block_rows(k1), b))

        _grouped_loop(N2, FFT_GROUP,
                      lambda n2: _load_ri(a_scr, a_block(n2)),
                      lambda n2, b: jnp.dot(f3_ref[...], b, preferred_element_type=f32),
                      lambda n2, y: _store_ri(z_scr, pl.ds(n2, N1h, stride=P), y))

    @pl.when(j >= J)
    def _epilogue():
        for ri in range(2):
            for blk in range(nb):
                rows = pl.ds(pl.multiple_of(((j - J) * nb + blk) * P, SUBLANES), N2)
                y = _load_cols(z_scr, (ri,), rows)
                sl = slice(blk * N2, (blk + 1) * N2)
                o = g_ref[ri, sl, :].astype(f32) * y
                if has_gate:
                    o = o * _silu(gate_ref[ri, sl, :].astype(f32))
                o_ref[ri, sl, :] = o.astype(bf16)


def _fftconv(u, u_cb, g, g_cb, gate, gate_cb, kf, order, tc):
    B, L, _ = u.shape
    C = HY_WIDTH
    N2, N1h, N1, P = _fft_dims(L)
    chunk = min(L, FFT_CHUNK)
    J = L // chunk
    f1, _, f2, g2, f3 = _dft_tables(L)
    T = tc // LANES
    in_map = lambda cb: (lambda c, p, j: (p, jnp.minimum(j, J - 1), cb + c))
    ep_map = lambda cb: (lambda c, p, j: (p, jnp.maximum(j - J, 0), cb + c))
    const = lambda nd: (lambda c, p, j: (0,) * nd)
    once = pl.Buffered(1)
    ops = [u, g]
    specs = [pl.BlockSpec((2, chunk, tc), in_map(u_cb)),
             pl.BlockSpec((2, chunk, tc), ep_map(g_cb))]
    if gate is not None:
        ops.append(gate)
        specs.append(pl.BlockSpec((2, chunk, tc), ep_map(gate_cb)))
    ops += [kf, f1, f2, g2, f3]
    specs += [pl.BlockSpec((1, 2, 2 * L, tc), lambda c, p, j: (order, 0, 0, c), pipeline_mode=once),
              pl.BlockSpec(f1.shape, const(2), pipeline_mode=once),
              pl.BlockSpec(f2.shape, const(3), pipeline_mode=once),
              pl.BlockSpec(g2.shape, const(3), pipeline_mode=once),
              pl.BlockSpec(f3.shape, const(2), pipeline_mode=once)]
    return pl.pallas_call(
        functools.partial(_fftconv_kernel, L=L, chunk=chunk, has_gate=gate is not None),
        grid=(C // tc, B // 2, 2 * J),
        in_specs=specs,
        out_specs=pl.BlockSpec((2, chunk, tc), ep_map(0)),
        out_shape=jax.ShapeDtypeStruct((B, L, C), bf16),
        scratch_shapes=[pltpu.VMEM((2, T, N1h * P, LANES), f32),
                        pltpu.VMEM((2, T, N2 * _a_pitch(L), LANES), f32)],
        compiler_params=_params(("parallel", "parallel", "arbitrary"), FFT_VMEM_LIMIT),
    )(*ops)


def _merge_kernel(hy_ref, at_ref, gmp_ref, gmg_ref, lng_ref, lnb_ref, ws_ref, bs_ref, mg_ref, x_ref, gate_ref,
                  gpost_ref, why_ref, wat_ref, wgm_ref, wout_ref, o_ref):
    D = x_ref.shape[2]
    tm = x_ref.shape[1]
    sub = min(tm, MERGE_ROWS)
    for r0 in range(0, tm, sub):
        rows = slice(r0, r0 + sub)

        def gated(i, br, w):
            y = jnp.dot(br, w[...], preferred_element_type=f32)
            return jax.nn.sigmoid(mg_ref[0, rows, i * D:(i + 1) * D].astype(f32)) * y

        gm = _gmlp_tile(gmp_ref[0, rows, :].astype(f32), gmg_ref[0, rows, :].astype(f32), lng_ref[...],
                        lnb_ref[...], ws_ref, bs_ref)
        acc = gated(0, hy_ref[0, rows, :], why_ref) + gated(1, at_ref[0, rows, :], wat_ref) + gated(2, gm, wgm_ref)
        o = jnp.dot(acc.astype(bf16), wout_ref[...], preferred_element_type=f32)
        r = o * lax.rsqrt(jnp.mean(o * o, axis=-1, keepdims=True) + RMS_EPS) * gpost_ref[...]
        o_ref[0, rows, :] = x_ref[0, rows, :] + gate_ref[0] * r


def _merge(hy, at, proj, x, mod, lp, tm):
    B, L, D = x.shape
    mb, pb, gb = P_MERGE // (3 * D), P_GM // (2 * GM_WIDTH), P_GM_GATE // GM_WIDTH
    row = lambda b, m: (b, m, 0)
    vec = pl.BlockSpec((1, D), lambda b, m: (0, 0))
    wspec = pl.BlockSpec((D, D), lambda b, m: (0, 0))
    return pl.pallas_call(
        _merge_kernel,
        grid=(B, L // tm),
        in_specs=[pl.BlockSpec((1, tm, D), row), pl.BlockSpec((1, tm, D), row),
                  pl.BlockSpec((1, tm, 2 * GM_WIDTH), lambda b, m: (b, m, pb)),
                  pl.BlockSpec((1, tm, GM_WIDTH), lambda b, m: (b, m, gb)),
                  vec, vec,
                  pl.BlockSpec((GM_GROUPS, CHUNK, CHUNK), lambda b, m: (0, 0, 0)),
                  pl.BlockSpec((CHUNK, GM_WIDTH), lambda b, m: (0, 0)),
                  pl.BlockSpec((1, tm, 3 * D), lambda b, m: (b, m, mb)),
                  pl.BlockSpec((1, tm, D), row),
                  pl.BlockSpec((1, 1, D), lambda b, m: (b, 0, 2)),
                  vec, wspec, wspec, wspec, wspec],
        out_specs=pl.BlockSpec((1, tm, D), row),
        out_shape=jax.ShapeDtypeStruct((B, L, D), f32),
        compiler_params=_params(("parallel", "parallel")),
    )(hy, at, proj, proj, lp['gm_ln_g'].reshape(1, D), lp['gm_ln_b'].reshape(1, D), lp['gm_ws'], lp['gm_bs_full'],
      proj, x, mod, lp['g_post'].reshape(1, D), lp['w_hy_o'], lp['w_att_o'], lp['w_gm_o'], lp['w_out'])


def _rope_tables(L):
    pos = np.arange(L)
    inv = np.power(ROPE_THETA, -np.arange(0, AXIS_DIM, 2) / AXIS_DIM)
    ar = (pos // GRID_W)[:, None] * inv
    ac = (pos % GRID_W)[:, None] * inv
    cos = np.concatenate([np.cos(ar), np.cos(ar), np.cos(ac), np.cos(ac)], axis=1)
    sin = np.concatenate([-np.sin(ar), np.sin(ar), -np.sin(ac), np.sin(ac)], axis=1)
    return jnp.asarray(cos, f32), jnp.asarray(sin, f32)


def _identity_rope(L):
    return jnp.ones((L, HEAD_DIM), f32), jnp.zeros((L, HEAD_DIM), f32)


def _hyena(proj, lp):
    B, L, _ = proj.shape
    tc = FFT_COLS
    nct = HY_WIDTH // tc
    hyc = _shortconv(proj, lp['hy_conv_w'], lp['hy_conv_b'], tc)
    taps = _filters(L, lp['hy_w1'], lp['hy_b1'], lp['hy_w2'], lp['hy_b2'], lp['hy_w3'], lp['hy_freq'],
                    lp['hy_decay'], lp['hy_bias'], min(L, 256))
    kf = _spectrum(taps)
    z1 = _fftconv(hyc, 2 * nct, hyc, 0, None, 0, kf, 0, tc)
    return _fftconv(z1, 0, hyc, nct, proj, P_HY_GATE // tc, kf, 1, tc)


def _mixer(proj, k_all, v1_all, Lk, kblk, rope_q, lp, tq):
    att = _attention(proj, k_all, v1_all, Lk, kblk, rope_q[0], rope_q[1], lp['q_gain'], tq)
    return _hyena(proj, lp), att


def _layer(x, xc, mod_x, mod_c, rope, lp, ctx_out):
    B, L, D = x.shape
    C = xc.shape[1]
    w_in = lp['w_in']
    proj = _inproj(x, mod_x, lp['g_pre'], w_in, min(L, 1024), INPROJ_COLS)
    wc = w_in if ctx_out else w_in[:, P_K:]
    projc = _inproj(xc.reshape(1, B * C, D), mod_c[:1], lp['g_pre'], wc, min(B * C, 1024),
                    min(wc.shape[1], INPROJ_COLS)).reshape(B, C, wc.shape[1])
    ident = _identity_rope(C)
    kv = _kvprep(proj, P_K, rope[0], rope[1], lp['k_gain'], L + C, 0, min(L, 1024))
    k_all, v1_all = _kvprep(projc, P_K if ctx_out else 0, ident[0], ident[1], lp['k_gain'], L + C, L, C, into=kv)

    hy, att = _mixer(proj, k_all, v1_all, C + L, 0, rope, lp, min(L, ATT_TQ))
    x_new = _merge(hy, att, proj, x, mod_x, lp, min(L, 512))
    if not ctx_out:
        return x_new, xc
    hyc, attc = _mixer(projc, k_all, v1_all, C, L // C, ident, lp, C)
    return x_new, _merge(hyc, attc, projc, xc, mod_c, lp, C)


def kernel(x, c, ctx, c_ctx, w_mod, b_mod, g_pre, g_post, w_in, hy_conv_w, hy_conv_b, hy_w1, hy_b1, hy_w2, hy_b2, hy_w3, hy_freq, hy_decay, hy_bias, q_gain, k_gain, gm_ln_g, gm_ln_b, gm_ws, gm_bs, w_hy_o, w_att_o, w_gm_o, w_out):
    B, L, D = x.shape
    rope = _rope_tables(L)

    R = -(-(B + 1) // (2 * SUBLANES)) * (2 * SUBLANES)
    cc = jnp.zeros((R, D), f32).at[:B].set(c).at[B].set(c_ctx)
    mod = _modulation(cc, w_mod, b_mod)

    w_in_p = jnp.concatenate([w_in[:, :, a:b].astype(bf16) for a, b in _COL_RANGES], axis=2)
    gm_bs_full = jnp.repeat(jnp.swapaxes(gm_bs, 1, 2), GM_GROUP_DIM, axis=2)

    xc = ctx
    for i in range(DEPTH):
        lp = {
            'g_pre': g_pre[i], 'g_post': g_post[i], 'w_in': w_in_p[i],
            'hy_conv_w': hy_conv_w[i], 'hy_conv_b': hy_conv_b[i],
            'hy_w1': hy_w1[i], 'hy_b1': hy_b1[i], 'hy_w2': hy_w2[i], 'hy_b2': hy_b2[i],
            'hy_w3': hy_w3[i], 'hy_freq': hy_freq[i], 'hy_decay': hy_decay[i], 'hy_bias': hy_bias[i],
            'q_gain': q_gain[i], 'k_gain': k_gain[i], 'gm_ln_g': gm_ln_g[i], 'gm_ln_b': gm_ln_b[i],
            'gm_ws': gm_ws[i].astype(bf16), 'gm_bs_full': gm_bs_full[i],
            'w_hy_o': w_hy_o[i].astype(bf16), 'w_att_o': w_att_o[i].astype(bf16),
            'w_gm_o': w_gm_o[i].astype(bf16), 'w_out': w_out[i].astype(bf16),
        }
        mod_x = mod[i, :B].reshape(B, 1, 3 * D)
        mod_c = jnp.broadcast_to(mod[i, B].reshape(1, 1, 3 * D), (B, 1, 3 * D))
        x, xc = _layer(x, xc, mod_x, mod_c, rope, lp, i < DEPTH - 1)
    return x
```

```python
import functools
import math

import numpy as np
import jax
import jax.numpy as jnp
from jax import lax
from jax.experimental import pallas as pl
from jax.experimental.pallas import tpu as pltpu

f32 = jnp.float32
bf16 = jnp.bfloat16

D_MODEL = 1024
DEPTH = 2
GRID_W = 64
RMS_EPS = 1e-6
LN_EPS = 1e-5

HY_WIDTH = D_MODEL
HY_ORDER = 2
HY_BANDS = 16
HY_EMB = 1 + 2 * HY_BANDS
HY_FILTER_HIDDEN = 64

HEAD_DIM = 128
ATT_HEADS = D_MODEL // HEAD_DIM
ATT_KV_HEADS = 2
ATT_GROUP = ATT_HEADS // ATT_KV_HEADS
ATT_WIDTH = ATT_HEADS * HEAD_DIM
KV_WIDTH = ATT_KV_HEADS * HEAD_DIM
AXIS_DIM = HEAD_DIM // 2
ROPE_THETA = 10000.0

GM_WIDTH = D_MODEL
GM_GROUPS = 8
GM_GROUP_DIM = GM_WIDTH // GM_GROUPS
CHUNK = 128

OFF_HY = 0
OFF_HY_GATE = OFF_HY + (HY_ORDER + 1) * HY_WIDTH
OFF_Q = OFF_HY_GATE + HY_WIDTH
OFF_K = OFF_Q + ATT_WIDTH
OFF_V = OFF_K + KV_WIDTH
OFF_ATT_GATE = OFF_V + KV_WIDTH
OFF_GM = OFF_ATT_GATE + ATT_WIDTH
OFF_GM_GATE = OFF_GM + 2 * GM_WIDTH
OFF_MERGE = OFF_GM_GATE + GM_WIDTH
IN_WIDTH = OFF_MERGE + 3 * D_MODEL

P_HY = 0
P_HY_GATE = 3 * D_MODEL
P_Q = 4 * D_MODEL
P_ATT_GATE = 5 * D_MODEL
P_GM = 6 * D_MODEL
P_GM_GATE = 8 * D_MODEL
P_MERGE = 9 * D_MODEL
P_K = 12 * D_MODEL
P_V = P_K + KV_WIDTH
_COL_RANGES = ((OFF_HY, OFF_K), (OFF_ATT_GATE, IN_WIDTH), (OFF_K, OFF_ATT_GATE))

LANES = 128
SUBLANES = 8
FFT_COLS = 256
FFT_GROUP = 32
INPROJ_COLS = 2560
ATT_AHEAD = 1
ATT_ROWS = 512
ATT_TQ = 1024
MERGE_ROWS = 256
FFT_CHUNK = 1024
V7X_VMEM_BYTES = 64 * 1024 * 1024
VMEM_LIMIT = V7X_VMEM_BYTES - 12 * 1024 * 1024
FFT_VMEM_LIMIT = V7X_VMEM_BYTES - 6 * 1024 * 1024

_SM_SCALE_LOG2E = (HEAD_DIM ** -0.5) * math.log2(math.e)


def _params(sem, vmem=VMEM_LIMIT):
    return pltpu.CompilerParams(dimension_semantics=sem, vmem_limit_bytes=vmem)


def _silu(x):
    return x * jax.nn.sigmoid(x)


def _mod_kernel(cc_ref, w_ref, b_ref, o_ref):
    s = _silu(cc_ref[...]).astype(bf16)
    o_ref[0] = jnp.dot(s, w_ref[0].astype(bf16), preferred_element_type=f32) + b_ref[0]


def _modulation(cc, w_mod, b_mod):
    R, D = cc.shape
    tn = D
    return pl.pallas_call(
        _mod_kernel,
        grid=(DEPTH, 3 * D // tn),
        in_specs=[pl.BlockSpec((R, D), lambda i, n: (0, 0)),
                  pl.BlockSpec((1, D, tn), lambda i, n: (i, 0, n)),
                  pl.BlockSpec((1, 1, tn), lambda i, n: (i, 0, n))],
        out_specs=pl.BlockSpec((1, R, tn), lambda i, n: (i, 0, n)),
        out_shape=jax.ShapeDtypeStruct((DEPTH, R, 3 * D), f32),
        compiler_params=_params(("parallel", "parallel")),
    )(cc, w_mod, b_mod.reshape(DEPTH, 1, 3 * D))


def _inproj_kernel(x_ref, mod_ref, g_ref, w_ref, o_ref, h_scr):
    D = x_ref.shape[2]

    @pl.when(pl.program_id(2) == 0)
    def _():
        x = x_ref[0]
        r = lax.rsqrt(jnp.mean(x * x, axis=-1, keepdims=True) + RMS_EPS)
        shift = mod_ref[0, :, 0:D]
        scale = mod_ref[0, :, D:2 * D]
        h_scr[...] = ((x * r) * g_ref[...] * (1.0 + scale) + shift).astype(bf16)

    o_ref[0] = jnp.dot(h_scr[...], w_ref[...], preferred_element_type=f32).astype(o_ref.dtype)


def _inproj(x, mod, g_pre, w, tm, tn):
    B, L, D = x.shape
    N = w.shape[1]
    return pl.pallas_call(
        _inproj_kernel,
        grid=(B, L // tm, N // tn),
        in_specs=[pl.BlockSpec((1, tm, D), lambda b, m, n: (b, m, 0)),
                  pl.BlockSpec((1, 1, 3 * D), lambda b, m, n: (b, 0, 0)),
                  pl.BlockSpec((1, D), lambda b, m, n: (0, 0)),
                  pl.BlockSpec((D, tn), lambda b, m, n: (0, n))],
        out_specs=pl.BlockSpec((1, tm, tn), lambda b, m, n: (b, m, n)),
        out_shape=jax.ShapeDtypeStruct((B, L, N), bf16),
        scratch_shapes=[pltpu.VMEM((tm, D), bf16)],
        compiler_params=_params(("parallel", "parallel", "arbitrary")),
    )(x, mod, g_pre.reshape(1, D), w)


def _rope(x, cos, sin):
    lane = lax.broadcasted_iota(jnp.int32, x.shape, 1)
    first = (lane & (AXIS_DIM // 2)) == 0
    rot = jnp.where(first, pltpu.roll(x, HEAD_DIM - AXIS_DIM // 2, 1), pltpu.roll(x, AXIS_DIM // 2, 1))
    return x * cos + rot * sin


def _head_norm(x, gain):
    r = lax.rsqrt(jnp.mean(x * x, axis=-1, keepdims=True) + RMS_EPS)
    return x * r * gain


def _kvprep_kernel(kv_ref, cos_ref, sin_ref, gain_ref, *rest):
    k_ref, v1_ref = rest[-2:]
    kv = kv_ref[0]
    for h in range(ATT_KV_HEADS):
        sl = slice(h * HEAD_DIM, (h + 1) * HEAD_DIM)
        kn = _head_norm(kv[:, sl].astype(f32), gain_ref[...])
        k_ref[0, :, sl] = _rope(kn, cos_ref[...], sin_ref[...]).astype(bf16)
        v = kv[:, KV_WIDTH + h * HEAD_DIM:KV_WIDTH + (h + 1) * HEAD_DIM]
        v1_ref[0, :, 2 * h * HEAD_DIM:(2 * h + 1) * HEAD_DIM] = v
        v1_ref[0, :, (2 * h + 1) * HEAD_DIM:(2 * h + 2) * HEAD_DIM] = jnp.ones_like(v)


def _kvprep(src, col_off, cos, sin, gain, Lk, row_off, tm, into=None):
    B, n, _ = src.shape
    w = 2 * KV_WIDTH
    cb, rb = col_off // w, row_off // tm
    in_specs = [pl.BlockSpec((1, tm, w), lambda b, m: (b, m, cb)),
                pl.BlockSpec((tm, HEAD_DIM), lambda b, m: (m, 0)),
                pl.BlockSpec((tm, HEAD_DIM), lambda b, m: (m, 0)),
                pl.BlockSpec((1, HEAD_DIM), lambda b, m: (0, 0))]
    ops = [src, cos, sin, gain.reshape(1, HEAD_DIM)]
    aliases = {}
    if into is not None:
        in_specs += [pl.BlockSpec(memory_space=pl.ANY)] * 2
        ops += list(into)
        aliases = {4: 0, 5: 1}
    return pl.pallas_call(
        _kvprep_kernel,
        grid=(B, n // tm),
        in_specs=in_specs,
        out_specs=[pl.BlockSpec((1, tm, KV_WIDTH), lambda b, m: (b, rb + m, 0)),
                   pl.BlockSpec((1, tm, w), lambda b, m: (b, rb + m, 0))],
        out_shape=[jax.ShapeDtypeStruct((B, Lk, KV_WIDTH), bf16),
                   jax.ShapeDtypeStruct((B, Lk, w), bf16)],
        input_output_aliases=aliases,
        compiler_params=_params(("parallel", "parallel")),
    )(*ops)


def _attn_kernel(q_ref, k_ref, v1_ref, cos_ref, sin_ref, gain_ref, gate_ref, o_ref):
    k = k_ref[0]
    v1 = v1_ref[0]
    tq = q_ref.shape[1]
    sub = min(tq, ATT_ROWS)
    units = [(slice(r * sub, (r + 1) * sub), slice(h * HEAD_DIM, (h + 1) * HEAD_DIM))
             for r in range(tq // sub) for h in range(ATT_GROUP)]

    def scores(unit):
        rows, sl = unit
        qn = _head_norm(q_ref[0, rows, sl].astype(f32), gain_ref[...])
        q = (_rope(qn, cos_ref[rows, :], sin_ref[rows, :]) * _SM_SCALE_LOG2E).astype(bf16)
        return lax.dot_general(q, k, (((1,), (1,)), ((), ())), preferred_element_type=f32)

    pending = [scores(u) for u in units[:ATT_AHEAD]]
    for i, (rows, sl) in enumerate(units):
        s = pending.pop(0)
        if i + ATT_AHEAD < len(units):
            pending.append(scores(units[i + ATT_AHEAD]))
        p = jnp.exp2(s - jnp.max(s, axis=-1, keepdims=True))
        ol = jnp.dot(p.astype(bf16), v1, preferred_element_type=f32)
        o = ol[:, :HEAD_DIM] / ol[:, HEAD_DIM:HEAD_DIM + 1]
        o_ref[0, rows, sl] = (o * _silu(gate_ref[0, rows, sl].astype(f32))).astype(bf16)


def _attention(proj, k_all, v1_all, Lk, kblk, cos, sin, gain, tq):
    B, L, _ = proj.shape
    gw = ATT_GROUP * HEAD_DIM
    qb, gb = P_Q // gw, P_ATT_GATE // gw
    return pl.pallas_call(
        _attn_kernel,
        grid=(B, ATT_KV_HEADS, L // tq),
        in_specs=[pl.BlockSpec((1, tq, gw), lambda b, h, i: (b, i, qb + h)),
                  pl.BlockSpec((1, Lk, HEAD_DIM), lambda b, h, i: (b, kblk, h)),
                  pl.BlockSpec((1, Lk, 2 * HEAD_DIM), lambda b, h, i: (b, kblk, h)),
                  pl.BlockSpec((tq, HEAD_DIM), lambda b, h, i: (i, 0)),
                  pl.BlockSpec((tq, HEAD_DIM), lambda b, h, i: (i, 0)),
                  pl.BlockSpec((1, HEAD_DIM), lambda b, h, i: (0, 0)),
                  pl.BlockSpec((1, tq, gw), lambda b, h, i: (b, i, gb + h))],
        out_specs=pl.BlockSpec((1, tq, gw), lambda b, h, i: (b, i, h)),
        out_shape=jax.ShapeDtypeStruct((B, L, ATT_WIDTH), bf16),
        compiler_params=_params(("parallel", "parallel", "parallel")),
    )(proj, k_all, v1_all, cos, sin, gain.reshape(1, HEAD_DIM), proj)


def _gmlp_tile(p, gate, lng, lnb, ws_ref, bs_ref):
    uv = 0.5 * p * (1.0 + lax.erf(p * (2.0 ** -0.5)))
    u = uv[:, :GM_WIDTH]
    v = uv[:, GM_WIDTH:]
    vc = v - jnp.mean(v, axis=-1, keepdims=True)
    var = jnp.mean(vc * vc, axis=-1, keepdims=True)
    vn = (vc * lax.rsqrt(var + LN_EPS) * lng + lnb).astype(bf16)
    ug = u * _silu(gate)
    chunks = []
    for n in range(p.shape[0] // CHUNK):
        rows = slice(n * CHUNK, (n + 1) * CHUNK)
        groups = []
        for g in range(GM_GROUPS):
            cols = slice(g * GM_GROUP_DIM, (g + 1) * GM_GROUP_DIM)
            y = jnp.dot(ws_ref[g], vn[rows, cols], preferred_element_type=f32) + bs_ref[:, cols]
            groups.append((ug[rows, cols] * y).astype(bf16))
        chunks.append(jnp.concatenate(groups, axis=1))
    return jnp.concatenate(chunks, axis=0)


def _shortconv_kernel(p_ref, w_ref, b_ref, o_ref, s_scr):
    L = p_ref.shape[1]
    lo, hi = SUBLANES, SUBLANES + L
    zero = jnp.zeros((SUBLANES, LANES), f32)
    for t in range(s_scr.shape[0]):
        cols = slice(t * LANES, (t + 1) * LANES)
        s_scr[t, 0:lo, :] = zero
        s_scr[t, hi:hi + SUBLANES, :] = zero
        s_scr[t, lo:hi, :] = p_ref[0, :, cols].astype(f32)
        y = (s_scr[t, lo - 1:hi - 1, :] * w_ref[0:1, cols] + s_scr[t, lo:hi, :] * w_ref[1:2, cols]
             + s_scr[t, lo + 1:hi + 1, :] * w_ref[2:3, cols] + b_ref[:, cols])
        o_ref[0, :, cols] = y.astype(bf16)


def _shortconv(proj, w, b, tc):
    B, L, _ = proj.shape
    W = w.shape[1]
    return pl.pallas_call(
        _shortconv_kernel,
        grid=(B, W // tc),
        in_specs=[pl.BlockSpec((1, L, tc), lambda b, c: (b, 0, c)),
                  pl.BlockSpec((3, tc), lambda b, c: (0, c)),
                  pl.BlockSpec((1, tc), lambda b, c: (0, c))],
        out_specs=pl.BlockSpec((1, L, tc), lambda b, c: (b, 0, c)),
        out_shape=jax.ShapeDtypeStruct((B, L, W), bf16),
        scratch_shapes=[pltpu.VMEM((tc // LANES, L + 2 * SUBLANES, LANES), f32)],
        compiler_params=_params(("parallel", "parallel")),
    )(proj, w, b.reshape(1, W))


def _filter_embedding(L):
    pos = np.arange(L)
    bands = np.linspace(1e-4, HY_BANDS - 1, HY_BANDS)
    ang = (2.0 * math.pi / L) * pos[:, None] * bands[None, :]
    z = np.zeros((L, LANES), np.float64)
    z[:, 0] = np.linspace(0.0, 1.0, L)
    z[:, 1:1 + HY_BANDS] = np.cos(ang)
    z[:, 1 + HY_BANDS:HY_EMB] = -np.sin(ang)
    z[:, HY_EMB] = (pos == 0)
    return jnp.asarray(z, f32)


def _filter_kernel(z_ref, w1_ref, b1_ref, w2_ref, b2_ref, freq_ref, w3_ref, dec_ref, bias_ref, o_ref):
    hp = lax.Precision.HIGHEST
    C = o_ref.shape[3]
    z = z_ref[...]
    t = z[:, 0:1]
    m0 = z[:, HY_EMB:HY_EMB + 1]
    h = jnp.sin(freq_ref[0:1, :] * (jnp.dot(z, w1_ref[...], precision=hp, preferred_element_type=f32) + b1_ref[...]))
    h = jnp.sin(freq_ref[1:2, :] * (jnp.dot(h, w2_ref[...], precision=hp, preferred_element_type=f32) + b2_ref[...]))
    hi = h.astype(bf16)
    lo = (h - hi.astype(f32)).astype(bf16)
    h3 = jnp.concatenate([hi, lo, hi], axis=1)
    for blk in range(2 * HY_ORDER):
        cols = slice(blk * C, (blk + 1) * C)
        taps = jnp.dot(h3, w3_ref[:, cols], preferred_element_type=f32) * jnp.exp(-t * jnp.abs(dec_ref[:, cols]))
        if blk % 2 == 0:
            taps = taps + m0 * bias_ref[blk // 2:blk // 2 + 1, :]
        o_ref[blk // 2, blk % 2] = taps


def _filters(L, w1, b1, w2, b2, w3, freq, decay, bias, tr):
    H = HY_FILTER_HIDDEN
    C = HY_WIDTH
    W = 2 * HY_ORDER * C
    z = _filter_embedding(L)
    w1p = jnp.zeros((LANES, H), f32).at[:HY_EMB].set(w1)
    w3_hi = w3.astype(bf16)
    w3_lo = (w3 - w3_hi.astype(f32)).astype(bf16)
    w3 = jnp.concatenate([w3_hi, w3_hi, w3_lo], axis=0)
    full = lambda shape: pl.BlockSpec(shape, lambda r: (0,) * len(shape))
    return pl.pallas_call(
        _filter_kernel,
        grid=(L // tr,),
        in_specs=[pl.BlockSpec((tr, LANES), lambda r: (r, 0)),
                  full((LANES, H)), full((1, H)), full((H, H)), full((1, H)), full((2, H)),
                  full((3 * H, W)), full((1, W)), full((HY_ORDER, C))],
        out_specs=pl.BlockSpec((HY_ORDER, 2, tr, C), lambda r: (0, 0, r, 0)),
        out_shape=jax.ShapeDtypeStruct((HY_ORDER, 2, L, C), f32),
        compiler_params=_params(("parallel",)),
    )(z, w1p, b1.reshape(1, H), w2, b2.reshape(1, H), freq, w3, decay.reshape(1, W), bias)


def _fft_dims(L):
    n2 = SUBLANES
    while (2 * n2) * (2 * n2) <= L:
        n2 *= 2
    return n2, L // n2, 2 * (L // n2), n2 + SUBLANES


def _a_pitch(L):
    return _fft_dims(L)[2] + SUBLANES


def _stack(fr, fi):
    return np.block([[fr, -fi], [fi, fr]])


@functools.lru_cache(maxsize=None)
def _dft_tables(L):
    N2, N1h, N1, _ = _fft_dims(L)
    N = 2 * L
    k1 = np.arange(N1)
    a1 = -2.0 * np.pi * np.outer(k1, np.arange(N1)) / N1
    f1 = _stack(np.cos(a1[:, :N1h]), np.sin(a1[:, :N1h]))
    f1k = np.concatenate([np.cos(a1[:, :N1h]), np.sin(a1[:, :N1h])], axis=0)
    f3 = _stack(np.cos(-a1[:, :N1h].T), np.sin(-a1[:, :N1h].T))
    n2 = np.arange(N2)
    a2 = -2.0 * np.pi * (np.outer(n2, n2)[None] / N2 + (k1[:, None, None] * n2[None, None, :]) / N)
    f2 = np.stack([_stack(np.cos(a), np.sin(a)) for a in a2])
    a2t = -np.transpose(a2, (0, 2, 1))
    g2 = np.stack([_stack(np.cos(a), np.sin(a)) for a in a2t])
    return tuple(jnp.asarray(m, bf16) for m in (f1, f1k, f2, g2, f3))


def _load_cols(ref, lead, rows):
    return jnp.concatenate([ref[lead + (t, rows, slice(None))] for t in range(ref.shape[-3])], axis=1)


def _store_cols(ref, lead, rows, val):
    for t in range(ref.shape[-3]):
        ref[lead + (t, rows, slice(None))] = val[:, t * LANES:(t + 1) * LANES]


def _grouped_loop(n, group, load, compute, store):
    group = min(group, n)
    assert n % group == 0

    def body(i, c):
        idx = [i * group + u for u in range(group)]
        vals = [load(ix) for ix in idx]
        outs = [compute(ix, v) for ix, v in zip(idx, vals)]
        for ix, o in zip(idx, outs):
            store(ix, o)
        return c

    lax.fori_loop(0, n // group, body, 0)


def _load_ri(ref, rows):
    return jnp.concatenate([_load_cols(ref, (ri,), rows) for ri in range(2)], axis=0).astype(bf16)


def _store_ri(ref, rows, val):
    h = val.shape[0] // 2
    for ri in range(2):
        _store_cols(ref, (ri,), rows, val[ri * h:(ri + 1) * h])


def _spectrum_kernel(k_ref, f1k_ref, f2_ref, o_ref, a_scr, *, L):
    N2, N1h, N1, P = _fft_dims(L)
    PA = _a_pitch(L)
    inv_n = 1.0 / (2 * L)

    def load1(n2):
        rows = pl.ds(n2, N1h, stride=N2)
        return jnp.concatenate([k_ref[0, 0, rows, :], k_ref[0, 1, rows, :]], axis=1).astype(bf16)

    _grouped_loop(N2, FFT_GROUP, load1,
                  lambda n2, w: jnp.dot(f1k_ref[...], w, preferred_element_type=f32),
                  lambda n2, r: _store_ri(a_scr, pl.ds(pl.multiple_of(n2 * PA, SUBLANES), N1), r))

    def store2(k1, x):
        out_rows = pl.ds(pl.multiple_of(k1 * N2, N2), N2)
        o_ref[0, 0, out_rows, :] = (x[:N2, :LANES] + x[:N2, LANES:]).astype(bf16)
        o_ref[0, 1, out_rows, :] = (x[N2:, :LANES] - x[N2:, LANES:]).astype(bf16)

    _grouped_loop(N1, FFT_GROUP,
                  lambda k1: _load_ri(a_scr, pl.ds(k1, N2, stride=PA)),
                  lambda k1, a: jnp.dot(f2_ref[k1], a, preferred_element_type=f32) * inv_n,
                  store2)


def _spectrum(taps):
    _, _, L, C = taps.shape
    N2, N1h, N1, P = _fft_dims(L)
    _, f1k, f2, _, _ = _dft_tables(L)
    return pl.pallas_call(
        functools.partial(_spectrum_kernel, L=L),
        grid=(HY_ORDER, C // LANES),
        in_specs=[pl.BlockSpec((1, 2, L, LANES), lambda o, c: (o, 0, 0, c)),
                  pl.BlockSpec(f1k.shape, lambda o, c: (0, 0)),
                  pl.BlockSpec(f2.shape, lambda o, c: (0, 0, 0))],
        out_specs=pl.BlockSpec((1, 2, 2 * L, LANES), lambda o, c: (o, 0, 0, c)),
        out_shape=jax.ShapeDtypeStruct((HY_ORDER, 2, 2 * L, C), bf16),
        scratch_shapes=[pltpu.VMEM((2, 2, N2 * _a_pitch(L), LANES), f32)],
        compiler_params=_params(("parallel", "parallel")),
    )(taps, f1k, f2)


def _fftconv_kernel(*refs, L, chunk, has_gate):
    if has_gate:
        u_in, g_ref, gate_ref, kf_ref, f1_ref, f2_ref, g2_ref, f3_ref, o_ref, z_scr, a_scr = refs
    else:
        u_in, g_ref, kf_ref, f1_ref, f2_ref, g2_ref, f3_ref, o_ref, z_scr, a_scr = refs
    N2, N1h, N1, P = _fft_dims(L)
    J = L // chunk
    nb = chunk // N2
    j = pl.program_id(2)

    @pl.when(j < J)
    def _load():
        for ri in range(2):
            for blk in range(nb):
                rows = pl.ds(pl.multiple_of((j * nb + blk) * P, SUBLANES), N2)
                _store_cols(z_scr, (ri,), rows, u_in[ri, blk * N2:(blk + 1) * N2, :].astype(f32))

    @pl.when(j == J - 1)
    def _transform():
        PA = _a_pitch(L)

        def a_block(n2):
            return pl.ds(pl.multiple_of(n2 * PA, SUBLANES), N1)

        _grouped_loop(N2, FFT_GROUP,
                      lambda n2: _load_ri(z_scr, pl.ds(n2, N1h, stride=P)),
                      lambda n2, w: jnp.dot(f1_ref[...], w, preferred_element_type=f32),
                      lambda n2, r: _store_ri(a_scr, a_block(n2), r))

        def block_rows(k1):
            return pl.ds(k1, N2, stride=PA)

        def middle(k1, a):
            x = jnp.dot(f2_ref[k1], a, preferred_element_type=f32)
            krows = pl.ds(pl.multiple_of(k1 * N2, N2), N2)
            kr = kf_ref[0, 0, krows, :].astype(f32)
            ki = kf_ref[0, 1, krows, :].astype(f32)
            xr, xi = x[:N2], x[N2:]
            y = jnp.concatenate([xr * kr - xi * ki, xr * ki + xi * kr], axis=0).astype(bf16)
            return jnp.dot(g2_ref[k1], y, preferred_element_type=f32)

        _grouped_loop(N1, FFT_GROUP,
                      lambda k1: _load_ri(a_scr, block_rows(k1)),
                      middle,
                      lambda k1, b: _store_ri(a_scr, block_rows(k1), b))

        _grouped_loop(N2, FFT_GROUP,
                      lambda n2: _load_ri(a_scr, a_block(n2)),
                      lambda n2, b: jnp.dot(f3_ref[...], b, preferred_element_type=f32),
                      lambda n2, y: _store_ri(z_scr, pl.ds(n2, N1h, stride=P), y))

    @pl.when(j >= J)
    def _epilogue():
        for ri in range(2):
            for blk in range(nb):
                rows = pl.ds(pl.multiple_of(((j - J) * nb + blk) * P, SUBLANES), N2)
                y = _load_cols(z_scr, (ri,), rows)
                sl = slice(blk * N2, (blk + 1) * N2)
                o = g_ref[ri, sl, :].astype(f32) * y
                if has_gate:
                    o = o * _silu(gate_ref[ri, sl, :].astype(f32))
                o_ref[ri, sl, :] = o.astype(bf16)


def _fftconv(u, u_cb, g, g_cb, gate, gate_cb, kf, order, tc):
    B, L, _ = u.shape
    C = HY_WIDTH
    N2, N1h, N1, P = _fft_dims(L)
    chunk = min(L, FFT_CHUNK)
    J = L // chunk
    f1, _, f2, g2, f3 = _dft_tables(L)
    T = tc // LANES
    in_map = lambda cb: (lambda c, p, j: (p, jnp.minimum(j, J - 1), cb + c))
    ep_map = lambda cb: (lambda c, p, j: (p, jnp.maximum(j - J, 0), cb + c))
    const = lambda nd: (lambda c, p, j: (0,) * nd)
    once = pl.Buffered(1)
    ops = [u, g]
    specs = [pl.BlockSpec((2, chunk, tc), in_map(u_cb)),
             pl.BlockSpec((2, chunk, tc), ep_map(g_cb))]
    if gate is not None:
        ops.append(gate)
        specs.append(pl.BlockSpec((2, chunk, tc), ep_map(gate_cb)))
    ops += [kf, f1, f2, g2, f3]
    specs += [pl.BlockSpec((1, 2, 2 * L, tc), lambda c, p, j: (order, 0, 0, c), pipeline_mode=once),
              pl.BlockSpec(f1.shape, const(2), pipeline_mode=once),
              pl.BlockSpec(f2.shape, const(3), pipeline_mode=once),
              pl.BlockSpec(g2.shape, const(3), pipeline_mode=once),
              pl.BlockSpec(f3.shape, const(2), pipeline_mode=once)]
    return pl.pallas_call(
        functools.partial(_fftconv_kernel, L=L, chunk=chunk, has_gate=gate is not None),
        grid=(C // tc, B // 2, 2 * J),
        in_specs=specs,
        out_specs=pl.BlockSpec((2, chunk, tc), ep_map(0)),
        out_shape=jax.ShapeDtypeStruct((B, L, C), bf16),
        scratch_shapes=[pltpu.VMEM((2, T, N1h * P, LANES), f32),
                        pltpu.VMEM((2, T, N2 * _a_pitch(L), LANES), f32)],
        compiler_params=_params(("parallel", "parallel", "arbitrary"), FFT_VMEM_LIMIT),
    )(*ops)


def _merge_kernel(hy_ref, at_ref, gmp_ref, gmg_ref, lng_ref, lnb_ref, ws_ref, bs_ref, mg_ref, x_ref, gate_ref,
                  gpost_ref, why_ref, wat_ref, wgm_ref, wout_ref, o_ref):
    D = x_ref.shape[2]
    tm = x_ref.shape[1]
    sub = min(tm, MERGE_ROWS)
    for r0 in range(0, tm, sub):
        rows = slice(r0, r0 + sub)

        def gated(i, br, w):
            y = jnp.dot(br, w[...], preferred_element_type=f32)
            return jax.nn.sigmoid(mg_ref[0, rows, i * D:(i + 1) * D].astype(f32)) * y

        gm = _gmlp_tile(gmp_ref[0, rows, :].astype(f32), gmg_ref[0, rows, :].astype(f32), lng_ref[...],
                        lnb_ref[...], ws_ref, bs_ref)
        acc = gated(0, hy_ref[0, rows, :], why_ref) + gated(1, at_ref[0, rows, :], wat_ref) + gated(2, gm, wgm_ref)
        o = jnp.dot(acc.astype(bf16), wout_ref[...], preferred_element_type=f32)
        r = o * lax.rsqrt(jnp.mean(o * o, axis=-1, keepdims=True) + RMS_EPS) * gpost_ref[...]
        o_ref[0, rows, :] = x_ref[0, rows, :] + gate_ref[0] * r


def _merge(hy, at, proj, x, mod, lp, tm):
    B, L, D = x.shape
    mb, pb, gb = P_MERGE // (3 * D), P_GM // (2 * GM_WIDTH), P_GM_GATE // GM_WIDTH
    row = lambda b, m: (b, m, 0)
    vec = pl.BlockSpec((1, D), lambda b, m: (0, 0))
    wspec = pl.BlockSpec((D, D), lambda b, m: (0, 0))
    return pl.pallas_call(
        _merge_kernel,
        grid=(B, L // tm),
        in_specs=[pl.BlockSpec((1, tm, D), row), pl.BlockSpec((1, tm, D), row),
                  pl.BlockSpec((1, tm, 2 * GM_WIDTH), lambda b, m: (b, m, pb)),
                  pl.BlockSpec((1, tm, GM_WIDTH), lambda b, m: (b, m, gb)),
                  vec, vec,
                  pl.BlockSpec((GM_GROUPS, CHUNK, CHUNK), lambda b, m: (0, 0, 0)),
                  pl.BlockSpec((CHUNK, GM_WIDTH), lambda b, m: (0, 0)),
                  pl.BlockSpec((1, tm, 3 * D), lambda b, m: (b, m, mb)),
                  pl.BlockSpec((1, tm, D), row),
                  pl.BlockSpec((1, 1, D), lambda b, m: (b, 0, 2)),
                  vec, wspec, wspec, wspec, wspec],
        out_specs=pl.BlockSpec((1, tm, D), row),
        out_shape=jax.ShapeDtypeStruct((B, L, D), f32),
        compiler_params=_params(("parallel", "parallel")),
    )(hy, at, proj, proj, lp['gm_ln_g'].reshape(1, D), lp['gm_ln_b'].reshape(1, D), lp['gm_ws'], lp['gm_bs_full'],
      proj, x, mod, lp['g_post'].reshape(1, D), lp['w_hy_o'], lp['w_att_o'], lp['w_gm_o'], lp['w_out'])


def _rope_tables(L):
    pos = np.arange(L)
    inv = np.power(ROPE_THETA, -np.arange(0, AXIS_DIM, 2) / AXIS_DIM)
    ar = (pos // GRID_W)[:, None] * inv
    ac = (pos % GRID_W)[:, None] * inv
    cos = np.concatenate([np.cos(ar), np.cos(ar), np.cos(ac), np.cos(ac)], axis=1)
    sin = np.concatenate([-np.sin(ar), np.sin(ar), -np.sin(ac), np.sin(ac)], axis=1)
    return jnp.asarray(cos, f32), jnp.asarray(sin, f32)


def _identity_rope(L):
    return jnp.ones((L, HEAD_DIM), f32), jnp.zeros((L, HEAD_DIM), f32)


def _hyena(proj, lp):
    B, L, _ = proj.shape
    tc = FFT_COLS
    nct = HY_WIDTH // tc
    hyc = _shortconv(proj, lp['hy_conv_w'], lp['hy_conv_b'], tc)
    taps = _filters(L, lp['hy_w1'], lp['hy_b1'], lp['hy_w2'], lp['hy_b2'], lp['hy_w3'], lp['hy_freq'],
                    lp['hy_decay'], lp['hy_bias'], min(L, 256))
    kf = _spectrum(taps)
    z1 = _fftconv(hyc, 2 * nct, hyc, 0, None, 0, kf, 0, tc)
    return _fftconv(z1, 0, hyc, nct, proj, P_HY_GATE // tc, kf, 1, tc)


def _mixer(proj, k_all, v1_all, Lk, kblk, rope_q, lp, tq):
    att = _attention(proj, k_all, v1_all, Lk, kblk, rope_q[0], rope_q[1], lp['q_gain'], tq)
    return _hyena(proj, lp), att


def _layer(x, xc, mod_x, mod_c, rope, lp, ctx_out):
    B, L, D = x.shape
    C = xc.shape[1]
    w_in = lp['w_in']
    proj = _inproj(x, mod_x, lp['g_pre'], w_in, min(L, 1024), INPROJ_COLS)
    wc = w_in if ctx_out else w_in[:, P_K:]
    projc = _inproj(xc.reshape(1, B * C, D), mod_c[:1], lp['g_pre'], wc, min(B * C, 1024),
                    min(wc.shape[1], INPROJ_COLS)).reshape(B, C, wc.shape[1])
    ident = _identity_rope(C)
    kv = _kvprep(proj, P_K, rope[0], rope[1], lp['k_gain'], L + C, 0, min(L, 1024))
    k_all, v1_all = _kvprep(projc, P_K if ctx_out else 0, ident[0], ident[1], lp['k_gain'], L + C, L, C, into=kv)

    hy, att = _mixer(proj, k_all, v1_all, C + L, 0, rope, lp, min(L, ATT_TQ))
    x_new = _merge(hy, att, proj, x, mod_x, lp, min(L, 512))
    if not ctx_out:
        return x_new, xc
    hyc, attc = _mixer(projc, k_all, v1_all, C, L // C, ident, lp, C)
    return x_new, _merge(hyc, attc, projc, xc, mod_c, lp, C)


def kernel(x, c, ctx, c_ctx, w_mod, b_mod, g_pre, g_post, w_in, hy_conv_w, hy_conv_b, hy_w1, hy_b1, hy_w2, hy_b2, hy_w3, hy_freq, hy_decay, hy_bias, q_gain, k_gain, gm_ln_g, gm_ln_b, gm_ws, gm_bs, w_hy_o, w_att_o, w_gm_o, w_out):
    B, L, D = x.shape
    rope = _rope_tables(L)

    R = -(-(B + 1) // (2 * SUBLANES)) * (2 * SUBLANES)
    cc = jnp.zeros((R, D), f32).at[:B].set(c).at[B].set(c_ctx)
    mod = _modulation(cc, w_mod, b_mod)

    w_in_p = jnp.concatenate([w_in[:, :, a:b].astype(bf16) for a, b in _COL_RANGES], axis=2)
    gm_bs_full = jnp.repeat(jnp.swapaxes(gm_bs, 1, 2), GM_GROUP_DIM, axis=2)

    xc = ctx
    for i in range(DEPTH):
        lp = {
            'g_pre': g_pre[i], 'g_post': g_post[i], 'w_in': w_in_p[i],
            'hy_conv_w': hy_conv_w[i], 'hy_conv_b': hy_conv_b[i],
            'hy_w1': hy_w1[i], 'hy_b1': hy_b1[i], 'hy_w2': hy_w2[i], 'hy_b2': hy_b2[i],
            'hy_w3': hy_w3[i], 'hy_freq': hy_freq[i], 'hy_decay': hy_decay[i], 'hy_bias': hy_bias[i],
            'q_gain': q_gain[i], 'k_gain': k_gain[i], 'gm_ln_g': gm_ln_g[i], 'gm_ln_b': gm_ln_b[i],
            'gm_ws': gm_ws[i].astype(bf16), 'gm_bs_full': gm_bs_full[i],
            'w_hy_o': w_hy_o[i].astype(bf16), 'w_att_o': w_att_o[i].astype(bf16),
            'w_gm_o': w_gm_o[i].astype(bf16), 'w_out': w_out[i].astype(bf16),
        }
        mod_x = mod[i, :B].reshape(B, 1, 3 * D)
        mod_c = jnp.broadcast_to(mod[i, B].reshape(1, 1, 3 * D), (B, 1, 3 * D))
        x, xc = _layer(x, xc, mod_x, mod_c, rope, lp, i < DEPTH - 1)
    return x
```

```python
import functools
import math

import numpy as np
import jax
import jax.numpy as jnp
from jax import lax
from jax.experimental import pallas as pl
from jax.experimental.pallas import tpu as pltpu

f32 = jnp.float32
bf16 = jnp.bfloat16

D_MODEL = 1024
DEPTH = 2
GRID_W = 64
RMS_EPS = 1e-6
LN_EPS = 1e-5

HY_WIDTH = D_MODEL
HY_ORDER = 2
HY_BANDS = 16
HY_EMB = 1 + 2 * HY_BANDS
HY_FILTER_HIDDEN = 64

HEAD_DIM = 128
ATT_HEADS = D_MODEL // HEAD_DIM
ATT_KV_HEADS = 2
ATT_GROUP = ATT_HEADS // ATT_KV_HEADS
ATT_WIDTH = ATT_HEADS * HEAD_DIM
KV_WIDTH = ATT_KV_HEADS * HEAD_DIM
AXIS_DIM = HEAD_DIM // 2
ROPE_THETA = 10000.0

GM_WIDTH = D_MODEL
GM_GROUPS = 8
GM_GROUP_DIM = GM_WIDTH // GM_GROUPS
CHUNK = 128

OFF_HY = 0
OFF_HY_GATE = OFF_HY + (HY_ORDER + 1) * HY_WIDTH
OFF_Q = OFF_HY_GATE + HY_WIDTH
OFF_K = OFF_Q + ATT_WIDTH
OFF_V = OFF_K + KV_WIDTH
OFF_ATT_GATE = OFF_V + KV_WIDTH
OFF_GM = OFF_ATT_GATE + ATT_WIDTH
OFF_GM_GATE = OFF_GM + 2 * GM_WIDTH
OFF_MERGE = OFF_GM_GATE + GM_WIDTH
IN_WIDTH = OFF_MERGE + 3 * D_MODEL

P_HY = 0
P_HY_GATE = 3 * D_MODEL
P_Q = 4 * D_MODEL
P_ATT_GATE = 5 * D_MODEL
P_GM = 6 * D_MODEL
P_GM_GATE = 8 * D_MODEL
P_MERGE = 9 * D_MODEL
P_K = 12 * D_MODEL
P_V = P_K + KV_WIDTH
_COL_RANGES = ((OFF_HY, OFF_K), (OFF_ATT_GATE, IN_WIDTH), (OFF_K, OFF_ATT_GATE))

LANES = 128
SUBLANES = 8
FFT_COLS = 256
FFT_GROUP = 32
INPROJ_COLS = 2560
ATT_AHEAD = 1
ATT_ROWS = 512
ATT_TQ = 1024
MERGE_ROWS = 256
FFT_CHUNK = 2048
V7X_VMEM_BYTES = 64 * 1024 * 1024
VMEM_LIMIT = V7X_VMEM_BYTES - 12 * 1024 * 1024
FFT_VMEM_LIMIT = V7X_VMEM_BYTES - 6 * 1024 * 1024

_SM_SCALE_LOG2E = (HEAD_DIM ** -0.5) * math.log2(math.e)


def _params(sem, vmem=VMEM_LIMIT):
    return pltpu.CompilerParams(dimension_semantics=sem, vmem_limit_bytes=vmem)


def _silu(x):
    return x * jax.nn.sigmoid(x)


def _mod_kernel(cc_ref, w_ref, b_ref, o_ref):
    s = _silu(cc_ref[...]).astype(bf16)
    o_ref[0] = jnp.dot(s, w_ref[0].astype(bf16), preferred_element_type=f32) + b_ref[0]


def _modulation(cc, w_mod, b_mod):
    R, D = cc.shape
    tn = D
    return pl.pallas_call(
        _mod_kernel,
        grid=(DEPTH, 3 * D // tn),
        in_specs=[pl.BlockSpec((R, D), lambda i, n: (0, 0)),
                  pl.BlockSpec((1, D, tn), lambda i, n: (i, 0, n)),
                  pl.BlockSpec((1, 1, tn), lambda i, n: (i, 0, n))],
        out_specs=pl.BlockSpec((1, R, tn), lambda i, n: (i, 0, n)),
        out_shape=jax.ShapeDtypeStruct((DEPTH, R, 3 * D), f32),
        compiler_params=_params(("parallel", "parallel")),
    )(cc, w_mod, b_mod.reshape(DEPTH, 1, 3 * D))


def _inproj_kernel(x_ref, mod_ref, g_ref, w_ref, o_ref, h_scr):
    D = x_ref.shape[2]

    @pl.when(pl.program_id(2) == 0)
    def _():
        x = x_ref[0]
        r = lax.rsqrt(jnp.mean(x * x, axis=-1, keepdims=True) + RMS_EPS)
        shift = mod_ref[0, :, 0:D]
        scale = mod_ref[0, :, D:2 * D]
        h_scr[...] = ((x * r) * g_ref[...] * (1.0 + scale) + shift).astype(bf16)

    o_ref[0] = jnp.dot(h_scr[...], w_ref[...], preferred_element_type=f32).astype(o_ref.dtype)


def _inproj(x, mod, g_pre, w, tm, tn):
    B, L, D = x.shape
    N = w.shape[1]
    return pl.pallas_call(
        _inproj_kernel,
        grid=(B, L // tm, N // tn),
        in_specs=[pl.BlockSpec((1, tm, D), lambda b, m, n: (b, m, 0)),
                  pl.BlockSpec((1, 1, 3 * D), lambda b, m, n: (b, 0, 0)),
                  pl.BlockSpec((1, D), lambda b, m, n: (0, 0)),
                  pl.BlockSpec((D, tn), lambda b, m, n: (0, n))],
        out_specs=pl.BlockSpec((1, tm, tn), lambda b, m, n: (b, m, n)),
        out_shape=jax.ShapeDtypeStruct((B, L, N), bf16),
        scratch_shapes=[pltpu.VMEM((tm, D), bf16)],
        compiler_params=_params(("parallel", "parallel", "arbitrary")),
    )(x, mod, g_pre.reshape(1, D), w)


def _rope(x, cos, sin):
    lane = lax.broadcasted_iota(jnp.int32, x.shape, 1)
    first = (lane & (AXIS_DIM // 2)) == 0
    rot = jnp.where(first, pltpu.roll(x, HEAD_DIM - AXIS_DIM // 2, 1), pltpu.roll(x, AXIS_DIM // 2, 1))
    return x * cos + rot * sin


def _head_norm(x, gain):
    r = lax.rsqrt(jnp.mean(x * x, axis=-1, keepdims=True) + RMS_EPS)
    return x * r * gain


def _kvprep_kernel(kv_ref, cos_ref, sin_ref, gain_ref, *rest):
    k_ref, v1_ref = rest[-2:]
    kv = kv_ref[0]
    for h in range(ATT_KV_HEADS):
        sl = slice(h * HEAD_DIM, (h + 1) * HEAD_DIM)
        kn = _head_norm(kv[:, sl].astype(f32), gain_ref[...])
        k_ref[0, :, sl] = _rope(kn, cos_ref[...], sin_ref[...]).astype(bf16)
        v = kv[:, KV_WIDTH + h * HEAD_DIM:KV_WIDTH + (h + 1) * HEAD_DIM]
        v1_ref[0, :, 2 * h * HEAD_DIM:(2 * h + 1) * HEAD_DIM] = v
        v1_ref[0, :, (2 * h + 1) * HEAD_DIM:(2 * h + 2) * HEAD_DIM] = jnp.ones_like(v)


def _kvprep(src, col_off, cos, sin, gain, Lk, row_off, tm, into=None):
    B, n, _ = src.shape
    w = 2 * KV_WIDTH
    cb, rb = col_off // w, row_off // tm
    in_specs = [pl.BlockSpec((1, tm, w), lambda b, m: (b, m, cb)),
                pl.BlockSpec((tm, HEAD_DIM), lambda b, m: (m, 0)),
                pl.BlockSpec((tm, HEAD_DIM), lambda b, m: (m, 0)),
                pl.BlockSpec((1, HEAD_DIM), lambda b, m: (0, 0))]
    ops = [src, cos, sin, gain.reshape(1, HEAD_DIM)]
    aliases = {}
    if into is not None:
        in_specs += [pl.BlockSpec(memory_space=pl.ANY)] * 2
        ops += list(into)
        aliases = {4: 0, 5: 1}
    return pl.pallas_call(
        _kvprep_kernel,
        grid=(B, n // tm),
        in_specs=in_specs,
        out_specs=[pl.BlockSpec((1, tm, KV_WIDTH), lambda b, m: (b, rb + m, 0)),
                   pl.BlockSpec((1, tm, w), lambda b, m: (b, rb + m, 0))],
        out_shape=[jax.ShapeDtypeStruct((B, Lk, KV_WIDTH), bf16),
                   jax.ShapeDtypeStruct((B, Lk, w), bf16)],
        input_output_aliases=aliases,
        compiler_params=_params(("parallel", "parallel")),
    )(*ops)


def _attn_kernel(q_ref, k_ref, v1_ref, cos_ref, sin_ref, gain_ref, gate_ref, o_ref):
    k = k_ref[0]
    v1 = v1_ref[0]
    tq = q_ref.shape[1]
    sub = min(tq, ATT_ROWS)
    units = [(slice(r * sub, (r + 1) * sub), slice(h * HEAD_DIM, (h + 1) * HEAD_DIM))
             for r in range(tq // sub) for h in range(ATT_GROUP)]

    def scores(unit):
        rows, sl = unit
        qn = _head_norm(q_ref[0, rows, sl].astype(f32), gain_ref[...])
        q = (_rope(qn, cos_ref[rows, :], sin_ref[rows, :]) * _SM_SCALE_LOG2E).astype(bf16)
        return lax.dot_general(q, k, (((1,), (1,)), ((), ())), preferred_element_type=f32)

    pending = [scores(u) for u in units[:ATT_AHEAD]]
    for i, (rows, sl) in enumerate(units):
        s = pending.pop(0)
        if i + ATT_AHEAD < len(units):
            pending.append(scores(units[i + ATT_AHEAD]))
        p = jnp.exp2(s - jnp.max(s, axis=-1, keepdims=True))
        ol = jnp.dot(p.astype(bf16), v1, preferred_element_type=f32)
        o = ol[:, :HEAD_DIM] / ol[:, HEAD_DIM:HEAD_DIM + 1]
        o_ref[0, rows, sl] = (o * _silu(gate_ref[0, rows, sl].astype(f32))).astype(bf16)


def _attention(proj, k_all, v1_all, Lk, kblk, cos, sin, gain, tq):
    B, L, _ = proj.shape
    gw = ATT_GROUP * HEAD_DIM
    qb, gb = P_Q // gw, P_ATT_GATE // gw
    return pl.pallas_call(
        _attn_kernel,
        grid=(B, ATT_KV_HEADS, L // tq),
        in_specs=[pl.BlockSpec((1, tq, gw), lambda b, h, i: (b, i, qb + h)),
                  pl.BlockSpec((1, Lk, HEAD_DIM), lambda b, h, i: (b, kblk, h)),
                  pl.BlockSpec((1, Lk, 2 * HEAD_DIM), lambda b, h, i: (b, kblk, h)),
                  pl.BlockSpec((tq, HEAD_DIM), lambda b, h, i: (i, 0)),
                  pl.BlockSpec((tq, HEAD_DIM), lambda b, h, i: (i, 0)),
                  pl.BlockSpec((1, HEAD_DIM), lambda b, h, i: (0, 0)),
                  pl.BlockSpec((1, tq, gw), lambda b, h, i: (b, i, gb + h))],
        out_specs=pl.BlockSpec((1, tq, gw), lambda b, h, i: (b, i, h)),
        out_shape=jax.ShapeDtypeStruct((B, L, ATT_WIDTH), bf16),
        compiler_params=_params(("parallel", "parallel", "parallel")),
    )(proj, k_all, v1_all, cos, sin, gain.reshape(1, HEAD_DIM), proj)


def _gmlp_tile(p, gate, lng, lnb, ws_ref, bs_ref):
    uv = 0.5 * p * (1.0 + lax.erf(p * (2.0 ** -0.5)))
    u = uv[:, :GM_WIDTH]
    v = uv[:, GM_WIDTH:]
    vc = v - jnp.mean(v, axis=-1, keepdims=True)
    var = jnp.mean(vc * vc, axis=-1, keepdims=True)
    vn = (vc * lax.rsqrt(var + LN_EPS) * lng + lnb).astype(bf16)
    ug = u * _silu(gate)
    chunks = []
    for n in range(p.shape[0] // CHUNK):
        rows = slice(n * CHUNK, (n + 1) * CHUNK)
        groups = []
        for g in range(GM_GROUPS):
            cols = slice(g * GM_GROUP_DIM, (g + 1) * GM_GROUP_DIM)
            y = jnp.dot(ws_ref[g], vn[rows, cols], preferred_element_type=f32) + bs_ref[:, cols]
            groups.append((ug[rows, cols] * y).astype(bf16))
        chunks.append(jnp.concatenate(groups, axis=1))
    return jnp.concatenate(chunks, axis=0)


def _shortconv_kernel(p_ref, w_ref, b_ref, o_ref, s_scr):
    L = p_ref.shape[1]
    lo, hi = SUBLANES, SUBLANES + L
    zero = jnp.zeros((SUBLANES, LANES), f32)
    for t in range(s_scr.shape[0]):
        cols = slice(t * LANES, (t + 1) * LANES)
        s_scr[t, 0:lo, :] = zero
        s_scr[t, hi:hi + SUBLANES, :] = zero
        s_scr[t, lo:hi, :] = p_ref[0, :, cols].astype(f32)
        y = (s_scr[t, lo - 1:hi - 1, :] * w_ref[0:1, cols] + s_scr[t, lo:hi, :] * w_ref[1:2, cols]
             + s_scr[t, lo + 1:hi + 1, :] * w_ref[2:3, cols] + b_ref[:, cols])
        o_ref[0, :, cols] = y.astype(bf16)


def _shortconv(proj, w, b, tc):
    B, L, _ = proj.shape
    W = w.shape[1]
    return pl.pallas_call(
        _shortconv_kernel,
        grid=(B, W // tc),
        in_specs=[pl.BlockSpec((1, L, tc), lambda b, c: (b, 0, c)),
                  pl.BlockSpec((3, tc), lambda b, c: (0, c)),
                  pl.BlockSpec((1, tc), lambda b, c: (0, c))],
        out_specs=pl.BlockSpec((1, L, tc), lambda b, c: (b, 0, c)),
        out_shape=jax.ShapeDtypeStruct((B, L, W), bf16),
        scratch_shapes=[pltpu.VMEM((tc // LANES, L + 2 * SUBLANES, LANES), f32)],
        compiler_params=_params(("parallel", "parallel")),
    )(proj, w, b.reshape(1, W))


def _filter_embedding(L):
    pos = np.arange(L)
    bands = np.linspace(1e-4, HY_BANDS - 1, HY_BANDS)
    ang = (2.0 * math.pi / L) * pos[:, None] * bands[None, :]
    z = np.zeros((L, LANES), np.float64)
    z[:, 0] = np.linspace(0.0, 1.0, L)
    z[:, 1:1 + HY_BANDS] = np.cos(ang)
    z[:, 1 + HY_BANDS:HY_EMB] = -np.sin(ang)
    z[:, HY_EMB] = (pos == 0)
    return jnp.asarray(z, f32)


def _filter_kernel(z_ref, w1_ref, b1_ref, w2_ref, b2_ref, freq_ref, w3_ref, dec_ref, bias_ref, o_ref):
    hp = lax.Precision.HIGHEST
    C = o_ref.shape[3]
    z = z_ref[...]
    t = z[:, 0:1]
    m0 = z[:, HY_EMB:HY_EMB + 1]
    h = jnp.sin(freq_ref[0:1, :] * (jnp.dot(z, w1_ref[...], precision=hp, preferred_element_type=f32) + b1_ref[...]))
    h = jnp.sin(freq_ref[1:2, :] * (jnp.dot(h, w2_ref[...], precision=hp, preferred_element_type=f32) + b2_ref[...]))
    hi = h.astype(bf16)
    lo = (h - hi.astype(f32)).astype(bf16)
    h3 = jnp.concatenate([hi, lo, hi], axis=1)
    for blk in range(2 * HY_ORDER):
        cols = slice(blk * C, (blk + 1) * C)
        taps = jnp.dot(h3, w3_ref[:, cols], preferred_element_type=f32) * jnp.exp(-t * jnp.abs(dec_ref[:, cols]))
        if blk % 2 == 0:
            taps = taps + m0 * bias_ref[blk // 2:blk // 2 + 1, :]
        o_ref[blk // 2, blk % 2] = taps


def _filters(L, w1, b1, w2, b2, w3, freq, decay, bias, tr):
    H = HY_FILTER_HIDDEN
    C = HY_WIDTH
    W = 2 * HY_ORDER * C
    z = _filter_embedding(L)
    w1p = jnp.zeros((LANES, H), f32).at[:HY_EMB].set(w1)
    w3_hi = w3.astype(bf16)
    w3_lo = (w3 - w3_hi.astype(f32)).astype(bf16)
    w3 = jnp.concatenate([w3_hi, w3_hi, w3_lo], axis=0)
    full = lambda shape: pl.BlockSpec(shape, lambda r: (0,) * len(shape))
    return pl.pallas_call(
        _filter_kernel,
        grid=(L // tr,),
        in_specs=[pl.BlockSpec((tr, LANES), lambda r: (r, 0)),
                  full((LANES, H)), full((1, H)), full((H, H)), full((1, H)), full((2, H)),
                  full((3 * H, W)), full((1, W)), full((HY_ORDER, C))],
        out_specs=pl.BlockSpec((HY_ORDER, 2, tr, C), lambda r: (0, 0, r, 0)),
        out_shape=jax.ShapeDtypeStruct((HY_ORDER, 2, L, C), f32),
        compiler_params=_params(("parallel",)),
    )(z, w1p, b1.reshape(1, H), w2, b2.reshape(1, H), freq, w3, decay.reshape(1, W), bias)


def _fft_dims(L):
    n2 = SUBLANES
    while (2 * n2) * (2 * n2) <= L:
        n2 *= 2
    return n2, L // n2, 2 * (L // n2), n2 + SUBLANES


def _a_pitch(L):
    return _fft_dims(L)[2] + SUBLANES


def _stack(fr, fi):
    return np.block([[fr, -fi], [fi, fr]])


@functools.lru_cache(maxsize=None)
def _dft_tables(L):
    N2, N1h, N1, _ = _fft_dims(L)
    N = 2 * L
    k1 = np.arange(N1)
    a1 = -2.0 * np.pi * np.outer(k1, np.arange(N1)) / N1
    f1 = _stack(np.cos(a1[:, :N1h]), np.sin(a1[:, :N1h]))
    f1k = np.concatenate([np.cos(a1[:, :N1h]), np.sin(a1[:, :N1h])], axis=0)
    f3 = _stack(np.cos(-a1[:, :N1h].T), np.sin(-a1[:, :N1h].T))
    n2 = np.arange(N2)
    a2 = -2.0 * np.pi * (np.outer(n2, n2)[None] / N2 + (k1[:, None, None] * n2[None, None, :]) / N)
    f2 = np.stack([_stack(np.cos(a), np.sin(a)) for a in a2])
    return tuple(jnp.asarray(m, bf16) for m in (f1, f1k, f2, f3))


def _load_cols(ref, lead, rows):
    return jnp.concatenate([ref[lead + (t, rows, slice(None))] for t in range(ref.shape[-3])], axis=1)


def _store_cols(ref, lead, rows, val):
    for t in range(ref.shape[-3]):
        ref[lead + (t, rows, slice(None))] = val[:, t * LANES:(t + 1) * LANES]


def _grouped_loop(n, group, load, compute, store):
    group = min(group, n)
    assert n % group == 0

    def body(i, c):
        idx = [i * group + u for u in range(group)]
        vals = [load(ix) for ix in idx]
        outs = [compute(ix, v) for ix, v in zip(idx, vals)]
        for ix, o in zip(idx, outs):
            store(ix, o)
        return c

    lax.fori_loop(0, n // group, body, 0)


def _load_ri(ref, rows):
    return jnp.concatenate([_load_cols(ref, (ri,), rows) for ri in range(2)], axis=0).astype(bf16)


def _store_ri(ref, rows, val):
    h = val.shape[0] // 2
    for ri in range(2):
        _store_cols(ref, (ri,), rows, val[ri * h:(ri + 1) * h])


def _spectrum_kernel(k_ref, f1k_ref, f2_ref, o_ref, a_scr, *, L):
    N2, N1h, N1, P = _fft_dims(L)
    PA = _a_pitch(L)
    inv_n = 1.0 / (2 * L)

    def load1(n2):
        rows = pl.ds(n2, N1h, stride=N2)
        return jnp.concatenate([k_ref[0, 0, rows, :], k_ref[0, 1, rows, :]], axis=1).astype(bf16)

    _grouped_loop(N2, FFT_GROUP, load1,
                  lambda n2, w: jnp.dot(f1k_ref[...], w, preferred_element_type=f32),
                  lambda n2, r: _store_ri(a_scr, pl.ds(pl.multiple_of(n2 * PA, SUBLANES), N1), r))

    def store2(k1, x):
        out_rows = pl.ds(pl.multiple_of(k1 * N2, N2), N2)
        o_ref[0, 0, out_rows, :] = (x[:N2, :LANES] + x[:N2, LANES:]).astype(bf16)
        o_ref[0, 1, out_rows, :] = (x[N2:, :LANES] - x[N2:, LANES:]).astype(bf16)

    _grouped_loop(N1, FFT_GROUP,
                  lambda k1: _load_ri(a_scr, pl.ds(k1, N2, stride=PA)),
                  lambda k1, a: jnp.dot(f2_ref[k1], a, preferred_element_type=f32) * inv_n,
                  store2)


def _spectrum(taps):
    _, _, L, C = taps.shape
    N2, N1h, N1, P = _fft_dims(L)
    _, f1k, f2, _ = _dft_tables(L)
    return pl.pallas_call(
        functools.partial(_spectrum_kernel, L=L),
        grid=(HY_ORDER, C // LANES),
        in_specs=[pl.BlockSpec((1, 2, L, LANES), lambda o, c: (o, 0, 0, c)),
                  pl.BlockSpec(f1k.shape, lambda o, c: (0, 0)),
                  pl.BlockSpec(f2.shape, lambda o, c: (0, 0, 0))],
        out_specs=pl.BlockSpec((1, 2, 2 * L, LANES), lambda o, c: (o, 0, 0, c)),
        out_shape=jax.ShapeDtypeStruct((HY_ORDER, 2, 2 * L, C), bf16),
        scratch_shapes=[pltpu.VMEM((2, 2, N2 * _a_pitch(L), LANES), f32)],
        compiler_params=_params(("parallel", "parallel")),
    )(taps, f1k, f2)


def _fftconv_kernel(*refs, L, chunk, has_gate):
    if has_gate:
        u_in, g_ref, gate_ref, kf_ref, f1_ref, f2_ref, f3_ref, o_ref, z_scr, a_scr = refs
    else:
        u_in, g_ref, kf_ref, f1_ref, f2_ref, f3_ref, o_ref, z_scr, a_scr = refs
    N2, N1h, N1, P = _fft_dims(L)
    J = L // chunk
    nb = chunk // N2
    j = pl.program_id(2)

    @pl.when(j < J)
    def _load():
        for ri in range(2):
            for blk in range(nb):
                rows = pl.ds(pl.multiple_of((j * nb + blk) * P, SUBLANES), N2)
                _store_cols(z_scr, (ri,), rows, u_in[ri, blk * N2:(blk + 1) * N2, :].astype(f32))

    @pl.when(j == J - 1)
    def _transform():
        PA = _a_pitch(L)

        def a_block(n2):
            return pl.ds(pl.multiple_of(n2 * PA, SUBLANES), N1)

        _grouped_loop(N2, FFT_GROUP,
                      lambda n2: _load_ri(z_scr, pl.ds(n2, N1h, stride=P)),
                      lambda n2, w: jnp.dot(f1_ref[...], w, preferred_element_type=f32),
                      lambda n2, r: _store_ri(a_scr, a_block(n2), r))

        def block_rows(k1):
            return pl.ds(k1, N2, stride=PA)

        def middle(k1, a):
            x = jnp.dot(f2_ref[k1], a, preferred_element_type=f32)
            krows = pl.ds(pl.multiple_of(k1 * N2, N2), N2)
            kr = kf_ref[0, 0, krows, :].astype(f32)
            ki = kf_ref[0, 1, krows, :].astype(f32)
            xr, xi = x[:N2], x[N2:]
            y = jnp.concatenate([xr * kr - xi * ki, xr * ki + xi * kr], axis=0).astype(bf16)
            return lax.dot_general(f2_ref[k1], y, (((0,), (0,)), ((), ())), preferred_element_type=f32)

        _grouped_loop(N1, FFT_GROUP,
                      lambda k1: _load_ri(a_scr, block_rows(k1)),
                      middle,
                      lambda k1, b: _store_ri(a_scr, block_rows(k1), b))

        _grouped_loop(N2, FFT_GROUP,
                      lambda n2: _load_ri(a_scr, a_block(n2)),
                      lambda n2, b: jnp.dot(f3_ref[...], b, preferred_element_type=f32),
                      lambda n2, y: _store_ri(z_scr, pl.ds(n2, N1h, stride=P), y))

    @pl.when(j >= J)
    def _epilogue():
        for ri in range(2):
            for blk in range(nb):
                rows = pl.ds(pl.multiple_of(((j - J) * nb + blk) * P, SUBLANES), N2)
                y = _load_cols(z_scr, (ri,), rows)
                sl = slice(blk * N2, (blk + 1) * N2)
                o = g_ref[ri, sl, :].astype(f32) * y
                if has_gate:
                    o = o * _silu(gate_ref[ri, sl, :].astype(f32))
                o_ref[ri, sl, :] = o.astype(bf16)


def _fftconv(u, u_cb, g, g_cb, gate, gate_cb, kf, order, tc):
    B, L, _ = u.shape
    C = HY_WIDTH
    N2, N1h, N1, P = _fft_dims(L)
    chunk = min(L, FFT_CHUNK)
    J = L // chunk
    f1, _, f2, f3 = _dft_tables(L)
    T = tc // LANES
    in_map = lambda cb: (lambda c, p, j: (p, jnp.minimum(j, J - 1), cb + c))
    ep_map = lambda cb: (lambda c, p, j: (p, jnp.maximum(j - J, 0), cb + c))
    const = lambda nd: (lambda c, p, j: (0,) * nd)
    once = pl.Buffered(1)
    ops = [u, g]
    specs = [pl.BlockSpec((2, chunk, tc), in_map(u_cb)),
             pl.BlockSpec((2, chunk, tc), ep_map(g_cb))]
    if gate is not None:
        ops.append(gate)
        specs.append(pl.BlockSpec((2, chunk, tc), ep_map(gate_cb)))
    ops += [kf, f1, f2, f3]
    specs += [pl.BlockSpec((1, 2, 2 * L, tc), lambda c, p, j: (order, 0, 0, c), pipeline_mode=once),
              pl.BlockSpec(f1.shape, const(2), pipeline_mode=once),
              pl.BlockSpec(f2.shape, const(3), pipeline_mode=once),
              pl.BlockSpec(f3.shape, const(2), pipeline_mode=once)]
    return pl.pallas_call(
        functools.partial(_fftconv_kernel, L=L, chunk=chunk, has_gate=gate is not None),
        grid=(C // tc, B // 2, 2 * J),
        in_specs=specs,
        out_specs=pl.BlockSpec((2, chunk, tc), ep_map(0)),
        out_shape=jax.ShapeDtypeStruct((B, L, C), bf16),
        scratch_shapes=[pltpu.VMEM((2, T, N1h * P, LANES), f32),
                        pltpu.VMEM((2, T, N2 * _a_pitch(L), LANES), f32)],
        compiler_params=_params(("parallel", "parallel", "arbitrary"), FFT_VMEM_LIMIT),
    )(*ops)


def _merge_kernel(hy_ref, at_ref, gmp_ref, gmg_ref, lng_ref, lnb_ref, ws_ref, bs_ref, mg_ref, x_ref, gate_ref,
                  gpost_ref, why_ref, wat_ref, wgm_ref, wout_ref, o_ref):
    D = x_ref.shape[2]
    tm = x_ref.shape[1]
    sub = min(tm, MERGE_ROWS)
    for r0 in range(0, tm, sub):
        rows = slice(r0, r0 + sub)

        def gated(i, br, w):
            y = jnp.dot(br, w[...], preferred_element_type=f32)
            return jax.nn.sigmoid(mg_ref[0, rows, i * D:(i + 1) * D].astype(f32)) * y

        gm = _gmlp_tile(gmp_ref[0, rows, :].astype(f32), gmg_ref[0, rows, :].astype(f32), lng_ref[...],
                        lnb_ref[...], ws_ref, bs_ref)
        acc = gated(0, hy_ref[0, rows, :], why_ref) + gated(1, at_ref[0, rows, :], wat_ref) + gated(2, gm, wgm_ref)
        o = jnp.dot(acc.astype(bf16), wout_ref[...], preferred_element_type=f32)
        r = o * lax.rsqrt(jnp.mean(o * o, axis=-1, keepdims=True) + RMS_EPS) * gpost_ref[...]
        o_ref[0, rows, :] = x_ref[0, rows, :] + gate_ref[0] * r


def _merge(hy, at, proj, x, mod, lp, tm):
    B, L, D = x.shape
    mb, pb, gb = P_MERGE // (3 * D), P_GM // (2 * GM_WIDTH), P_GM_GATE // GM_WIDTH
    row = lambda b, m: (b, m, 0)
    vec = pl.BlockSpec((1, D), lambda b, m: (0, 0))
    wspec = pl.BlockSpec((D, D), lambda b, m: (0, 0))
    return pl.pallas_call(
        _merge_kernel,
        grid=(B, L // tm),
        in_specs=[pl.BlockSpec((1, tm, D), row), pl.BlockSpec((1, tm, D), row),
                  pl.BlockSpec((1, tm, 2 * GM_WIDTH), lambda b, m: (b, m, pb)),
                  pl.BlockSpec((1, tm, GM_WIDTH), lambda b, m: (b, m, gb)),
                  vec, vec,
                  pl.BlockSpec((GM_GROUPS, CHUNK, CHUNK), lambda b, m: (0, 0, 0)),
                  pl.BlockSpec((CHUNK, GM_WIDTH), lambda b, m: (0, 0)),
                  pl.BlockSpec((1, tm, 3 * D), lambda b, m: (b, m, mb)),
                  pl.BlockSpec((1, tm, D), row),
                  pl.BlockSpec((1, 1, D), lambda b, m: (b, 0, 2)),
                  vec, wspec, wspec, wspec, wspec],
        out_specs=pl.BlockSpec((1, tm, D), row),
        out_shape=jax.ShapeDtypeStruct((B, L, D), f32),
        compiler_params=_params(("parallel", "parallel")),
    )(hy, at, proj, proj, lp['gm_ln_g'].reshape(1, D), lp['gm_ln_b'].reshape(1, D), lp['gm_ws'], lp['gm_bs_full'],
      proj, x, mod, lp['g_post'].reshape(1, D), lp['w_hy_o'], lp['w_att_o'], lp['w_gm_o'], lp['w_out'])


def _rope_tables(L):
    pos = np.arange(L)
    inv = np.power(ROPE_THETA, -np.arange(0, AXIS_DIM, 2) / AXIS_DIM)
    ar = (pos // GRID_W)[:, None] * inv
    ac = (pos % GRID_W)[:, None] * inv
    cos = np.concatenate([np.cos(ar), np.cos(ar), np.cos(ac), np.cos(ac)], axis=1)
    sin = np.concatenate([-np.sin(ar), np.sin(ar), -np.sin(ac), np.sin(ac)], axis=1)
    return jnp.asarray(cos, f32), jnp.asarray(sin, f32)


def _identity_rope(L):
    return jnp.ones((L, HEAD_DIM), f32), jnp.zeros((L, HEAD_DIM), f32)


def _hyena(proj, lp):
    B, L, _ = proj.shape
    tc = FFT_COLS
    nct = HY_WIDTH // tc
    hyc = _shortconv(proj, lp['hy_conv_w'], lp['hy_conv_b'], tc)
    taps = _filters(L, lp['hy_w1'], lp['hy_b1'], lp['hy_w2'], lp['hy_b2'], lp['hy_w3'], lp['hy_freq'],
                    lp['hy_decay'], lp['hy_bias'], min(L, 256))
    kf = _spectrum(taps)
    z1 = _fftconv(hyc, 2 * nct, hyc, 0, None, 0, kf, 0, tc)
    return _fftconv(z1, 0, hyc, nct, proj, P_HY_GATE // tc, kf, 1, tc)


def _mixer(proj, k_all, v1_all, Lk, kblk, rope_q, lp, tq):
    att = _attention(proj, k_all, v1_all, Lk, kblk, rope_q[0], rope_q[1], lp['q_gain'], tq)
    return _hyena(proj, lp), att


def _layer(x, xc, mod_x, mod_c, rope, lp, ctx_out):
    B, L, D = x.shape
    C = xc.shape[1]
    w_in = lp['w_in']
    proj = _inproj(x, mod_x, lp['g_pre'], w_in, min(L, 1024), INPROJ_COLS)
    wc = w_in if ctx_out else w_in[:, P_K:]
    projc = _inproj(xc.reshape(1, B * C, D), mod_c[:1], lp['g_pre'], wc, min(B * C, 1024),
                    min(wc.shape[1], INPROJ_COLS)).reshape(B, C, wc.shape[1])
    ident = _identity_rope(C)
    kv = _kvprep(proj, P_K, rope[0], rope[1], lp['k_gain'], L + C, 0, min(L, 1024))
    k_all, v1_all = _kvprep(projc, P_K if ctx_out else 0, ident[0], ident[1], lp['k_gain'], L + C, L, C, into=kv)

    hy, att = _mixer(proj, k_all, v1_all, C + L, 0, rope, lp, min(L, ATT_TQ))
    x_new = _merge(hy, att, proj, x, mod_x, lp, min(L, 512))
    if not ctx_out:
        return x_new, xc
    hyc, attc = _mixer(projc, k_all, v1_all, C, L // C, ident, lp, C)
    return x_new, _merge(hyc, attc, projc, xc, mod_c, lp, C)


def kernel(x, c, ctx, c_ctx, w_mod, b_mod, g_pre, g_post, w_in, hy_conv_w, hy_conv_b, hy_w1, hy_b1, hy_w2, hy_b2, hy_w3, hy_freq, hy_decay, hy_bias, q_gain, k_gain, gm_ln_g, gm_ln_b, gm_ws, gm_bs, w_hy_o, w_att_o, w_gm_o, w_out):
    B, L, D = x.shape
    rope = _rope_tables(L)

    R = -(-(B + 1) // (2 * SUBLANES)) * (2 * SUBLANES)
    cc = jnp.zeros((R, D), f32).at[:B].set(c).at[B].set(c_ctx)
    mod = _modulation(cc, w_mod, b_mod)

    w_in_p = jnp.concatenate([w_in[:, :, a:b].astype(bf16) for a, b in _COL_RANGES], axis=2)
    gm_bs_full = jnp.repeat(jnp.swapaxes(gm_bs, 1, 2), GM_GROUP_DIM, axis=2)

    xc = ctx
    for i in range(DEPTH):
        lp = {
            'g_pre': g_pre[i], 'g_post': g_post[i], 'w_in': w_in_p[i],
            'hy_conv_w': hy_conv_w[i], 'hy_conv_b': hy_conv_b[i],
            'hy_w1': hy_w1[i], 'hy_b1': hy_b1[i], 'hy_w2': hy_w2[i], 'hy_b2': hy_b2[i],
            'hy_w3': hy_w3[i], 'hy_freq': hy_freq[i], 'hy_decay': hy_decay[i], 'hy_bias': hy_bias[i],
            'q_gain': q_gain[i], 'k_gain': k_gain[i], 'gm_ln_g': gm_ln_g[i], 'gm_ln_b': gm_ln_b[i],
            'gm_ws': gm_ws[i].astype(bf16), 'gm_bs_full': gm_bs_full[i],
            'w_hy_o': w_hy_o[i].astype(bf16), 'w_att_o': w_att_o[i].astype(bf16),
            'w_gm_o': w_gm_o[i].astype(bf16), 'w_out': w_out[i].astype(bf16),
        }
        mod_x = mod[i, :B].reshape(B, 1, 3 * D)
        mod_c = jnp.broadcast_to(mod[i, B].reshape(1, 1, 3 * D), (B, 1, 3 * D))
        x, xc = _layer(x, xc, mod_x, mod_c, rope, lp, i < DEPTH - 1)
    return x
```
